```python
import math
import jax
import jax.numpy as jnp
from jax import lax
import numpy as np

D_MODEL = 2048
BATCH = 4
SEQ = 4096
DEPTH = 2

MLA_HEADS = 16
MLA_Q_LORA = 1536
MLA_KV_LORA = 512
MLA_NOPE = 128
MLA_ROPE = 64
MLA_V = 128
ROPE_THETA = 10000.0

NSA_HEADS = 16
NSA_GROUPS = 2
NSA_HPG = NSA_HEADS // NSA_GROUPS
NSA_DK = 192
NSA_DV = 128
CMP_LEN = 32
CMP_STRIDE = 16
SEL_BLOCK = 64
SEL_COUNT = 16
WINDOW = 512
NSA_BRANCHES = 3
SEL_FORCE = 1.0e4

REL_BUCKETS = 32
REL_MAX_DIST = 128

N_EXPERTS = 32
TOP_K = 4
D_EXPERT = 2048
SWIGLU_LIMIT = 7.0
SWIGLU_ALPHA = 1.702

DEEPNORM_ALPHA = (2 * DEPTH) ** 0.25
DEEPNORM_BETA = (8 * DEPTH) ** -0.25

Q_BLOCK = 128
SEL_Q_BLOCK = 32
EXPERT_ROWS = 128
NEG_INF = -1.0e30
LN_EPS = 1e-5
RMS_EPS = 1e-6

IN_SPLITS = (MLA_Q_LORA, MLA_KV_LORA, MLA_ROPE, NSA_HEADS * NSA_DK,
             NSA_GROUPS * NSA_DK, NSA_GROUPS * NSA_DV,
             NSA_GROUPS * NSA_DK, NSA_GROUPS * NSA_DV,
             NSA_GROUPS * NSA_DK, NSA_GROUPS * NSA_DV,
             NSA_HEADS * NSA_BRANCHES,
             2 * D_MODEL)
D_IN = sum(IN_SPLITS)

kernel_name = 'hybrid_mla_nsa_moe_deepnorm'


def layer_norm(x, g, b):
    xf = x.astype(jnp.float32)
    mu = jnp.mean(xf, axis=-1, keepdims=True)
    var = jnp.mean(jnp.square(xf - mu), axis=-1, keepdims=True)
    return ((xf - mu) * lax.rsqrt(var + LN_EPS) * g + b).astype(x.dtype)


def rms_norm(x, g):
    xf = x.astype(jnp.float32)
    return (xf * lax.rsqrt(jnp.mean(jnp.square(xf), axis=-1, keepdims=True) + RMS_EPS) * g).astype(x.dtype)


def masked_softmax(s, mask):
    s = jnp.where(mask, s.astype(jnp.float32), NEG_INF)
    m = jnp.max(s, axis=-1, keepdims=True)
    p = jnp.exp(s - m) * mask
    return p / jnp.maximum(jnp.sum(p, axis=-1, keepdims=True), 1e-30)


def rope_tables(positions):
    inv = 1.0 / (ROPE_THETA ** (jnp.arange(0, MLA_ROPE, 2, dtype=jnp.float32) / MLA_ROPE))
    ang = positions.astype(jnp.float32)[:, None] * inv[None, :]
    ang = jnp.concatenate([ang, ang], axis=-1)
    return jnp.cos(ang), jnp.sin(ang)


def apply_rope(x, cos, sin):
    x1, x2 = jnp.split(x, 2, axis=-1)
    rot = jnp.concatenate([-x2, x1], axis=-1)
    return x * cos.astype(x.dtype) + rot * sin.astype(x.dtype)


def t5_bucket(dist):
    max_exact = REL_BUCKETS // 2
    d = jnp.maximum(dist, 1).astype(jnp.float32)
    large = max_exact + (jnp.log(d / max_exact) / math.log(REL_MAX_DIST / max_exact)
                         * (REL_BUCKETS - max_exact)).astype(jnp.int32)
    large = jnp.minimum(large, REL_BUCKETS - 1)
    return jnp.where(dist < max_exact, dist, large)


def mla_attention(q_a, kv_a, k_r, cos, sin, q_norm, w_uq, kv_norm, w_ukv):
    B, T, _ = q_a.shape
    q = (rms_norm(q_a, q_norm) @ w_uq).reshape(B, T, MLA_HEADS, MLA_NOPE + MLA_ROPE)
    q_nope = q[..., :MLA_NOPE]
    q_rope = apply_rope(q[..., MLA_NOPE:], cos[None, :, None], sin[None, :, None])
    kv = (rms_norm(kv_a, kv_norm) @ w_ukv).reshape(B, T, MLA_HEADS, MLA_NOPE + MLA_V)
    k_nope, v = kv[..., :MLA_NOPE], kv[..., MLA_NOPE:]
    k_rope = apply_rope(k_r, cos[None], sin[None])
    scale = (MLA_NOPE + MLA_ROPE) ** -0.5
    nb = T // Q_BLOCK
    key_pos = jnp.arange(T)

    def block(args):
        qn, qr, start = args
        s = (jnp.einsum('bqhd,bkhd->bhqk', qn, k_nope)
             + jnp.einsum('bqhr,bkr->bhqk', qr, k_rope)) * scale
        q_pos = start + jnp.arange(Q_BLOCK)
        p = masked_softmax(s, key_pos[None, :] <= q_pos[:, None])
        return jnp.einsum('bhqk,bkhd->bqhd', p.astype(v.dtype), v)

    qn_b = q_nope.reshape(B, nb, Q_BLOCK, MLA_HEADS, MLA_NOPE).swapaxes(0, 1)
    qr_b = q_rope.reshape(B, nb, Q_BLOCK, MLA_HEADS, MLA_ROPE).swapaxes(0, 1)
    o = lax.map(block, (qn_b, qr_b, jnp.arange(nb) * Q_BLOCK))
    return o.swapaxes(0, 1).reshape(B, T, MLA_HEADS * MLA_V)


def nsa_attention(q, k_c, v_c, k_s, v_s, k_w, v_w, gate_logits, bd,
                  pe_k, w1_k, w2_k, pe_v, w1_v, w2_v):
    B, T, _ = q.shape
    G, HPG = NSA_GROUPS, NSA_HPG
    dt = q.dtype
    q = q.reshape(B, T, G, HPG, NSA_DK)
    scale = NSA_DK ** -0.5
    t = np.arange(T)

    n_cmp = (T - CMP_LEN) // CMP_STRIDE + 1
    cmp_idx = np.arange(n_cmp)[:, None] * CMP_STRIDE + np.arange(CMP_LEN)[None, :]

    def compress(kv, pe, w1, w2):
        d = kv.shape[-1] // G
        blocks = kv.reshape(B, T, G, d)[:, cmp_idx] + pe[None, None, :, None, :]
        blocks = blocks.transpose(0, 1, 3, 2, 4).reshape(B, n_cmp, G, CMP_LEN * d)
        return jax.nn.gelu(blocks @ w1) @ w2

    k_cmp = compress(k_c, pe_k, w1_k, w2_k)
    v_cmp = compress(v_c, pe_v, w1_v, w2_v)
    dist_c = t[:, None] - (np.arange(n_cmp) * CMP_STRIDE + CMP_LEN - 1)[None, :]
    s_c = jnp.einsum('btghd,bngd->bghtn', q, k_cmp) * scale + bd[:, :, np.maximum(dist_c, 0)]
    p_cmp = masked_softmax(s_c, dist_c >= 0)
    o_cmp = jnp.einsum('bghtn,bngd->btghd', p_cmp.astype(dt), v_cmp)

    n_slc = T // SEL_BLOCK
    n_sel = min(SEL_COUNT, n_slc)
    cover = (cmp_idx[:, :, None] // SEL_BLOCK == np.arange(n_slc)[None, None, :]).sum(1) / CMP_LEN
    imp = jnp.einsum('bghtn,nj->bgtj', p_cmp, jnp.asarray(cover, jnp.float32))
    cur = t // SEL_BLOCK
    j = np.arange(n_slc)
    forced = (j[None] == 0) | (j[None] == cur[:, None]) | (j[None] == cur[:, None] - 1)
    score = jnp.where(forced, SEL_FORCE, jnp.where(j[None] <= cur[:, None], imp, -1.0))
    top_s, top_j = lax.top_k(score, n_sel)
    top_ok = top_s >= 0.0

    kb = k_s.reshape(B, n_slc, SEL_BLOCK, G, NSA_DK).transpose(0, 3, 1, 2, 4)
    vb = v_s.reshape(B, n_slc, SEL_BLOCK, G, NSA_DV).transpose(0, 3, 1, 2, 4)
    gather = jax.vmap(jax.vmap(lambda blocks, idx: blocks[idx]))
    nq = T // SEL_Q_BLOCK
    g_ix = np.arange(G)[None, :, None, None, None, None]
    h_ix = np.arange(HPG)[None, None, :, None, None, None]
    n_keys = n_sel * SEL_BLOCK

    def sel_block(args):
        qc, jc, okc, start = args
        kg = gather(kb, jc)
        vg = gather(vb, jc)
        s = jnp.einsum('bqghd,bgqnsd->bghqns', qc, kg) * scale
        q_pos = start + jnp.arange(SEL_Q_BLOCK)
        dist = q_pos[None, None, :, None, None] - (jc[..., None] * SEL_BLOCK + jnp.arange(SEL_BLOCK))
        mask = ((dist >= 0) & okc[..., None])[:, :, None]
        s = s + bd[g_ix, h_ix, jnp.maximum(dist, 0)[:, :, None]]
        p = masked_softmax(s.reshape(B, G, HPG, SEL_Q_BLOCK, n_keys),
                           mask.reshape(B, G, 1, SEL_Q_BLOCK, n_keys))
        p = p.reshape(s.shape).astype(vg.dtype)
        return jnp.einsum('bghqns,bgqnsd->bqghd', p, vg)

    q_ch = q.reshape(B, nq, SEL_Q_BLOCK, G, HPG, NSA_DK).swapaxes(0, 1)
    j_ch = top_j.reshape(B, G, nq, SEL_Q_BLOCK, n_sel).transpose(2, 0, 1, 3, 4)
    ok_ch = top_ok.reshape(B, G, nq, SEL_Q_BLOCK, n_sel).transpose(2, 0, 1, 3, 4)
    o_slc = lax.map(sel_block, (q_ch, j_ch, ok_ch, jnp.arange(nq) * SEL_Q_BLOCK))
    o_slc = o_slc.swapaxes(0, 1).reshape(B, T, G, HPG, NSA_DV)

    nw = T // Q_BLOCK
    span = WINDOW + Q_BLOCK
    pad = ((0, 0), (WINDOW, 0), (0, 0))
    kp = jnp.pad(k_w, pad).reshape(B, WINDOW + T, G, NSA_DK)
    vp = jnp.pad(v_w, pad).reshape(B, WINDOW + T, G, NSA_DV)
    ki = np.arange(span)
    dist_w = np.arange(Q_BLOCK)[:, None] + WINDOW - ki[None, :]
    band = (dist_w >= 0) & (dist_w < WINDOW)
    bias_w = bd[:, :, np.clip(dist_w, 0, T - 1)]

    def win_block(args):
        qc, start = args
        kc = lax.dynamic_slice_in_dim(kp, start, span, axis=1)
        vc = lax.dynamic_slice_in_dim(vp, start, span, axis=1)
        s = jnp.einsum('bqghd,bkgd->bghqk', qc, kc) * scale + bias_w
        mask = band & (start - WINDOW + ki[None, :] >= 0)
        p = masked_softmax(s, mask)
        return jnp.einsum('bghqk,bkgd->bqghd', p.astype(vc.dtype), vc)

    q_w = q.reshape(B, nw, Q_BLOCK, G, HPG, NSA_DK).swapaxes(0, 1)
    o_win = lax.map(win_block, (q_w, jnp.arange(nw) * Q_BLOCK))
    o_win = o_win.swapaxes(0, 1).reshape(B, T, G, HPG, NSA_DV)

    g = jax.nn.sigmoid(gate_logits.astype(jnp.float32)).reshape(B, T, G, HPG, NSA_BRANCHES)
    o = g[..., 0:1] * o_cmp + g[..., 1:2] * o_slc + g[..., 2:3] * o_win
    return o.astype(dt).reshape(B, T, NSA_HEADS * NSA_DV)


def token_mixer(x, cos, sin, bd, w_in, q_norm, w_uq, kv_norm, w_ukv,
                pe_k, w1_k, w2_k, pe_v, w1_v, w2_v, w_branch_a, w_branch_b, w_out):
    h = x @ w_in
    (q_a, kv_a, k_r, q_nsa, k_c, v_c, k_s, v_s, k_w, v_w, nsa_gate, merge_gate) = jnp.split(
        h, np.cumsum(IN_SPLITS)[:-1].tolist(), axis=-1)
    y_a = mla_attention(q_a, kv_a, k_r, cos, sin, q_norm, w_uq, kv_norm, w_ukv) @ w_branch_a
    y_b = nsa_attention(q_nsa, k_c, v_c, k_s, v_s, k_w, v_w, nsa_gate, bd,
                        pe_k, w1_k, w2_k, pe_v, w1_v, w2_v) @ w_branch_b
    g_a, g_b = jnp.split(jax.nn.sigmoid(merge_gate.astype(jnp.float32)), 2, axis=-1)
    merged = (g_a * y_a + g_b * y_b).astype(x.dtype)
    return merged @ w_out


def moe_ffn(x, router_w, router_b, w_up, b_up, w_down, b_down):
    B, T, D = x.shape
    n_tok = B * T
    xt = x.reshape(n_tok, D)
    logits = (xt @ router_w).astype(jnp.float32) + router_b.astype(jnp.float32)
    top_v, top_e = lax.top_k(logits, TOP_K)
    gates = jax.nn.softmax(top_v, axis=-1)
    n_rows = n_tok * TOP_K
    e_flat = top_e.reshape(-1)
    tok_flat = jnp.repeat(jnp.arange(n_tok, dtype=jnp.int32), TOP_K)
    order = jnp.argsort(e_flat)
    e_sorted = e_flat[order]
    counts = jnp.bincount(e_flat, length=N_EXPERTS)
    starts = jnp.cumsum(counts) - counts
    padded = (counts + EXPERT_ROWS - 1) // EXPERT_ROWS * EXPERT_ROWS
    pad_end = jnp.cumsum(padded)
    dest = (pad_end - padded)[e_sorted] + jnp.arange(n_rows) - starts[e_sorted]
    n_pad = (n_rows + N_EXPERTS * EXPERT_ROWS + EXPERT_ROWS - 1) // EXPERT_ROWS * EXPERT_ROWS
    row_tok = jnp.zeros((n_pad,), jnp.int32).at[dest].set(tok_flat[order])
    row_w = jnp.zeros((n_pad,), jnp.float32).at[dest].set(gates.reshape(-1)[order])
    n_blk = n_pad // EXPERT_ROWS
    blk_expert = jnp.minimum(jnp.searchsorted(pad_end, jnp.arange(n_blk) * EXPERT_ROWS, side='right'),
                             N_EXPERTS - 1)

    def expert_block(args):
        rows, e = args
        h = xt[rows] @ w_up[e] + b_up[e]
        x_glu = jnp.minimum(h[:, 0::2], SWIGLU_LIMIT)
        x_lin = jnp.clip(h[:, 1::2], -SWIGLU_LIMIT, SWIGLU_LIMIT)
        a = x_glu * jax.nn.sigmoid(SWIGLU_ALPHA * x_glu) * (x_lin + 1.0)
        return a @ w_down[e] + b_down[e]

    out = lax.map(expert_block, (row_tok.reshape(n_blk, EXPERT_ROWS), blk_expert))
    y = jax.ops.segment_sum(out.reshape(n_pad, D) * row_w[:, None].astype(out.dtype), row_tok,
                            num_segments=n_tok)
    return y.reshape(B, T, D)


def setup_inputs(seed: int = 0) -> dict:
    key = jax.random.key(seed)
    ks = jax.random.split(key, 32)
    L, D = DEPTH, D_MODEL

    def nrm(i, shape, scale):
        return jax.random.normal(ks[i], shape, jnp.float32) * scale

    def gain(i, shape):
        return 1.0 + 0.05 * jax.random.normal(ks[i], shape, jnp.float32)

    return {
        'x': nrm(0, (BATCH, SEQ, D), 1.0),
        'positions': jnp.arange(SEQ, dtype=jnp.int32),
        'rel_bias': nrm(1, (REL_BUCKETS, NSA_HEADS), 0.5),
        'w_in': nrm(2, (L, D, D_IN), D ** -0.5),
        'mla_q_norm': gain(3, (L, MLA_Q_LORA)),
        'mla_w_uq': nrm(4, (L, MLA_Q_LORA, MLA_HEADS * (MLA_NOPE + MLA_ROPE)), MLA_Q_LORA ** -0.5),
        'mla_kv_norm': gain(5, (L, MLA_KV_LORA)),
        'mla_w_ukv': nrm(6, (L, MLA_KV_LORA, MLA_HEADS * (MLA_NOPE + MLA_V)), MLA_KV_LORA ** -0.5),
        'cmp_pe_k': nrm(7, (L, CMP_LEN, NSA_DK), 0.1),
        'cmp_w1_k': nrm(8, (L, CMP_LEN * NSA_DK, NSA_DK), (CMP_LEN * NSA_DK) ** -0.5),
        'cmp_w2_k': nrm(9, (L, NSA_DK, NSA_DK), NSA_DK ** -0.5),
        'cmp_pe_v': nrm(10, (L, CMP_LEN, NSA_DV), 0.1),
        'cmp_w1_v': nrm(11, (L, CMP_LEN * NSA_DV, NSA_DV), (CMP_LEN * NSA_DV) ** -0.5),
        'cmp_w2_v': nrm(12, (L, NSA_DV, NSA_DV), NSA_DV ** -0.5),
        'w_branch_a': nrm(13, (L, MLA_HEADS * MLA_V, D), (MLA_HEADS * MLA_V) ** -0.5),
        'w_branch_b': nrm(14, (L, NSA_HEADS * NSA_DV, D), (NSA_HEADS * NSA_DV) ** -0.5),
        'w_out': nrm(15, (L, D, D), DEEPNORM_BETA * D ** -0.5),
        'ln1_g': gain(16, (L, D)),
        'ln1_b': nrm(17, (L, D), 0.02),
        'router_w': nrm(18, (L, D, N_EXPERTS), D ** -0.5),
        'router_b': nrm(19, (L, N_EXPERTS), 0.01),
        'exp_w_up': nrm(20, (L, N_EXPERTS, D, 2 * D_EXPERT), D ** -0.5),
        'exp_b_up': nrm(21, (L, N_EXPERTS, 2 * D_EXPERT), 0.02),
        'exp_w_down': nrm(22, (L, N_EXPERTS, D_EXPERT, D), DEEPNORM_BETA * D_EXPERT ** -0.5),
        'exp_b_down': nrm(23, (L, N_EXPERTS, D), 0.02),
        'ln2_g': gain(24, (L, D)),
        'ln2_b': nrm(25, (L, D), 0.02),
    }


def reference(x, positions, rel_bias, w_in, mla_q_norm, mla_w_uq, mla_kv_norm, mla_w_ukv,
              cmp_pe_k, cmp_w1_k, cmp_w2_k, cmp_pe_v, cmp_w1_v, cmp_w2_v,
              w_branch_a, w_branch_b, w_out, ln1_g, ln1_b, router_w, router_b,
              exp_w_up, exp_b_up, exp_w_down, exp_b_down, ln2_g, ln2_b):
    T = x.shape[1]
    cos, sin = rope_tables(positions)
    bd = rel_bias[t5_bucket(jnp.arange(T))].astype(jnp.float32).T.reshape(NSA_GROUPS, NSA_HPG, T)
    for l in range(DEPTH):
        y = token_mixer(x, cos, sin, bd, w_in[l], mla_q_norm[l], mla_w_uq[l], mla_kv_norm[l], mla_w_ukv[l],
                        cmp_pe_k[l], cmp_w1_k[l], cmp_w2_k[l], cmp_pe_v[l], cmp_w1_v[l], cmp_w2_v[l],
                        w_branch_a[l], w_branch_b[l], w_out[l])
        x = layer_norm(DEEPNORM_ALPHA * x + y, ln1_g[l], ln1_b[l])
        y = moe_ffn(x, router_w[l], router_b[l], exp_w_up[l], exp_b_up[l], exp_w_down[l], exp_b_down[l])
        x = layer_norm(DEEPNORM_ALPHA * x + y, ln2_g[l], ln2_b[l])
    return x
```

```python
import functools
import math

import numpy as np
import jax
import jax.numpy as jnp
from jax import lax
from jax.experimental import pallas as pl
from jax.experimental.pallas import tpu as pltpu

F32, BF16, I32 = jnp.float32, jnp.bfloat16, jnp.int32

MLA_HEADS = 16
MLA_Q_LORA = 1536
MLA_KV_LORA = 512
MLA_NOPE = 128
MLA_ROPE = 64
MLA_V = 128
ROPE_THETA = 10000.0
NSA_HEADS = 16
NSA_GROUPS = 2
NSA_HPG = NSA_HEADS // NSA_GROUPS
NSA_DK = 192
NSA_DV = 128
CMP_LEN = 32
CMP_STRIDE = 16
SEL_BLOCK = 64
SEL_COUNT = 16
WINDOW = 512
NSA_BRANCHES = 3
SEL_FORCE = 1.0e4
REL_BUCKETS = 32
REL_MAX_DIST = 128
TOP_K = 4
SWIGLU_LIMIT = 7.0
SWIGLU_ALPHA = 1.702
NEG_INF = -1.0e30
LN_EPS = 1e-5
RMS_EPS = 1e-6

LANES = 128
HEAD_PAD = 256
NSA_TILE = 128
VMEM_LIMIT_BYTES = 56 * 1024 * 1024


def _pcall(body, *, name, grid, in_specs, out_specs, out_shape, scratch=(), prefetch=0):
    gs = pltpu.PrefetchScalarGridSpec(num_scalar_prefetch=prefetch, grid=grid, in_specs=in_specs,
                                      out_specs=out_specs, scratch_shapes=list(scratch))
    return pl.pallas_call(
        body, grid_spec=gs, out_shape=out_shape, name=name,
        compiler_params=pltpu.CompilerParams(dimension_semantics=("arbitrary",) * len(grid),
                                             vmem_limit_bytes=VMEM_LIMIT_BYTES))


def _sigmoid(x):
    return 1.0 / (1.0 + jnp.exp(-x))


def _dot_t(a, b):
    return lax.dot_general(a, b, (((1,), (1,)), ((), ())), preferred_element_type=F32)


def _dot(a, b):
    return jnp.dot(a, b, preferred_element_type=F32)


def _layer_norm(r, g, b):
    mu = jnp.mean(r, axis=-1, keepdims=True)
    c = r - mu
    var = jnp.mean(c * c, axis=-1, keepdims=True)
    return c * lax.rsqrt(var + LN_EPS) * g + b


def _matmul(name, lhs, w, extras, outs, prologue, epilogue, *, m, k, n, tm, tn):
    nl, ne, no = len(lhs), len(extras), len(outs)

    def body(*refs):
        lr = refs[:nl]
        wr = refs[nl]
        er = refs[nl + 1:nl + 1 + ne]
        orr = refs[nl + 1 + ne:nl + 1 + ne + no]
        xs = refs[-1]

        @pl.when(pl.program_id(1) == 0)
        def _():
            xs[...] = prologue(*lr)

        epilogue(_dot(xs[...], wr[...]), er, orr)

    return _pcall(
        body, name=name, grid=(m // tm, n // tn),
        in_specs=[s for _, s in lhs] + [pl.BlockSpec((k, tn), lambda i, j: (0, j))] + [s for _, s in extras],
        out_specs=[s for _, s in outs], out_shape=[o for o, _ in outs],
        scratch=[pltpu.VMEM((tm, k), BF16)],
    )(*[a for a, _ in lhs], w, *[a for a, _ in extras])


def _row_spec(tm, width, col=0):
    return pl.BlockSpec((tm, width), lambda i, j: (i, col))


def _tile_spec(tm, tn, col_off=0):
    return pl.BlockSpec((tm, tn), lambda i, j: (i, j + col_off))


def _head_spec(tm, hb, width, tpb):
    return pl.BlockSpec((1, hb, tm, width), lambda i, j: (i // tpb, j, i % tpb, 0))


def _cast_prologue(x_ref):
    return x_ref[...].astype(BF16)


def _rms_prologue(x_ref, g_ref):
    xf = x_ref[...].astype(F32)
    ms = jnp.mean(xf * xf, axis=-1, keepdims=True)
    return (xf * lax.rsqrt(ms + RMS_EPS) * g_ref[...]).astype(BF16)


def _rope_slab(t, cos, sin):
    lane = lax.broadcasted_iota(I32, t.shape, 1)
    rot = jnp.where(lane < MLA_ROPE // 2, -pltpu.roll(t, LANES - MLA_ROPE // 2, 1), pltpu.roll(t, MLA_ROPE // 2, 1))
    return t * cos + rot * sin


def _heads_epilogue(hb, width, scale=None):
    def epi(acc, er, orr):
        for c in range(hb):
            v = acc[:, c * width:(c + 1) * width]
            if scale is not None:
                v = v * scale
            orr[0][0, c] = v.astype(BF16)
    return epi


def _mla_attention(q, kv, kr, *, b, t, tq):
    nq = t // tq

    def body(q_ref, kn_ref, kr_ref, v_ref, o_ref, kf_ref):
        i = pl.program_id(2)

        @pl.when(i == 0)
        def _():
            kf_ref[:, :MLA_NOPE] = kn_ref[0, 0]
            kf_ref[:, MLA_NOPE:] = kr_ref[0]

        qv = q_ref[0, 0]

        def step(koff, carry, diagonal):
            m, l, acc = carry
            k = kf_ref[pl.ds(koff, tq), :]
            v = v_ref[0, 0, pl.ds(koff, tq), :]
            s = _dot_t(qv, k)
            if diagonal:
                r = lax.broadcasted_iota(I32, (tq, tq), 0)
                c = lax.broadcasted_iota(I32, (tq, tq), 1)
                s = jnp.where(c <= r, s, NEG_INF)
            m_new = jnp.maximum(m, jnp.max(s, axis=-1, keepdims=True))
            a = jnp.exp(m - m_new)
            p = jnp.exp(s - m_new)
            l = a * l + jnp.sum(p, axis=-1, keepdims=True)
            acc = a * acc + _dot(p.astype(BF16), v)
            return m_new, l, acc

        init = (jnp.full((tq, 1), NEG_INF, F32), jnp.zeros((tq, 1), F32), jnp.zeros((tq, MLA_V), F32))
        carry = lax.fori_loop(0, i, lambda j, c: step(pl.multiple_of(j * tq, tq), c, False), init)
        _, l, acc = step(pl.multiple_of(i * tq, tq), carry, True)
        o_ref[0] = (acc / l).astype(BF16)

    return _pcall(
        body, name="mla_attention", grid=(b, MLA_HEADS, nq),
        in_specs=[
            pl.BlockSpec((1, 1, tq, HEAD_PAD), lambda bi, h, i: (bi, h, i, 0)),
            pl.BlockSpec((1, 1, t, MLA_NOPE), lambda bi, h, i: (bi, h, 0, 0)),
            pl.BlockSpec((1, t, LANES), lambda bi, h, i: (bi, 0, 0)),
            pl.BlockSpec((1, 1, t, MLA_V), lambda bi, h, i: (bi, h, 0, 1)),
        ],
        out_specs=pl.BlockSpec((1, tq, MLA_V), lambda bi, h, i: (bi, i, h)),
        out_shape=jax.ShapeDtypeStruct((b, t, MLA_HEADS * MLA_V), BF16),
        scratch=[pltpu.VMEM((t, HEAD_PAD), BF16)],
    )(q, kv, kr, kv)


def _gelu_tanh(x):
    return 0.5 * x * (1.0 + jnp.tanh(math.sqrt(2.0 / math.pi) * (x + 0.044715 * (x * x * x))))


def _compress(kv6, pe2, w1lo, w1hi, w2, *, b, t, width):
    nb = t // CMP_STRIDE
    half = CMP_STRIDE * width
    h = kv6.reshape(b, kv6.shape[1], nb, half)

    def body(h_ref, pe_ref, lo_ref, hi_ref, w2_ref, o_ref):
        hv = h_ref[0, 0]
        lo, hi = lo_ref[...], hi_ref[...]
        a = _dot(hv, lo)
        bb = _dot(hv, hi)
        pe = pe_ref[...]
        c = _dot(pe, lo)[0:1] + _dot(pe, hi)[1:2]
        z = a + pltpu.roll(bb, nb - 1, 0) + c
        o_ref[0, 0] = _dot(_gelu_tanh(z).astype(BF16), w2_ref[...]).astype(BF16)

    return _pcall(
        body, name=f"nsa_compress_{width}", grid=(b, NSA_GROUPS),
        in_specs=[
            pl.BlockSpec((1, 1, nb, half), lambda bi, g: (bi, g, 0, 0)),
            pl.BlockSpec((8, half), lambda bi, g: (0, 0)),
            pl.BlockSpec((half, width), lambda bi, g: (0, 0)),
            pl.BlockSpec((half, width), lambda bi, g: (0, 0)),
            pl.BlockSpec((width, width), lambda bi, g: (0, 0)),
        ],
        out_specs=pl.BlockSpec((1, 1, nb, width), lambda bi, g: (bi, g, 0, 0)),
        out_shape=jax.ShapeDtypeStruct((b, NSA_GROUPS, nb, width), BF16),
    )(h, pe2, w1lo, w1hi, w2)


def _gate_col(gates, col):
    lane = lax.broadcasted_iota(I32, gates.shape, 1)
    return jnp.sum(jnp.where(lane == col, gates, 0.0), axis=-1, keepdims=True)


def _nsa_cmp(q, kcmp, vcmp, bias_c, cover, gates, *, b, t):
    tq = NSA_TILE
    nb = t // CMP_STRIDE
    n_slc = t // SEL_BLOCK
    n_sel = min(SEL_COUNT, n_slc)
    hpg = NSA_HPG

    def body(q_ref, k_ref, v_ref, bias_ref, cov_ref, g_ref, o_ref, sel_ref):
        g = pl.program_id(1)
        i = pl.program_id(2)
        q2 = q_ref[0].reshape(hpg * tq, HEAD_PAD)
        s3 = _dot_t(q2, k_ref[0, 0]).reshape(hpg, tq, nb) + bias_ref[...]
        t_idx = i * tq + lax.broadcasted_iota(I32, (tq, nb), 0)
        n_idx = lax.broadcasted_iota(I32, (tq, nb), 1)
        valid = (t_idx >= n_idx * CMP_STRIDE + (CMP_LEN - 1))[None]
        s3 = jnp.where(valid, s3, NEG_INF)
        m = jnp.max(s3, axis=-1, keepdims=True)
        p = jnp.where(valid, jnp.exp(s3 - m), 0.0)
        p = p / jnp.maximum(jnp.sum(p, axis=-1, keepdims=True), 1e-30)
        o3 = _dot(p.reshape(hpg * tq, nb).astype(BF16), v_ref[0, 0]).reshape(hpg, tq, NSA_DV)
        gates_v = g_ref[0]
        for hh in range(hpg):
            gc = _gate_col(gates_v, (g * hpg + hh) * NSA_BRANCHES)
            o_ref[0, :, hh * NSA_DV:(hh + 1) * NSA_DV] = (o3[hh] * gc).astype(BF16)

        psum = jnp.sum(p, axis=0)
        p_hi = psum.astype(BF16)
        p_lo = (psum - p_hi.astype(F32)).astype(BF16)
        cov = cov_ref[...]
        imp = _dot(p_hi, cov) + _dot(p_lo, cov)

        jl = lax.broadcasted_iota(I32, (tq, LANES), 1)
        cur = (i * tq + lax.broadcasted_iota(I32, (tq, LANES), 0)) // SEL_BLOCK
        forced = (jl == 0) | (jl == cur) | (jl == cur - 1)
        score = jnp.where(forced, SEL_FORCE, jnp.where(jl <= cur, imp, -1.0))
        rank = jnp.zeros((tq, LANES), F32)
        for kk in range(n_slc):
            col = score[:, kk:kk + 1]
            ahead = (col > score) | ((col == score) & (kk < jl))
            rank = rank + jnp.where(ahead, 1.0, 0.0)
        chosen = (rank < n_sel) & (score >= 0.0) & (jl < n_slc)
        sel_ref[0, 0] = jnp.where(chosen, 1.0, 0.0).astype(BF16)

    return _pcall(
        body, name="nsa_cmp_select", grid=(b, NSA_GROUPS, t // tq),
        in_specs=[
            pl.BlockSpec((1, hpg, tq, HEAD_PAD), lambda bi, g, i: (bi, g, i, 0)),
            pl.BlockSpec((1, 1, nb, HEAD_PAD), lambda bi, g, i: (bi, g, 0, 0)),
            pl.BlockSpec((1, 1, nb, NSA_DV), lambda bi, g, i: (bi, g, 0, 0)),
            pl.BlockSpec((hpg, tq, nb), lambda bi, g, i: (g, i, 0)),
            pl.BlockSpec((nb, LANES), lambda bi, g, i: (0, 0)),
            pl.BlockSpec((1, tq, LANES), lambda bi, g, i: (bi, i, 0)),
        ],
        out_specs=[
            pl.BlockSpec((1, tq, hpg * NSA_DV), lambda bi, g, i: (bi, i, g)),
            pl.BlockSpec((1, 1, tq, LANES), lambda bi, g, i: (bi, g, i, 0)),
        ],
        out_shape=[jax.ShapeDtypeStruct((b, t, NSA_HEADS * NSA_DV), BF16),
                   jax.ShapeDtypeStruct((b, NSA_GROUPS, t, LANES), BF16)],
    )(q, kcmp, vcmp, bias_c, cover, gates)


def _nsa_sparse(q, k6, v6, band, cvec, gates, sel, expand_t, *, b, t, mode):
    tb = NSA_TILE
    hpg = NSA_HPG
    stream = 1 if mode == "sel" else 2
    branch = 1 if mode == "sel" else 2

    def body(*refs):
        if mode == "sel":
            q_ref, k_ref, v_ref, band_ref, cv_ref, g_ref, sel_ref, et_ref, o_ref, m_ref, l_ref, acc_ref = refs
        else:
            q_ref, k_ref, v_ref, band_ref, cv_ref, g_ref, o_ref, m_ref, l_ref, acc_ref = refs
        g = pl.program_id(1)
        i = pl.program_id(2)
        q2 = q_ref[0].reshape(hpg * tb, HEAD_PAD)
        m_ref[...] = jnp.full(m_ref.shape, NEG_INF, F32)
        l_ref[...] = jnp.zeros(l_ref.shape, F32)
        acc_ref[...] = jnp.zeros(acc_ref.shape, F32)
        row = lax.broadcasted_iota(I32, (tb, tb), 0)
        col = lax.broadcasted_iota(I32, (tb, tb), 1)

        def tile(koff, kind):
            k = k_ref[0, 0, pl.ds(koff, tb), :]
            v = v_ref[0, 0, pl.ds(koff, tb), :]
            s3 = _dot_t(q2, k).reshape(hpg, tb, tb)
            mask = None
            if kind == "diag":
                s3 = s3 + band_ref[0]
                mask = col <= row
            elif kind == "near":
                s3 = s3 + band_ref[1]
            else:
                s3 = s3 + cv_ref[0][:, None, :]
                if kind == "edge":
                    mask = col > row
            if mode == "sel":
                chosen = _dot_t(sel_ref[0, 0], et_ref[pl.ds(koff, tb), :]) > 0.5
                mask = chosen if mask is None else (mask & chosen)
            if mask is not None:
                s3 = jnp.where(mask[None], s3, NEG_INF)
            m_old = m_ref[...]
            m_new = jnp.maximum(m_old, jnp.max(s3, axis=-1, keepdims=True))
            a = jnp.exp(m_old - m_new)
            p = jnp.exp(s3 - m_new)
            l_ref[...] = a * l_ref[...] + jnp.sum(p, axis=-1, keepdims=True)
            pv = _dot(p.reshape(hpg * tb, tb).astype(BF16), v).reshape(hpg, tb, NSA_DV)
            acc_ref[...] = a * acc_ref[...] + pv
            m_ref[...] = m_new

        if mode == "sel":
            def far(j, carry):
                tile(pl.multiple_of(j * tb, tb), "far")
                return carry
            lax.fori_loop(0, jnp.maximum(i - 1, 0), far, 0)
            steps = [(1, "near"), (0, "diag")]
        else:
            steps = [(WINDOW // tb, "edge")] + [(d, "far") for d in range(WINDOW // tb - 1, 1, -1)] + [(1, "near"), (0, "diag")]
        for back, kind in steps:
            if back == 0:
                tile(pl.multiple_of(i * tb, tb), kind)
            else:
                @pl.when(i >= back)
                def _(back=back, kind=kind):
                    tile(pl.multiple_of((i - back) * tb, tb), kind)

        gates_v = g_ref[0]
        o3 = acc_ref[...] / l_ref[...]
        for hh in range(hpg):
            gc = _gate_col(gates_v, (g * hpg + hh) * NSA_BRANCHES + branch)
            o_ref[0, :, hh * NSA_DV:(hh + 1) * NSA_DV] = (o3[hh] * gc).astype(BF16)

    in_specs = [
        pl.BlockSpec((1, hpg, tb, HEAD_PAD), lambda bi, g, i: (bi, g, i, 0)),
        pl.BlockSpec((1, 1, t, HEAD_PAD), lambda bi, g, i: (bi, stream * NSA_GROUPS + g, 0, 0)),
        pl.BlockSpec((1, 1, t, NSA_DV), lambda bi, g, i: (bi, stream * NSA_GROUPS + g, 0, 0)),
        pl.BlockSpec((2, hpg, tb, tb), lambda bi, g, i: (0, g, 0, 0)),
        pl.BlockSpec((1, hpg, LANES), lambda bi, g, i: (g, 0, 0)),
        pl.BlockSpec((1, tb, LANES), lambda bi, g, i: (bi, i, 0)),
    ]
    args = [q, k6, v6, band, cvec, gates]
    if mode == "sel":
        in_specs += [pl.BlockSpec((1, 1, tb, LANES), lambda bi, g, i: (bi, g, i, 0)),
                     pl.BlockSpec((t, LANES), lambda bi, g, i: (0, 0))]
        args += [sel, expand_t]
    return _pcall(
        body, name=f"nsa_{mode}_attention", grid=(b, NSA_GROUPS, t // tb),
        in_specs=in_specs,
        out_specs=pl.BlockSpec((1, tb, hpg * NSA_DV), lambda bi, g, i: (bi, i, g)),
        out_shape=jax.ShapeDtypeStruct((b, t, NSA_HEADS * NSA_DV), BF16),
        scratch=[pltpu.VMEM((hpg, tb, 1), F32), pltpu.VMEM((hpg, tb, 1), F32), pltpu.VMEM((hpg, tb, NSA_DV), F32)],
    )(*args)


def _router(x, w_hi, w_lo, bias, *, n, d, n_exp, tm):
    def body(x_ref, wh_ref, wl_ref, b_ref, e_ref, gate_ref):
        xv = x_ref[...]
        x_hi = xv.astype(BF16)
        x_lo = (xv - x_hi.astype(F32)).astype(BF16)
        logits = _dot(x_hi, wh_ref[...]) + _dot(x_lo, wh_ref[...]) + _dot(x_hi, wl_ref[...]) + b_ref[...]
        lane = lax.broadcasted_iota(I32, (tm, LANES), 1)
        logits = jnp.where(lane < n_exp, logits, NEG_INF)
        e_out = jnp.zeros((tm, LANES), I32)
        v_out = jnp.full((tm, LANES), NEG_INF, F32)
        for kk in range(TOP_K):
            mx = jnp.max(logits, axis=-1, keepdims=True)
            idx = jnp.min(jnp.where(logits == mx, lane, LANES), axis=-1, keepdims=True)
            e_out = jnp.where(lane == kk, idx, e_out)
            v_out = jnp.where(lane == kk, mx, v_out)
            logits = jnp.where(lane == idx, NEG_INF, logits)
        ex = jnp.where(lane < TOP_K, jnp.exp(v_out - jnp.max(v_out, axis=-1, keepdims=True)), 0.0)
        e_ref[...] = e_out
        gate_ref[...] = ex / jnp.sum(ex, axis=-1, keepdims=True)

    return _pcall(
        body, name="moe_router", grid=(n // tm,),
        in_specs=[pl.BlockSpec((tm, d), lambda i: (i, 0)),
                  pl.BlockSpec((d, LANES), lambda i: (0, 0)),
                  pl.BlockSpec((d, LANES), lambda i: (0, 0)),
                  pl.BlockSpec((1, LANES), lambda i: (0, 0))],
        out_specs=[pl.BlockSpec((tm, LANES), lambda i: (i, 0)), pl.BlockSpec((tm, LANES), lambda i: (i, 0))],
        out_shape=[jax.ShapeDtypeStruct((n, LANES), I32), jax.ShapeDtypeStruct((n, LANES), F32)],
    )(x, w_hi, w_lo, bias)


def _row_copy(src_hbm, row, dst, slot, sem):
    return pltpu.make_async_copy(src_hbm.at[pl.ds(row, 1)], dst.at[pl.ds(slot, 1)], sem)


def _gather_rows(x, idx3, *, n_rows, d, rows_per_step):
    r_blk = rows_per_step

    def body(idx_ref, x_hbm, o_ref, buf, sem):
        def issue(r, c):
            _row_copy(x_hbm, idx_ref[0, 0, r], buf, r, sem).start()
            return c
        lax.fori_loop(0, r_blk, issue, 0)

        def wait(r, c):
            _row_copy(x_hbm, 0, buf, r, sem).wait()
            return c
        lax.fori_loop(0, r_blk, wait, 0)
        o_ref[...] = buf[...].astype(BF16)

    return _pcall(
        body, name="moe_gather", grid=(n_rows // r_blk,),
        in_specs=[pl.BlockSpec((1, 1, r_blk), lambda i: (i, 0, 0), memory_space=pltpu.SMEM),
                  pl.BlockSpec(memory_space=pl.ANY)],
        out_specs=pl.BlockSpec((r_blk, d), lambda i: (i, 0)),
        out_shape=jax.ShapeDtypeStruct((n_rows, d), BF16),
        scratch=[pltpu.VMEM((r_blk, d), F32), pltpu.SemaphoreType.DMA],
    )(idx3, x)


def _moe_up(blk_e, n_valid, xs, w_glu, w_lin, b_glu, b_lin, *, n_rows, d, f, tm, tf):
    def body(be_ref, nv_ref, x_ref, wg_ref, wl_ref, bg_ref, bl_ref, o_ref):
        i = pl.program_id(1)

        @pl.when(i < nv_ref[0])
        def _():
            xv = x_ref[...]
            h_glu = jnp.minimum(_dot(xv, wg_ref[0]) + bg_ref[0], SWIGLU_LIMIT)
            h_lin = jnp.clip(_dot(xv, wl_ref[0]) + bl_ref[0], -SWIGLU_LIMIT, SWIGLU_LIMIT)
            o_ref[...] = (h_glu * _sigmoid(SWIGLU_ALPHA * h_glu) * (h_lin + 1.0)).astype(BF16)

        @pl.when(i >= nv_ref[0])
        def _():
            o_ref[...] = jnp.zeros(o_ref.shape, BF16)

    return _pcall(
        body, name="moe_up", grid=(f // tf, n_rows // tm), prefetch=2,
        in_specs=[pl.BlockSpec((tm, d), lambda j, i, be, nv: (i, 0)),
                  pl.BlockSpec((1, d, tf), lambda j, i, be, nv: (be[i], 0, j)),
                  pl.BlockSpec((1, d, tf), lambda j, i, be, nv: (be[i], 0, j)),
                  pl.BlockSpec((1, 1, tf), lambda j, i, be, nv: (be[i], 0, j)),
                  pl.BlockSpec((1, 1, tf), lambda j, i, be, nv: (be[i], 0, j))],
        out_specs=pl.BlockSpec((tm, tf), lambda j, i, be, nv: (i, j)),
        out_shape=jax.ShapeDtypeStruct((n_rows, f), BF16),
    )(blk_e, n_valid, xs, w_glu, w_lin, b_glu, b_lin)


def _moe_down(blk_e, n_valid, a, w_down, b_down, *, n_rows, d, f, tm, tn):
    def body(be_ref, nv_ref, a_ref, w_ref, b_ref, o_ref):
        i = pl.program_id(1)

        @pl.when(i < nv_ref[0])
        def _():
            o_ref[...] = _dot(a_ref[...], w_ref[0]) + b_ref[0]

        @pl.when(i >= nv_ref[0])
        def _():
            o_ref[...] = jnp.zeros(o_ref.shape, F32)

    return _pcall(
        body, name="moe_down", grid=(d // tn, n_rows // tm), prefetch=2,
        in_specs=[pl.BlockSpec((tm, f), lambda j, i, be, nv: (i, 0)),
                  pl.BlockSpec((1, f, tn), lambda j, i, be, nv: (be[i], 0, j)),
                  pl.BlockSpec((1, 1, tn), lambda j, i, be, nv: (be[i], 0, j))],
        out_specs=pl.BlockSpec((tm, tn), lambda j, i, be, nv: (i, j)),
        out_shape=jax.ShapeDtypeStruct((n_rows, d), F32),
    )(blk_e, n_valid, a, w_down, b_down)


def _moe_combine(pos3, ys, gates, x, ln_g, ln_b, *, n, d, tq, alpha):
    def body(pos_ref, y_hbm, g_ref, x_ref, lg_ref, lb_ref, of_ref, ob_ref, buf, sem):
        def issue(r, c):
            for kk in range(TOP_K):
                _row_copy(y_hbm, pos_ref[0, 0, kk * tq + r], buf.at[kk], r, sem).start()
            return c
        lax.fori_loop(0, tq, issue, 0)

        def wait(r, c):
            for kk in range(TOP_K):
                _row_copy(y_hbm, 0, buf.at[kk], r, sem).wait()
            return c
        lax.fori_loop(0, tq, wait, 0)
        gv = g_ref[...]
        y = gv[:, 0:1] * buf[0]
        for kk in range(1, TOP_K):
            y = y + gv[:, kk:kk + 1] * buf[kk]
        out = _layer_norm(alpha * x_ref[...] + y, lg_ref[...], lb_ref[...])
        of_ref[...] = out
        ob_ref[...] = out.astype(BF16)

    return _pcall(
        body, name="moe_combine_ln", grid=(n // tq,),
        in_specs=[pl.BlockSpec((1, 1, TOP_K * tq), lambda i: (i, 0, 0), memory_space=pltpu.SMEM),
                  pl.BlockSpec(memory_space=pl.ANY),
                  pl.BlockSpec((tq, LANES), lambda i: (i, 0)),
                  pl.BlockSpec((tq, d), lambda i: (i, 0)),
                  pl.BlockSpec((1, d), lambda i: (0, 0)),
                  pl.BlockSpec((1, d), lambda i: (0, 0))],
        out_specs=[pl.BlockSpec((tq, d), lambda i: (i, 0)), pl.BlockSpec((tq, d), lambda i: (i, 0))],
        out_shape=[jax.ShapeDtypeStruct((n, d), F32), jax.ShapeDtypeStruct((n, d), BF16)],
        scratch=[pltpu.VMEM((TOP_K, tq, d), F32), pltpu.SemaphoreType.DMA],
    )(pos3, ys, gates, x, ln_g, ln_b)


def _t5_bucket(dist):
    max_exact = REL_BUCKETS // 2
    dd = jnp.maximum(dist, 1).astype(F32)
    large = max_exact + (jnp.log(dd / max_exact) / math.log(REL_MAX_DIST / max_exact)
                         * (REL_BUCKETS - max_exact)).astype(I32)
    large = jnp.minimum(large, REL_BUCKETS - 1)
    return jnp.where(dist < max_exact, dist, large)


def _tables(positions, rel_bias, t):
    half = MLA_ROPE // 2
    inv = 1.0 / (ROPE_THETA ** (jnp.arange(0, MLA_ROPE, 2, dtype=F32) / MLA_ROPE))
    ang = positions.astype(F32)[:, None] * inv[None, :]
    zeros = jnp.zeros((t, LANES - MLA_ROPE), F32)
    cos = jnp.concatenate([jnp.cos(ang), jnp.cos(ang), zeros], axis=-1)
    sin = jnp.concatenate([jnp.sin(ang), jnp.sin(ang), zeros], axis=-1)
    assert half * 2 == MLA_ROPE

    bd = rel_bias[_t5_bucket(jnp.arange(t))].astype(F32).T
    tb = NSA_TILE
    rr = np.arange(tb)[:, None] - np.arange(tb)[None, :]
    band = jnp.stack([bd[:, np.maximum(rr, 0)], bd[:, np.minimum(tb + rr, t - 1)]])
    sat = bd[:, min(t - 1, REL_MAX_DIST)]
    cvec = jnp.broadcast_to(sat.reshape(NSA_GROUPS, NSA_HPG, 1), (NSA_GROUPS, NSA_HPG, LANES))
    nb = t // CMP_STRIDE
    dist_c = np.arange(t)[:, None] - (np.arange(nb) * CMP_STRIDE + CMP_LEN - 1)[None, :]
    bias_c = bd[:, np.maximum(dist_c, 0)]

    n_slc = t // SEL_BLOCK
    tok = np.arange(nb)[:, None] * CMP_STRIDE + np.arange(CMP_LEN)[None, :]
    cover = (tok[:, :, None] // SEL_BLOCK == np.arange(LANES)[None, None, :]).sum(1) / CMP_LEN
    cover[nb - 1] = 0.0
    cover[:, n_slc:] = 0.0
    expand_t = (np.arange(t)[:, None] // SEL_BLOCK == np.arange(LANES)[None, :])
    return dict(cos=cos, sin=sin, band=band, cvec=cvec, bias_c=bias_c,
                cover=jnp.asarray(cover, BF16), expand_t=jnp.asarray(expand_t, BF16))


def _pad_heads(w, heads, width):
    k = w.shape[0]
    w = w.reshape(k, heads, width)
    return jnp.pad(w, ((0, 0), (0, 0), (0, HEAD_PAD - width))).reshape(k, heads * HEAD_PAD)


def _pick_tile(total, want):
    tile = min(total, want)
    while total % tile:
        tile //= 2
    return tile


def _token_mixer(x, xb, tabs, p, *, b, t, d):
    n = b * t
    tm = _pick_tile(t, 512)
    tpb = t // tm
    offs = np.cumsum([0, MLA_Q_LORA, MLA_KV_LORA, MLA_ROPE, NSA_HEADS * NSA_DK,
                      NSA_GROUPS * NSA_DK, NSA_GROUPS * NSA_DV, NSA_GROUPS * NSA_DK, NSA_GROUPS * NSA_DV,
                      NSA_GROUPS * NSA_DK, NSA_GROUPS * NSA_DV, NSA_HEADS * NSA_BRANCHES, 2 * d])
    w_in = p["w_in"]
    seg = [w_in[:, offs[s]:offs[s + 1]] for s in range(12)]
    x_lhs = [(xb, _row_spec(tm, d))]
    mla_scale = (MLA_NOPE + MLA_ROPE) ** -0.5
    nsa_scale = NSA_DK ** -0.5

    def cast_epi(acc, er, orr):
        orr[0][...] = acc.astype(BF16)

    lat_w = MLA_Q_LORA + MLA_KV_LORA
    w_lat = jnp.concatenate([seg[0], seg[1]], axis=1).astype(BF16)
    tn_lat = _pick_tile(lat_w, 512)
    (lat,) = _matmul("proj_latents", x_lhs, w_lat, [], [(jax.ShapeDtypeStruct((n, lat_w), BF16), _tile_spec(tm, tn_lat))],
                     _cast_prologue, cast_epi, m=n, k=d, n=lat_w, tm=tm, tn=tn_lat)

    w_small = jnp.concatenate([seg[2], jnp.zeros((d, LANES - MLA_ROPE), F32), seg[10],
                               jnp.zeros((d, LANES - NSA_HEADS * NSA_BRANCHES), F32)], axis=1).astype(BF16)

    def small_epi(acc, er, orr):
        orr[0][...] = _rope_slab(acc[:, :LANES], er[0][...], er[1][...]).astype(BF16)
        orr[1][...] = _sigmoid(acc[:, LANES:])

    cs_extras = [(tabs["cos_n"], _row_spec(tm, LANES)), (tabs["sin_n"], _row_spec(tm, LANES))]
    kr, nsa_gates = _matmul(
        "proj_rope_key_gates", x_lhs, w_small, cs_extras,
        [(jax.ShapeDtypeStruct((n, LANES), BF16), _row_spec(tm, LANES)),
         (jax.ShapeDtypeStruct((n, LANES), F32), _row_spec(tm, LANES))],
        _cast_prologue, small_epi, m=n, k=d, n=2 * LANES, tm=tm, tn=2 * LANES)

    hb = 2
    w_qn = _pad_heads(seg[3], NSA_HEADS, NSA_DK).astype(BF16)
    (q_nsa,) = _matmul("proj_nsa_q", x_lhs, w_qn, [],
                       [(jax.ShapeDtypeStruct((b, NSA_HEADS, t, HEAD_PAD), BF16), _head_spec(tm, hb, HEAD_PAD, tpb))],
                       _cast_prologue, _heads_epilogue(hb, HEAD_PAD, nsa_scale),
                       m=n, k=d, n=NSA_HEADS * HEAD_PAD, tm=tm, tn=hb * HEAD_PAD)

    w_k6 = jnp.concatenate([_pad_heads(seg[s], NSA_GROUPS, NSA_DK) for s in (4, 6, 8)], axis=1).astype(BF16)
    (k6,) = _matmul("proj_nsa_k", x_lhs, w_k6, [],
                    [(jax.ShapeDtypeStruct((b, 3 * NSA_GROUPS, t, HEAD_PAD), BF16), _head_spec(tm, hb, HEAD_PAD, tpb))],
                    _cast_prologue, _heads_epilogue(hb, HEAD_PAD),
                    m=n, k=d, n=3 * NSA_GROUPS * HEAD_PAD, tm=tm, tn=hb * HEAD_PAD)

    w_v6 = jnp.concatenate([seg[s] for s in (5, 7, 9)], axis=1).astype(BF16)
    (v6,) = _matmul("proj_nsa_v", x_lhs, w_v6, [],
                    [(jax.ShapeDtypeStruct((b, 3 * NSA_GROUPS, t, NSA_DV), BF16), _head_spec(tm, hb, NSA_DV, tpb))],
                    _cast_prologue, _heads_epilogue(hb, NSA_DV),
                    m=n, k=d, n=3 * NSA_GROUPS * NSA_DV, tm=tm, tn=hb * NSA_DV)

    def sig_epi(acc, er, orr):
        orr[0][...] = _sigmoid(acc).astype(BF16)

    tn_d = _pick_tile(d, 512)
    (merge,) = _matmul("proj_merge_gates", x_lhs, seg[11].astype(BF16), [],
                       [(jax.ShapeDtypeStruct((n, 2 * d), BF16), _tile_spec(tm, tn_d))],
                       _cast_prologue, sig_epi, m=n, k=d, n=2 * d, tm=tm, tn=tn_d)

    w_uq = p["mla_w_uq"].reshape(MLA_Q_LORA, MLA_HEADS, MLA_NOPE + MLA_ROPE)
    w_uq = jnp.pad(w_uq, ((0, 0), (0, 0), (0, HEAD_PAD - MLA_NOPE - MLA_ROPE))).reshape(MLA_Q_LORA, MLA_HEADS * HEAD_PAD)

    def q_epi(acc, er, orr):
        cos, sin = er[0][...], er[1][...]
        for c in range(hb):
            orr[0][0, c, :, :MLA_NOPE] = (acc[:, c * HEAD_PAD:c * HEAD_PAD + MLA_NOPE] * mla_scale).astype(BF16)
            slab = acc[:, c * HEAD_PAD + MLA_NOPE:(c + 1) * HEAD_PAD] * mla_scale
            orr[0][0, c, :, MLA_NOPE:] = _rope_slab(slab, cos, sin).astype(BF16)

    (q_mla,) = _matmul(
        "mla_q_up", [(lat, _row_spec(tm, MLA_Q_LORA)), (p["mla_q_norm"].reshape(1, -1), pl.BlockSpec((1, MLA_Q_LORA), lambda i, j: (0, 0)))],
        w_uq.astype(BF16), cs_extras,
        [(jax.ShapeDtypeStruct((b, MLA_HEADS, t, HEAD_PAD), BF16), _head_spec(tm, hb, HEAD_PAD, tpb))],
        _rms_prologue, q_epi, m=n, k=MLA_Q_LORA, n=MLA_HEADS * HEAD_PAD, tm=tm, tn=hb * HEAD_PAD)

    (kv_mla,) = _matmul(
        "mla_kv_up", [(lat, _row_spec(tm, MLA_KV_LORA, MLA_Q_LORA // MLA_KV_LORA)),
                      (p["mla_kv_norm"].reshape(1, -1), pl.BlockSpec((1, MLA_KV_LORA), lambda i, j: (0, 0)))],
        p["mla_w_ukv"].astype(BF16), [],
        [(jax.ShapeDtypeStruct((b, MLA_HEADS, t, MLA_NOPE + MLA_V), BF16), _head_spec(tm, hb, MLA_NOPE + MLA_V, tpb))],
        _rms_prologue, _heads_epilogue(hb, MLA_NOPE + MLA_V),
        m=n, k=MLA_KV_LORA, n=MLA_HEADS * (MLA_NOPE + MLA_V), tm=tm, tn=hb * (MLA_NOPE + MLA_V))

    attn_a = _mla_attention(q_mla, kv_mla, kr.reshape(b, t, LANES), b=b, t=t, tq=_pick_tile(t, 512))

    def ya_epi(acc, er, orr):
        orr[0][...] = (er[0][...].astype(F32) * acc).astype(BF16)

    (y_a,) = _matmul("mla_branch_out", [(attn_a.reshape(n, MLA_HEADS * MLA_V), _row_spec(tm, MLA_HEADS * MLA_V))],
                     p["w_branch_a"].astype(BF16), [(merge, _tile_spec(tm, tn_d))],
                     [(jax.ShapeDtypeStruct((n, d), BF16), _tile_spec(tm, tn_d))],
                     _cast_prologue, ya_epi, m=n, k=MLA_HEADS * MLA_V, n=d, tm=tm, tn=tn_d)

    def cmp_weights(pe, w1, w2, width_in, width):
        pe_p = jnp.pad(pe, ((0, 0), (0, width - width_in)))
        pe2 = jnp.zeros((8, CMP_STRIDE * width), F32).at[0].set(pe_p[:CMP_STRIDE].reshape(-1)).at[1].set(pe_p[CMP_STRIDE:].reshape(-1))
        w1p = jnp.pad(w1.reshape(CMP_LEN, width_in, width_in), ((0, 0), (0, width - width_in), (0, width - width_in)))
        w2p = jnp.pad(w2, ((0, width - width_in), (0, width - width_in)))
        return (pe2.astype(BF16), w1p[:CMP_STRIDE].reshape(-1, width).astype(BF16),
                w1p[CMP_STRIDE:].reshape(-1, width).astype(BF16), w2p.astype(BF16))

    kcmp = _compress(k6, *cmp_weights(p["cmp_pe_k"], p["cmp_w1_k"], p["cmp_w2_k"], NSA_DK, HEAD_PAD), b=b, t=t, width=HEAD_PAD)
    vcmp = _compress(v6, *cmp_weights(p["cmp_pe_v"], p["cmp_w1_v"], p["cmp_w2_v"], NSA_DV, NSA_DV), b=b, t=t, width=NSA_DV)
    gates3 = nsa_gates.reshape(b, t, LANES)
    o_cmp, sel = _nsa_cmp(q_nsa, kcmp, vcmp, tabs["bias_c"], tabs["cover"], gates3, b=b, t=t)
    o_slc = _nsa_sparse(q_nsa, k6, v6, tabs["band"], tabs["cvec"], gates3, sel, tabs["expand_t"], b=b, t=t, mode="sel")
    o_win = _nsa_sparse(q_nsa, k6, v6, tabs["band"], tabs["cvec"], gates3, None, None, b=b, t=t, mode="win")

    def sum3_prologue(a_ref, b_ref, c_ref):
        return (a_ref[...].astype(F32) + b_ref[...].astype(F32) + c_ref[...].astype(F32)).astype(BF16)

    def merge_epi(acc, er, orr):
        orr[0][...] = (er[0][...].astype(F32) + er[1][...].astype(F32) * acc).astype(BF16)

    hv = NSA_HEADS * NSA_DV
    (merged,) = _matmul("nsa_branch_out_merge", [(o.reshape(n, hv), _row_spec(tm, hv)) for o in (o_cmp, o_slc, o_win)],
                        p["w_branch_b"].astype(BF16), [(y_a, _tile_spec(tm, tn_d)), (merge, _tile_spec(tm, tn_d, d // tn_d))],
                        [(jax.ShapeDtypeStruct((n, d), BF16), _tile_spec(tm, tn_d))],
                        sum3_prologue, merge_epi, m=n, k=hv, n=d, tm=tm, tn=tn_d)

    tm_ln = _pick_tile(t, 256)
    alpha = p["alpha"]

    def ln_epi(acc, er, orr):
        out = _layer_norm(alpha * er[0][...] + acc, er[1][...], er[2][...])
        orr[0][...] = out
        orr[1][...] = out.astype(BF16)

    vec_spec = pl.BlockSpec((1, d), lambda i, j: (0, 0))
    x1, x1b = _matmul("mixer_out_ln", [(merged, _row_spec(tm_ln, d))], p["w_out"].astype(BF16),
                      [(x, _row_spec(tm_ln, d)), (p["ln1_g"].reshape(1, d), vec_spec), (p["ln1_b"].reshape(1, d), vec_spec)],
                      [(jax.ShapeDtypeStruct((n, d), F32), _row_spec(tm_ln, d)), (jax.ShapeDtypeStruct((n, d), BF16), _row_spec(tm_ln, d))],
                      _cast_prologue, ln_epi, m=n, k=d, n=d, tm=tm_ln, tn=d)
    return x1, x1b


def _moe(x, p, *, n, d):
    n_exp, _, f2 = p["exp_w_up"].shape
    f = f2 // 2
    tm_e = p["tm_e"]
    rw = jnp.pad(p["router_w"], ((0, 0), (0, LANES - n_exp)))
    rw_hi = rw.astype(BF16)
    rw_lo = (rw - rw_hi.astype(F32)).astype(BF16)
    rb = jnp.pad(p["router_b"], (0, LANES - n_exp)).reshape(1, LANES)
    top_e, gates = _router(x, rw_hi, rw_lo, rb, n=n, d=d, n_exp=n_exp, tm=_pick_tile(n, 512))

    n_pairs = n * TOP_K
    e_flat = top_e[:, :TOP_K].reshape(-1)
    onehot = (e_flat[:, None] == jnp.arange(n_exp, dtype=I32)[None, :]).astype(I32)
    csum = jnp.cumsum(onehot, axis=0)
    rank = jnp.take_along_axis(csum, e_flat[:, None], axis=1)[:, 0] - 1
    counts = csum[-1]
    padded = (counts + tm_e - 1) // tm_e * tm_e
    pad_end = jnp.cumsum(padded)
    dest = (pad_end - padded)[e_flat] + rank
    n_rows = n_pairs + n_exp * tm_e
    n_blk = n_rows // tm_e
    row_tok = jnp.zeros((n_rows,), I32).at[dest].set(jnp.arange(n_pairs, dtype=I32) // TOP_K)
    n_valid = (pad_end[-1] // tm_e).astype(I32)
    blk_e = jnp.searchsorted(pad_end, jnp.arange(n_blk, dtype=I32) * tm_e, side="right").astype(I32)
    blk_e = jnp.minimum(blk_e, blk_e[jnp.maximum(n_valid - 1, 0)])
    n_valid = n_valid.reshape(1)

    r_blk = _pick_tile(n_rows, 256)
    xs = _gather_rows(x, row_tok.reshape(n_rows // r_blk, 1, r_blk), n_rows=n_rows, d=d, rows_per_step=r_blk)

    w_up = p["exp_w_up"].reshape(n_exp, d, f, 2)
    b_up = p["exp_b_up"].reshape(n_exp, 1, f, 2)
    tf = _pick_tile(f, 512)
    a = _moe_up(blk_e, n_valid, xs, w_up[..., 0].astype(BF16), w_up[..., 1].astype(BF16), b_up[..., 0], b_up[..., 1],
                n_rows=n_rows, d=d, f=f, tm=tm_e, tf=tf)
    ys = _moe_down(blk_e, n_valid, a, p["exp_w_down"].astype(BF16), p["exp_b_down"].reshape(n_exp, 1, d),
                   n_rows=n_rows, d=d, f=f, tm=tm_e, tn=_pick_tile(d, 512))

    tq = _pick_tile(n, 128)
    pos3 = dest.reshape(n // tq, tq, TOP_K).transpose(0, 2, 1).reshape(n // tq, 1, TOP_K * tq)
    return _moe_combine(pos3, ys, gates, x, p["ln2_g"].reshape(1, d), p["ln2_b"].reshape(1, d),
                        n=n, d=d, tq=tq, alpha=p["alpha"])


def kernel(x, positions, rel_bias, w_in, mla_q_norm, mla_w_uq, mla_kv_norm, mla_w_ukv, cmp_pe_k, cmp_w1_k, cmp_w2_k, cmp_pe_v, cmp_w1_v, cmp_w2_v, w_branch_a, w_branch_b, w_out, ln1_g, ln1_b, router_w, router_b, exp_w_up, exp_b_up, exp_w_down, exp_b_down, ln2_g, ln2_b):
    b, t, d = x.shape
    depth = w_in.shape[0]
    n = b * t
    alpha = (2 * depth) ** 0.25
    tabs = _tables(positions, rel_bias, t)
    tabs["cos_n"] = jnp.tile(tabs["cos"], (b, 1))
    tabs["sin_n"] = jnp.tile(tabs["sin"], (b, 1))
    xf = x.reshape(n, d)
    xb = xf.astype(BF16)
    for l in range(depth):
        p = dict(w_in=w_in[l], mla_q_norm=mla_q_norm[l], mla_w_uq=mla_w_uq[l], mla_kv_norm=mla_kv_norm[l],
                 mla_w_ukv=mla_w_ukv[l], cmp_pe_k=cmp_pe_k[l], cmp_w1_k=cmp_w1_k[l], cmp_w2_k=cmp_w2_k[l],
                 cmp_pe_v=cmp_pe_v[l], cmp_w1_v=cmp_w1_v[l], cmp_w2_v=cmp_w2_v[l], w_branch_a=w_branch_a[l],
                 w_branch_b=w_branch_b[l], w_out=w_out[l], ln1_g=ln1_g[l], ln1_b=ln1_b[l], router_w=router_w[l],
                 router_b=router_b[l], exp_w_up=exp_w_up[l], exp_b_up=exp_b_up[l], exp_w_down=exp_w_down[l],
                 exp_b_down=exp_b_down[l], ln2_g=ln2_g[l], ln2_b=ln2_b[l], alpha=alpha, tm_e=_pick_tile(n * TOP_K, 512))
        x1, _ = _token_mixer(xf, xb, tabs, p, b=b, t=t, d=d)
        xf, xb = _moe(x1, p, n=n, d=d)
    return xf.reshape(b, t, d)
```

```python
import functools
import math

import numpy as np
import jax
import jax.numpy as jnp
from jax import lax
from jax.experimental import pallas as pl
from jax.experimental.pallas import tpu as pltpu

F32, BF16, I32 = jnp.float32, jnp.bfloat16, jnp.int32

MLA_HEADS = 16
MLA_Q_LORA = 1536
MLA_KV_LORA = 512
MLA_NOPE = 128
MLA_ROPE = 64
MLA_V = 128
ROPE_THETA = 10000.0
NSA_HEADS = 16
NSA_GROUPS = 2
NSA_HPG = NSA_HEADS // NSA_GROUPS
NSA_DK = 192
NSA_DV = 128
CMP_LEN = 32
CMP_STRIDE = 16
SEL_BLOCK = 64
SEL_COUNT = 16
WINDOW = 512
NSA_BRANCHES = 3
SEL_FORCE = 1.0e4
REL_BUCKETS = 32
REL_MAX_DIST = 128
TOP_K = 4
SWIGLU_LIMIT = 7.0
SWIGLU_ALPHA = 1.702
NEG_INF = -1.0e30
LN_EPS = 1e-5
RMS_EPS = 1e-6

LANES = 128
HEAD_PAD = 256
NSA_TILE = 128
VMEM_LIMIT_BYTES = 56 * 1024 * 1024


def _pcall(body, *, name, grid, in_specs, out_specs, out_shape, scratch=(), prefetch=0):
    gs = pltpu.PrefetchScalarGridSpec(num_scalar_prefetch=prefetch, grid=grid, in_specs=in_specs,
                                      out_specs=out_specs, scratch_shapes=list(scratch))
    return pl.pallas_call(
        body, grid_spec=gs, out_shape=out_shape, name=name,
        compiler_params=pltpu.CompilerParams(dimension_semantics=("arbitrary",) * len(grid),
                                             vmem_limit_bytes=VMEM_LIMIT_BYTES))


def _sigmoid(x):
    return 1.0 / (1.0 + jnp.exp(-x))


def _dot_t(a, b):
    return lax.dot_general(a, b, (((1,), (1,)), ((), ())), preferred_element_type=F32)


def _dot(a, b):
    return jnp.dot(a, b, preferred_element_type=F32)


def _layer_norm(r, g, b):
    mu = jnp.mean(r, axis=-1, keepdims=True)
    c = r - mu
    var = jnp.mean(c * c, axis=-1, keepdims=True)
    return c * lax.rsqrt(var + LN_EPS) * g + b


def _matmul(name, lhs, w, extras, outs, prologue, epilogue, *, m, k, n, tm, tn):
    nl, ne, no = len(lhs), len(extras), len(outs)

    def body(*refs):
        lr = refs[:nl]
        wr = refs[nl]
        er = refs[nl + 1:nl + 1 + ne]
        orr = refs[nl + 1 + ne:nl + 1 + ne + no]
        xs = refs[-1]

        @pl.when(pl.program_id(1) == 0)
        def _():
            xs[...] = prologue(*lr)

        epilogue(_dot(xs[...], wr[...]), er, orr)

    return _pcall(
        body, name=name, grid=(m // tm, n // tn),
        in_specs=[s for _, s in lhs] + [pl.BlockSpec((k, tn), lambda i, j: (0, j))] + [s for _, s in extras],
        out_specs=[s for _, s in outs], out_shape=[o for o, _ in outs],
        scratch=[pltpu.VMEM((tm, k), BF16)],
    )(*[a for a, _ in lhs], w, *[a for a, _ in extras])


def _row_spec(tm, width, col=0):
    return pl.BlockSpec((tm, width), lambda i, j: (i, col))


def _tile_spec(tm, tn, col_off=0):
    return pl.BlockSpec((tm, tn), lambda i, j: (i, j + col_off))


def _head_spec(tm, hb, width, tpb):
    return pl.BlockSpec((1, hb, tm, width), lambda i, j: (i // tpb, j, i % tpb, 0))


def _cast_prologue(x_ref):
    return x_ref[...].astype(BF16)


def _rms_prologue(x_ref, g_ref):
    xf = x_ref[...].astype(F32)
    ms = jnp.mean(xf * xf, axis=-1, keepdims=True)
    return (xf * lax.rsqrt(ms + RMS_EPS) * g_ref[...]).astype(BF16)


def _rope_slab(t, cos, sin):
    lane = lax.broadcasted_iota(I32, t.shape, 1)
    rot = jnp.where(lane < MLA_ROPE // 2, -pltpu.roll(t, LANES - MLA_ROPE // 2, 1), pltpu.roll(t, MLA_ROPE // 2, 1))
    return t * cos + rot * sin


def _heads_epilogue(hb, width, scale=None):
    def epi(acc, er, orr):
        for c in range(hb):
            v = acc[:, c * width:(c + 1) * width]
            if scale is not None:
                v = v * scale
            orr[0][0, c] = v.astype(BF16)
    return epi


def _flash_update(s, v_ones, m_ref, acc_ref):
    m_old = m_ref[...]
    m_new = jnp.maximum(m_old, jnp.max(s, axis=-1, keepdims=True))
    p = jnp.exp(s - m_new)
    acc_ref[...] = jnp.exp(m_old - m_new) * acc_ref[...] + _dot(p.astype(BF16), v_ones)
    m_ref[...] = m_new


def _mla_attention(q, k, v_ones, causal, *, b, t, tq):
    nq = t // tq

    def body(q_ref, k_ref, v_ref, c_ref, o_ref, m_ref, acc_ref):
        i = pl.program_id(2)
        qv = q_ref[0, 0]
        m_ref[...] = jnp.full(m_ref.shape, NEG_INF, F32)
        acc_ref[...] = jnp.zeros(acc_ref.shape, F32)

        def step(koff, diagonal):
            s = _dot_t(qv, k_ref[0, 0, pl.ds(koff, tq), :])
            if diagonal:
                s = s + c_ref[...]
            _flash_update(s, v_ref[0, 0, pl.ds(koff, tq), :], m_ref, acc_ref)

        def far(j, carry):
            step(pl.multiple_of(j * tq, tq), False)
            return carry
        lax.fori_loop(0, i, far, 0)
        step(pl.multiple_of(i * tq, tq), True)
        acc = acc_ref[...]
        o_ref[0] = (acc[:, :MLA_V] / acc[:, MLA_V:MLA_V + 1]).astype(BF16)

    kv_spec = pl.BlockSpec((1, 1, t, HEAD_PAD), lambda bi, h, i: (bi, h, 0, 0))
    return _pcall(
        body, name="mla_attention", grid=(b, MLA_HEADS, nq),
        in_specs=[pl.BlockSpec((1, 1, tq, HEAD_PAD), lambda bi, h, i: (bi, h, i, 0)), kv_spec, kv_spec,
                  pl.BlockSpec((tq, tq), lambda bi, h, i: (0, 0))],
        out_specs=pl.BlockSpec((1, tq, MLA_V), lambda bi, h, i: (bi, i, h)),
        out_shape=jax.ShapeDtypeStruct((b, t, MLA_HEADS * MLA_V), BF16),
        scratch=[pltpu.VMEM((tq, 1), F32), pltpu.VMEM((tq, HEAD_PAD), F32)],
    )(q, k, v_ones, causal)


def _gelu_tanh(x):
    return 0.5 * x * (1.0 + jnp.tanh(math.sqrt(2.0 / math.pi) * (x + 0.044715 * (x * x * x))))


def _compress(kv6, pe2, w1lo, w1hi, w2, *, b, t, width):
    nb = t // CMP_STRIDE
    half = CMP_STRIDE * width
    h = kv6.reshape(b, kv6.shape[1], nb, half)

    def body(h_ref, pe_ref, lo_ref, hi_ref, w2_ref, o_ref):
        hv = h_ref[0, 0]
        lo, hi = lo_ref[...], hi_ref[...]
        a = _dot(hv, lo)
        bb = _dot(hv, hi)
        pe = pe_ref[...]
        c = _dot(pe, lo)[0:1] + _dot(pe, hi)[1:2]
        z = a + pltpu.roll(bb, nb - 1, 0) + c
        o_ref[0, 0] = _dot(_gelu_tanh(z).astype(BF16), w2_ref[...]).astype(BF16)

    return _pcall(
        body, name=f"nsa_compress_{width}", grid=(b, NSA_GROUPS),
        in_specs=[
            pl.BlockSpec((1, 1, nb, half), lambda bi, g: (bi, g, 0, 0)),
            pl.BlockSpec((8, half), lambda bi, g: (0, 0)),
            pl.BlockSpec((half, width), lambda bi, g: (0, 0)),
            pl.BlockSpec((half, width), lambda bi, g: (0, 0)),
            pl.BlockSpec((width, width), lambda bi, g: (0, 0)),
        ],
        out_specs=pl.BlockSpec((1, 1, nb, width), lambda bi, g: (bi, g, 0, 0)),
        out_shape=jax.ShapeDtypeStruct((b, NSA_GROUPS, nb, width), BF16),
    )(h, pe2, w1lo, w1hi, w2)


def _gate_col(gates, col):
    lane = lax.broadcasted_iota(I32, gates.shape, 1)
    return jnp.sum(jnp.where(lane == col, gates, 0.0), axis=-1, keepdims=True)


def _nsa_cmp(q, kcmp, vcmp, bias_c, cover, gates, *, b, t):
    tq = NSA_TILE
    nb = t // CMP_STRIDE
    n_slc = t // SEL_BLOCK
    n_sel = min(SEL_COUNT, n_slc)
    hpg = NSA_HPG

    def body(q_ref, k_ref, v_ref, bias_ref, cov_ref, g_ref, o_ref, sel_ref):
        g = pl.program_id(1)
        i = pl.program_id(2)
        q2 = q_ref[0].reshape(hpg * tq, HEAD_PAD)
        s3 = _dot_t(q2, k_ref[0, 0]).reshape(hpg, tq, nb) + bias_ref[...]
        p = jnp.exp(s3 - jnp.max(s3, axis=-1, keepdims=True))
        has_key = jnp.where(i * tq + lax.broadcasted_iota(I32, (tq, 1), 0) >= CMP_LEN - 1, 1.0, 0.0)
        p = p * (has_key / jnp.maximum(jnp.sum(p, axis=-1, keepdims=True), 1e-30))
        o3 = _dot(p.reshape(hpg * tq, nb).astype(BF16), v_ref[0, 0]).reshape(hpg, tq, NSA_DV)
        _write_gated_heads(o_ref, o3, g_ref[0], g, 0)

        psum = jnp.sum(p, axis=0)
        p_hi = psum.astype(BF16)
        p_lo = (psum - p_hi.astype(F32)).astype(BF16)
        cov = cov_ref[...]
        imp = _dot(p_hi, cov) + _dot(p_lo, cov)

        jl = lax.broadcasted_iota(I32, (tq, LANES), 1)
        cur = (i * tq + lax.broadcasted_iota(I32, (tq, LANES), 0)) // SEL_BLOCK
        forced = (jl == 0) | (jl == cur) | (jl == cur - 1)
        score = jnp.where(forced, SEL_FORCE, jnp.where(jl <= cur, imp, -1.0))
        rank = jnp.zeros((tq, LANES), F32)
        for kk in range(n_slc):
            col = score[:, kk:kk + 1]
            ahead = (col > score) | ((col == score) & (kk < jl))
            rank = rank + jnp.where(ahead, 1.0, 0.0)
        chosen = (rank < n_sel) & (score >= 0.0) & (jl < n_slc)
        sel_ref[0, 0] = jnp.where(chosen, 1.0, 0.0).astype(BF16)

    return _pcall(
        body, name="nsa_cmp_select", grid=(b, NSA_GROUPS, t // tq),
        in_specs=[
            pl.BlockSpec((1, hpg, tq, HEAD_PAD), lambda bi, g, i: (bi, g, i, 0)),
            pl.BlockSpec((1, 1, nb, HEAD_PAD), lambda bi, g, i: (bi, g, 0, 0)),
            pl.BlockSpec((1, 1, nb, NSA_DV), lambda bi, g, i: (bi, g, 0, 0)),
            pl.BlockSpec((hpg, tq, nb), lambda bi, g, i: (g, i, 0)),
            pl.BlockSpec((nb, LANES), lambda bi, g, i: (0, 0)),
            pl.BlockSpec((1, tq, LANES), lambda bi, g, i: (bi, i, 0)),
        ],
        out_specs=[
            pl.BlockSpec((1, tq, hpg * NSA_DV), lambda bi, g, i: (bi, i, g)),
            pl.BlockSpec((1, 1, tq, LANES), lambda bi, g, i: (bi, g, i, 0)),
        ],
        out_shape=[jax.ShapeDtypeStruct((b, t, NSA_HEADS * NSA_DV), BF16),
                   jax.ShapeDtypeStruct((b, NSA_GROUPS, t, LANES), BF16)],
    )(q, kcmp, vcmp, bias_c, cover, gates)


def _write_gated_heads(o_ref, o3, gates_v, g, branch):
    for hh in range(NSA_HPG):
        gc = _gate_col(gates_v, (g * NSA_HPG + hh) * NSA_BRANCHES + branch)
        o_ref[0, :, hh * NSA_DV:(hh + 1) * NSA_DV] = (o3[hh] * gc).astype(BF16)


def _nsa_sel(q, k6, v4, stab, cvec, gates, sel, expand_t, *, b, t):
    tb = NSA_TILE
    hpg = NSA_HPG
    wide = 4 * tb
    span = WINDOW + tb
    back = WINDOW // tb
    rows = hpg * tb

    def body(q_ref, k_ref, v_ref, tab_ref, cv_ref, g_ref, sel_ref, et_ref, o_ref, m_ref, acc_ref):
        g = pl.program_id(1)
        i = pl.program_id(2)
        q2 = q_ref[0].reshape(rows, HEAD_PAD)
        sel_v = sel_ref[0, 0]
        m_ref[...] = jnp.full(m_ref.shape, NEG_INF, F32)
        acc_ref[...] = jnp.zeros(acc_ref.shape, F32)
        far_end = jnp.maximum(i - back, 0) * tb

        def far(j, carry):
            koff = pl.multiple_of(j * wide, wide)
            s3 = _dot_t(q2, k_ref[0, 0, pl.ds(koff, wide), :]).reshape(hpg, tb, wide) + cv_ref[0][:, None, :]
            key = koff + lax.broadcasted_iota(I32, (tb, wide), 1)
            chosen = jnp.where(key < far_end, _dot_t(sel_v, et_ref[pl.ds(koff, wide), :]), 0.0) > 0.5
            s3 = jnp.where(chosen[None], s3, NEG_INF)
            _flash_update(s3.reshape(rows, wide), v_ref[0, 0, pl.ds(koff, wide), :], m_ref, acc_ref)
            return carry

        lax.fori_loop(0, (far_end + wide - 1) // wide, far, 0)

        koff = pl.multiple_of(far_end, tb)
        s3 = _dot_t(q2, k_ref[0, 0, pl.ds(koff, span), :]).reshape(hpg, tb, span) + tab_ref[0]
        chosen = _dot_t(sel_v, et_ref[pl.ds(koff, span), :]) > 0.5
        s3 = jnp.where(chosen[None], s3, NEG_INF)
        _flash_update(s3.reshape(rows, span), v_ref[0, 0, pl.ds(koff, span), :], m_ref, acc_ref)

        acc = acc_ref[...]
        o3 = (acc[:, :NSA_DV] / acc[:, NSA_DV:NSA_DV + 1]).reshape(hpg, tb, NSA_DV)
        _write_gated_heads(o_ref, o3, g_ref[0], g, 1)

    return _pcall(
        body, name="nsa_sel_attention", grid=(b, NSA_GROUPS, t // tb),
        in_specs=[
            pl.BlockSpec((1, hpg, tb, HEAD_PAD), lambda bi, g, i: (bi, g, i, 0)),
            pl.BlockSpec((1, 1, t, HEAD_PAD), lambda bi, g, i: (bi, NSA_GROUPS + g, 0, 0)),
            pl.BlockSpec((1, 1, t, HEAD_PAD), lambda bi, g, i: (bi, g, 0, 0)),
            pl.BlockSpec((1, hpg, tb, span), lambda bi, g, i: (jnp.minimum(i, back), g, 0, 0)),
            pl.BlockSpec((1, hpg, wide), lambda bi, g, i: (g, 0, 0)),
            pl.BlockSpec((1, tb, LANES), lambda bi, g, i: (bi, i, 0)),
            pl.BlockSpec((1, 1, tb, LANES), lambda bi, g, i: (bi, g, i, 0)),
            pl.BlockSpec((t, LANES), lambda bi, g, i: (0, 0)),
        ],
        out_specs=pl.BlockSpec((1, tb, hpg * NSA_DV), lambda bi, g, i: (bi, i, g)),
        out_shape=jax.ShapeDtypeStruct((b, t, NSA_HEADS * NSA_DV), BF16),
        scratch=[pltpu.VMEM((rows, 1), F32), pltpu.VMEM((rows, HEAD_PAD), F32)],
    )(q, k6, v4, stab, cvec, gates, sel, expand_t)


def _nsa_win(q, k6, v4, wtab, gates, *, b, t):
    tb = NSA_TILE
    hpg = NSA_HPG
    span = WINDOW + tb
    back = WINDOW // tb
    rows = hpg * tb

    def body(q_ref, k_ref, v_ref, w_ref, g_ref, o_ref):
        g = pl.program_id(1)
        i = pl.program_id(2)
        koff = pl.multiple_of(jnp.maximum(i - back, 0) * tb, tb)
        q2 = q_ref[0].reshape(rows, HEAD_PAD)
        s3 = _dot_t(q2, k_ref[0, 0, pl.ds(koff, span), :]).reshape(hpg, tb, span) + w_ref[0]
        s = s3.reshape(rows, span)
        p = jnp.exp(s - jnp.max(s, axis=-1, keepdims=True))
        acc = _dot(p.astype(BF16), v_ref[0, 0, pl.ds(koff, span), :])
        o3 = (acc[:, :NSA_DV] / acc[:, NSA_DV:NSA_DV + 1]).reshape(hpg, tb, NSA_DV)
        _write_gated_heads(o_ref, o3, g_ref[0], g, 2)

    return _pcall(
        body, name="nsa_win_attention", grid=(b, NSA_GROUPS, t // tb),
        in_specs=[
            pl.BlockSpec((1, hpg, tb, HEAD_PAD), lambda bi, g, i: (bi, g, i, 0)),
            pl.BlockSpec((1, 1, t, HEAD_PAD), lambda bi, g, i: (bi, 2 * NSA_GROUPS + g, 0, 0)),
            pl.BlockSpec((1, 1, t, HEAD_PAD), lambda bi, g, i: (bi, NSA_GROUPS + g, 0, 0)),
            pl.BlockSpec((1, hpg, tb, span), lambda bi, g, i: (jnp.minimum(i, back), g, 0, 0)),
            pl.BlockSpec((1, tb, LANES), lambda bi, g, i: (bi, i, 0)),
        ],
        out_specs=pl.BlockSpec((1, tb, hpg * NSA_DV), lambda bi, g, i: (bi, i, g)),
        out_shape=jax.ShapeDtypeStruct((b, t, NSA_HEADS * NSA_DV), BF16),
    )(q, k6, v4, wtab, gates)


def _router(x, w_hi, w_lo, bias, *, n, d, n_exp, tm):
    def body(x_ref, wh_ref, wl_ref, b_ref, e_ref, gate_ref):
        xv = x_ref[...]
        x_hi = xv.astype(BF16)
        x_lo = (xv - x_hi.astype(F32)).astype(BF16)
        logits = _dot(x_hi, wh_ref[...]) + _dot(x_lo, wh_ref[...]) + _dot(x_hi, wl_ref[...]) + b_ref[...]
        lane = lax.broadcasted_iota(I32, (tm, LANES), 1)
        logits = jnp.where(lane < n_exp, logits, NEG_INF)
        e_out = jnp.zeros((tm, LANES), I32)
        v_out = jnp.full((tm, LANES), NEG_INF, F32)
        for kk in range(TOP_K):
            mx = jnp.max(logits, axis=-1, keepdims=True)
            idx = jnp.min(jnp.where(logits == mx, lane, LANES), axis=-1, keepdims=True)
            e_out = jnp.where(lane == kk, idx, e_out)
            v_out = jnp.where(lane == kk, mx, v_out)
            logits = jnp.where(lane == idx, NEG_INF, logits)
        ex = jnp.where(lane < TOP_K, jnp.exp(v_out - jnp.max(v_out, axis=-1, keepdims=True)), 0.0)
        e_ref[...] = e_out
        gate_ref[...] = ex / jnp.sum(ex, axis=-1, keepdims=True)

    return _pcall(
        body, name="moe_router", grid=(n // tm,),
        in_specs=[pl.BlockSpec((tm, d), lambda i: (i, 0)),
                  pl.BlockSpec((d, LANES), lambda i: (0, 0)),
                  pl.BlockSpec((d, LANES), lambda i: (0, 0)),
                  pl.BlockSpec((1, LANES), lambda i: (0, 0))],
        out_specs=[pl.BlockSpec((tm, LANES), lambda i: (i, 0)), pl.BlockSpec((tm, LANES), lambda i: (i, 0))],
        out_shape=[jax.ShapeDtypeStruct((n, LANES), I32), jax.ShapeDtypeStruct((n, LANES), F32)],
    )(x, w_hi, w_lo, bias)


def _row_copy(src_hbm, row, dst, slot, sem, chunks):
    src = src_hbm.at[pl.ds(pl.multiple_of(row * chunks, chunks), chunks)]
    return pltpu.make_async_copy(src, dst.at[pl.ds(pl.multiple_of(slot * chunks, chunks), chunks)], sem)


def _gather_rows(x_chunks, idx3, *, n_rows, d, rows_per_step):
    r_blk = rows_per_step
    chunks = d // LANES

    def body(idx_ref, x_hbm, o_ref, buf, sem):
        def issue(r, c):
            _row_copy(x_hbm, idx_ref[0, 0, r], buf, r, sem, chunks).start()
            return c
        lax.fori_loop(0, r_blk, issue, 0)

        def wait(r, c):
            _row_copy(x_hbm, 0, buf, r, sem, chunks).wait()
            return c
        lax.fori_loop(0, r_blk, wait, 0)
        for c in range(chunks):
            o_ref[:, c * LANES:(c + 1) * LANES] = buf[pl.ds(c, r_blk, stride=chunks), :].astype(BF16)

    return _pcall(
        body, name="moe_gather", grid=(n_rows // r_blk,),
        in_specs=[pl.BlockSpec((1, 1, r_blk), lambda i: (i, 0, 0), memory_space=pltpu.SMEM),
                  pl.BlockSpec(memory_space=pl.ANY)],
        out_specs=pl.BlockSpec((r_blk, d), lambda i: (i, 0)),
        out_shape=jax.ShapeDtypeStruct((n_rows, d), BF16),
        scratch=[pltpu.VMEM((r_blk * chunks, LANES), F32), pltpu.SemaphoreType.DMA],
    )(idx3, x_chunks)


def _moe_up(blk_e, n_valid, xs, w_glu, w_lin, b_glu, b_lin, *, n_rows, d, f, tm, tf):
    def body(be_ref, nv_ref, x_ref, wg_ref, wl_ref, bg_ref, bl_ref, o_ref):
        i = pl.program_id(1)

        @pl.when(i < nv_ref[0])
        def _():
            xv = x_ref[...]
            h_glu = jnp.minimum(_dot(xv, wg_ref[0]) + bg_ref[0], SWIGLU_LIMIT)
            h_lin = jnp.clip(_dot(xv, wl_ref[0]) + bl_ref[0], -SWIGLU_LIMIT, SWIGLU_LIMIT)
            o_ref[...] = (h_glu * _sigmoid(SWIGLU_ALPHA * h_glu) * (h_lin + 1.0)).astype(BF16)

        @pl.when(i >= nv_ref[0])
        def _():
            o_ref[...] = jnp.zeros(o_ref.shape, BF16)

    return _pcall(
        body, name="moe_up", grid=(f // tf, n_rows // tm), prefetch=2,
        in_specs=[pl.BlockSpec((tm, d), lambda j, i, be, nv: (i, 0)),
                  pl.BlockSpec((1, d, tf), lambda j, i, be, nv: (be[i], 0, j)),
                  pl.BlockSpec((1, d, tf), lambda j, i, be, nv: (be[i], 0, j)),
                  pl.BlockSpec((1, 1, tf), lambda j, i, be, nv: (be[i], 0, j)),
                  pl.BlockSpec((1, 1, tf), lambda j, i, be, nv: (be[i], 0, j))],
        out_specs=pl.BlockSpec((tm, tf), lambda j, i, be, nv: (i, j)),
        out_shape=jax.ShapeDtypeStruct((n_rows, f), BF16),
    )(blk_e, n_valid, xs, w_glu, w_lin, b_glu, b_lin)


def _moe_down(blk_e, n_valid, a, w_down, b_down, *, n_rows, d, f, tm):
    chunks = d // LANES

    def body(be_ref, nv_ref, a_ref, w_ref, b_ref, o_ref):
        i = pl.program_id(0)

        @pl.when(i < nv_ref[0])
        def _():
            res = _dot(a_ref[...], w_ref[0]) + b_ref[0]
            for c in range(chunks):
                o_ref[pl.ds(c, tm, stride=chunks), :] = res[:, c * LANES:(c + 1) * LANES]

        @pl.when(i >= nv_ref[0])
        def _():
            o_ref[...] = jnp.zeros(o_ref.shape, F32)

    return _pcall(
        body, name="moe_down", grid=(n_rows // tm,), prefetch=2,
        in_specs=[pl.BlockSpec((tm, f), lambda i, be, nv: (i, 0)),
                  pl.BlockSpec((1, f, d), lambda i, be, nv: (be[i], 0, 0)),
                  pl.BlockSpec((1, 1, d), lambda i, be, nv: (be[i], 0, 0))],
        out_specs=pl.BlockSpec((tm * chunks, LANES), lambda i, be, nv: (i, 0)),
        out_shape=jax.ShapeDtypeStruct((n_rows * chunks, LANES), F32),
    )(blk_e, n_valid, a, w_down, b_down)


def _moe_combine(pos3, ys, gates, x, ln_g, ln_b, *, n, d, tq, alpha):
    chunks = d // LANES

    def body(pos_ref, y_hbm, g_ref, x_ref, lg_ref, lb_ref, of_ref, ob_ref, buf, y_ref, sem):
        def issue(r, c):
            for kk in range(TOP_K):
                _row_copy(y_hbm, pos_ref[0, 0, kk * tq + r], buf.at[kk], r, sem, chunks).start()
            return c
        lax.fori_loop(0, tq, issue, 0)

        def wait(r, c):
            for kk in range(TOP_K):
                _row_copy(y_hbm, 0, buf.at[kk], r, sem, chunks).wait()
            return c
        lax.fori_loop(0, tq, wait, 0)
        gv = g_ref[...]
        for c in range(chunks):
            piece = gv[:, 0:1] * buf[0, pl.ds(c, tq, stride=chunks), :]
            for kk in range(1, TOP_K):
                piece = piece + gv[:, kk:kk + 1] * buf[kk, pl.ds(c, tq, stride=chunks), :]
            y_ref[:, c * LANES:(c + 1) * LANES] = piece
        out = _layer_norm(alpha * x_ref[...] + y_ref[...], lg_ref[...], lb_ref[...])
        of_ref[...] = out
        ob_ref[...] = out.astype(BF16)

    return _pcall(
        body, name="moe_combine_ln", grid=(n // tq,),
        in_specs=[pl.BlockSpec((1, 1, TOP_K * tq), lambda i: (i, 0, 0), memory_space=pltpu.SMEM),
                  pl.BlockSpec(memory_space=pl.ANY),
                  pl.BlockSpec((tq, LANES), lambda i: (i, 0)),
                  pl.BlockSpec((tq, d), lambda i: (i, 0)),
                  pl.BlockSpec((1, d), lambda i: (0, 0)),
                  pl.BlockSpec((1, d), lambda i: (0, 0))],
        out_specs=[pl.BlockSpec((tq, d), lambda i: (i, 0)), pl.BlockSpec((tq, d), lambda i: (i, 0))],
        out_shape=[jax.ShapeDtypeStruct((n, d), F32), jax.ShapeDtypeStruct((n, d), BF16)],
        scratch=[pltpu.VMEM((TOP_K, tq * chunks, LANES), F32), pltpu.VMEM((tq, d), F32), pltpu.SemaphoreType.DMA],
    )(pos3, ys, gates, x, ln_g, ln_b)


def _t5_bucket(dist):
    max_exact = REL_BUCKETS // 2
    dd = jnp.maximum(dist, 1).astype(F32)
    large = max_exact + (jnp.log(dd / max_exact) / math.log(REL_MAX_DIST / max_exact)
                         * (REL_BUCKETS - max_exact)).astype(I32)
    large = jnp.minimum(large, REL_BUCKETS - 1)
    return jnp.where(dist < max_exact, dist, large)


def _tables(positions, rel_bias, t):
    half = MLA_ROPE // 2
    inv = 1.0 / (ROPE_THETA ** (jnp.arange(0, MLA_ROPE, 2, dtype=F32) / MLA_ROPE))
    ang = positions.astype(F32)[:, None] * inv[None, :]
    zeros = jnp.zeros((t, LANES - MLA_ROPE), F32)
    cos = jnp.concatenate([jnp.cos(ang), jnp.cos(ang), zeros], axis=-1)
    sin = jnp.concatenate([jnp.sin(ang), jnp.sin(ang), zeros], axis=-1)
    assert half * 2 == MLA_ROPE

    first = jnp.searchsorted(_t5_bucket(jnp.arange(t)), jnp.arange(REL_BUCKETS), side="left").astype(I32)
    rb = rel_bias.astype(F32)

    def bias_of(dist, valid):
        out = jnp.broadcast_to(rb[0].reshape((NSA_HEADS,) + (1,) * dist.ndim), (NSA_HEADS,) + dist.shape)
        for bkt in range(1, REL_BUCKETS):
            out = jnp.where((dist >= first[bkt])[None], rb[bkt].reshape((NSA_HEADS,) + (1,) * dist.ndim), out)
        return jnp.where(valid[None], out, NEG_INF)

    tb = NSA_TILE
    sat = rb[REL_BUCKETS - 1]
    cvec = jnp.broadcast_to(sat.reshape(NSA_GROUPS, NSA_HPG, 1), (NSA_GROUPS, NSA_HPG, 4 * tb))
    span = WINDOW + tb
    wd = (jnp.arange(WINDOW // tb + 1, dtype=I32) * tb)[:, None, None] + jnp.arange(tb, dtype=I32)[None, :, None] \
        - jnp.arange(span, dtype=I32)[None, None, :]
    wtab = bias_of(wd, (wd >= 0) & (wd < WINDOW)).transpose(1, 0, 2, 3)
    stab = bias_of(wd, wd >= 0).transpose(1, 0, 2, 3)
    nb = t // CMP_STRIDE
    dist_c = jnp.arange(t, dtype=I32)[:, None] - (jnp.arange(nb, dtype=I32) * CMP_STRIDE + CMP_LEN - 1)[None, :]
    bias_c = bias_of(dist_c, dist_c >= 0)
    tq_mla = _pick_tile(t, 512)
    cm = jnp.arange(tq_mla, dtype=I32)
    causal = jnp.where(cm[None, :] <= cm[:, None], 0.0, NEG_INF).astype(F32)

    n_slc = t // SEL_BLOCK
    tok = np.arange(nb)[:, None] * CMP_STRIDE + np.arange(CMP_LEN)[None, :]
    cover = (tok[:, :, None] // SEL_BLOCK == np.arange(LANES)[None, None, :]).sum(1) / CMP_LEN
    cover[nb - 1] = 0.0
    cover[:, n_slc:] = 0.0
    expand_t = (np.arange(t)[:, None] // SEL_BLOCK == np.arange(LANES)[None, :])
    return dict(cos=cos, sin=sin, stab=stab, cvec=cvec, wtab=wtab, bias_c=bias_c, causal=causal,
                cover=jnp.asarray(cover, BF16), expand_t=jnp.asarray(expand_t, BF16))


def _pad_heads(w, heads, width):
    k = w.shape[0]
    w = w.reshape(k, heads, width)
    return jnp.pad(w, ((0, 0), (0, 0), (0, HEAD_PAD - width))).reshape(k, heads * HEAD_PAD)


def _pick_tile(total, want):
    tile = min(total, want)
    while total % tile:
        tile //= 2
    return tile


def _token_mixer(x, xb, tabs, p, *, b, t, d):
    n = b * t
    tm = _pick_tile(t, 512)
    tpb = t // tm
    offs = np.cumsum([0, MLA_Q_LORA, MLA_KV_LORA, MLA_ROPE, NSA_HEADS * NSA_DK,
                      NSA_GROUPS * NSA_DK, NSA_GROUPS * NSA_DV, NSA_GROUPS * NSA_DK, NSA_GROUPS * NSA_DV,
                      NSA_GROUPS * NSA_DK, NSA_GROUPS * NSA_DV, NSA_HEADS * NSA_BRANCHES, 2 * d])
    w_in = p["w_in"]
    seg = [w_in[:, offs[s]:offs[s + 1]] for s in range(12)]
    x_lhs = [(xb, _row_spec(tm, d))]
    mla_scale = (MLA_NOPE + MLA_ROPE) ** -0.5
    nsa_scale = NSA_DK ** -0.5

    def cast_epi(acc, er, orr):
        orr[0][...] = acc.astype(BF16)

    lat_w = MLA_Q_LORA + MLA_KV_LORA
    w_lat = jnp.concatenate([seg[0], seg[1]], axis=1).astype(BF16)
    tn_lat = _pick_tile(lat_w, 512)
    (lat,) = _matmul("proj_latents", x_lhs, w_lat, [], [(jax.ShapeDtypeStruct((n, lat_w), BF16), _tile_spec(tm, tn_lat))],
                     _cast_prologue, cast_epi, m=n, k=d, n=lat_w, tm=tm, tn=tn_lat)

    w_small = jnp.concatenate([seg[2], jnp.zeros((d, LANES - MLA_ROPE), F32), seg[10],
                               jnp.zeros((d, LANES - NSA_HEADS * NSA_BRANCHES), F32)], axis=1).astype(BF16)

    def small_epi(acc, er, orr):
        orr[0][...] = _rope_slab(acc[:, :LANES], er[0][...], er[1][...]).astype(BF16)
        orr[1][...] = _sigmoid(acc[:, LANES:])

    cs_extras = [(tabs["cos_n"], _row_spec(tm, LANES)), (tabs["sin_n"], _row_spec(tm, LANES))]
    kr, nsa_gates = _matmul(
        "proj_rope_key_gates", x_lhs, w_small, cs_extras,
        [(jax.ShapeDtypeStruct((n, LANES), BF16), _row_spec(tm, LANES)),
         (jax.ShapeDtypeStruct((n, LANES), F32), _row_spec(tm, LANES))],
        _cast_prologue, small_epi, m=n, k=d, n=2 * LANES, tm=tm, tn=2 * LANES)

    hb = 2
    w_qn = _pad_heads(seg[3], NSA_HEADS, NSA_DK).astype(BF16)
    (q_nsa,) = _matmul("proj_nsa_q", x_lhs, w_qn, [],
                       [(jax.ShapeDtypeStruct((b, NSA_HEADS, t, HEAD_PAD), BF16), _head_spec(tm, hb, HEAD_PAD, tpb))],
                       _cast_prologue, _heads_epilogue(hb, HEAD_PAD, nsa_scale),
                       m=n, k=d, n=NSA_HEADS * HEAD_PAD, tm=tm, tn=hb * HEAD_PAD)

    w_k6 = jnp.concatenate([_pad_heads(seg[s], NSA_GROUPS, NSA_DK) for s in (4, 6, 8)], axis=1).astype(BF16)
    (k6,) = _matmul("proj_nsa_k", x_lhs, w_k6, [],
                    [(jax.ShapeDtypeStruct((b, 3 * NSA_GROUPS, t, HEAD_PAD), BF16), _head_spec(tm, hb, HEAD_PAD, tpb))],
                    _cast_prologue, _heads_epilogue(hb, HEAD_PAD),
                    m=n, k=d, n=3 * NSA_GROUPS * HEAD_PAD, tm=tm, tn=hb * HEAD_PAD)

    (v_c,) = _matmul("proj_nsa_v_cmp", x_lhs, seg[5].astype(BF16), [],
                     [(jax.ShapeDtypeStruct((b, NSA_GROUPS, t, NSA_DV), BF16), _head_spec(tm, hb, NSA_DV, tpb))],
                     _cast_prologue, _heads_epilogue(hb, NSA_DV),
                     m=n, k=d, n=NSA_GROUPS * NSA_DV, tm=tm, tn=hb * NSA_DV)

    def ones_lane(rows):
        return jnp.where(lax.broadcasted_iota(I32, (rows, HEAD_PAD - NSA_DV), 1) == 0, 1.0, 0.0).astype(BF16)

    def v_ones_epi(acc, er, orr):
        for c in range(hb):
            orr[0][0, c, :, :NSA_DV] = acc[:, c * NSA_DV:(c + 1) * NSA_DV].astype(BF16)
            orr[0][0, c, :, NSA_DV:] = ones_lane(acc.shape[0])

    w_v4 = jnp.concatenate([seg[7], seg[9]], axis=1).astype(BF16)
    (v4,) = _matmul("proj_nsa_v", x_lhs, w_v4, [],
                    [(jax.ShapeDtypeStruct((b, 2 * NSA_GROUPS, t, HEAD_PAD), BF16), _head_spec(tm, hb, HEAD_PAD, tpb))],
                    _cast_prologue, v_ones_epi, m=n, k=d, n=2 * NSA_GROUPS * NSA_DV, tm=tm, tn=hb * NSA_DV)

    def sig_epi(acc, er, orr):
        orr[0][...] = _sigmoid(acc).astype(BF16)

    tn_d = _pick_tile(d, 512)
    (merge,) = _matmul("proj_merge_gates", x_lhs, seg[11].astype(BF16), [],
                       [(jax.ShapeDtypeStruct((n, 2 * d), BF16), _tile_spec(tm, tn_d))],
                       _cast_prologue, sig_epi, m=n, k=d, n=2 * d, tm=tm, tn=tn_d)

    w_uq = p["mla_w_uq"].reshape(MLA_Q_LORA, MLA_HEADS, MLA_NOPE + MLA_ROPE)
    w_uq = jnp.pad(w_uq, ((0, 0), (0, 0), (0, HEAD_PAD - MLA_NOPE - MLA_ROPE))).reshape(MLA_Q_LORA, MLA_HEADS * HEAD_PAD)

    def q_epi(acc, er, orr):
        cos, sin = er[0][...], er[1][...]
        for c in range(hb):
            orr[0][0, c, :, :MLA_NOPE] = (acc[:, c * HEAD_PAD:c * HEAD_PAD + MLA_NOPE] * mla_scale).astype(BF16)
            slab = acc[:, c * HEAD_PAD + MLA_NOPE:(c + 1) * HEAD_PAD] * mla_scale
            orr[0][0, c, :, MLA_NOPE:] = _rope_slab(slab, cos, sin).astype(BF16)

    (q_mla,) = _matmul(
        "mla_q_up", [(lat, _row_spec(tm, MLA_Q_LORA)), (p["mla_q_norm"].reshape(1, -1), pl.BlockSpec((1, MLA_Q_LORA), lambda i, j: (0, 0)))],
        w_uq.astype(BF16), cs_extras,
        [(jax.ShapeDtypeStruct((b, MLA_HEADS, t, HEAD_PAD), BF16), _head_spec(tm, hb, HEAD_PAD, tpb))],
        _rms_prologue, q_epi, m=n, k=MLA_Q_LORA, n=MLA_HEADS * HEAD_PAD, tm=tm, tn=hb * HEAD_PAD)

    kv_w = MLA_NOPE + MLA_V

    def kv_epi(acc, er, orr):
        k_rope = er[0][...]
        for c in range(hb):
            orr[0][0, c, :, :MLA_NOPE] = acc[:, c * kv_w:c * kv_w + MLA_NOPE].astype(BF16)
            orr[0][0, c, :, MLA_NOPE:] = k_rope
            orr[1][0, c, :, :MLA_V] = acc[:, c * kv_w + MLA_NOPE:(c + 1) * kv_w].astype(BF16)
            orr[1][0, c, :, MLA_V:] = ones_lane(acc.shape[0])

    k_mla, v_mla = _matmul(
        "mla_kv_up", [(lat, _row_spec(tm, MLA_KV_LORA, MLA_Q_LORA // MLA_KV_LORA)),
                      (p["mla_kv_norm"].reshape(1, -1), pl.BlockSpec((1, MLA_KV_LORA), lambda i, j: (0, 0)))],
        p["mla_w_ukv"].astype(BF16), [(kr, _row_spec(tm, LANES))],
        [(jax.ShapeDtypeStruct((b, MLA_HEADS, t, HEAD_PAD), BF16), _head_spec(tm, hb, HEAD_PAD, tpb)),
         (jax.ShapeDtypeStruct((b, MLA_HEADS, t, HEAD_PAD), BF16), _head_spec(tm, hb, HEAD_PAD, tpb))],
        _rms_prologue, kv_epi, m=n, k=MLA_KV_LORA, n=MLA_HEADS * kv_w, tm=tm, tn=hb * kv_w)

    attn_a = _mla_attention(q_mla, k_mla, v_mla, tabs["causal"], b=b, t=t, tq=tabs["causal"].shape[0])

    def ya_epi(acc, er, orr):
        orr[0][...] = (er[0][...].astype(F32) * acc).astype(BF16)

    (y_a,) = _matmul("mla_branch_out", [(attn_a.reshape(n, MLA_HEADS * MLA_V), _row_spec(tm, MLA_HEADS * MLA_V))],
                     p["w_branch_a"].astype(BF16), [(merge, _tile_spec(tm, tn_d))],
                     [(jax.ShapeDtypeStruct((n, d), BF16), _tile_spec(tm, tn_d))],
                     _cast_prologue, ya_epi, m=n, k=MLA_HEADS * MLA_V, n=d, tm=tm, tn=tn_d)

    def cmp_weights(pe, w1, w2, width_in, width):
        pe_p = jnp.pad(pe, ((0, 0), (0, width - width_in)))
        pe2 = jnp.zeros((8, CMP_STRIDE * width), F32).at[0].set(pe_p[:CMP_STRIDE].reshape(-1)).at[1].set(pe_p[CMP_STRIDE:].reshape(-1))
        w1p = jnp.pad(w1.reshape(CMP_LEN, width_in, width_in), ((0, 0), (0, width - width_in), (0, width - width_in)))
        w2p = jnp.pad(w2, ((0, width - width_in), (0, width - width_in)))
        return (pe2.astype(BF16), w1p[:CMP_STRIDE].reshape(-1, width).astype(BF16),
                w1p[CMP_STRIDE:].reshape(-1, width).astype(BF16), w2p.astype(BF16))

    kcmp = _compress(k6, *cmp_weights(p["cmp_pe_k"], p["cmp_w1_k"], p["cmp_w2_k"], NSA_DK, HEAD_PAD), b=b, t=t, width=HEAD_PAD)
    vcmp = _compress(v_c, *cmp_weights(p["cmp_pe_v"], p["cmp_w1_v"], p["cmp_w2_v"], NSA_DV, NSA_DV), b=b, t=t, width=NSA_DV)
    gates3 = nsa_gates.reshape(b, t, LANES)
    o_cmp, sel = _nsa_cmp(q_nsa, kcmp, vcmp, tabs["bias_c"], tabs["cover"], gates3, b=b, t=t)
    o_slc = _nsa_sel(q_nsa, k6, v4, tabs["stab"], tabs["cvec"], gates3, sel, tabs["expand_t"], b=b, t=t)
    o_win = _nsa_win(q_nsa, k6, v4, tabs["wtab"], gates3, b=b, t=t)

    def sum3_prologue(a_ref, b_ref, c_ref):
        return (a_ref[...].astype(F32) + b_ref[...].astype(F32) + c_ref[...].astype(F32)).astype(BF16)

    def merge_epi(acc, er, orr):
        orr[0][...] = (er[0][...].astype(F32) + er[1][...].astype(F32) * acc).astype(BF16)

    hv = NSA_HEADS * NSA_DV
    (merged,) = _matmul("nsa_branch_out_merge", [(o.reshape(n, hv), _row_spec(tm, hv)) for o in (o_cmp, o_slc, o_win)],
                        p["w_branch_b"].astype(BF16), [(y_a, _tile_spec(tm, tn_d)), (merge, _tile_spec(tm, tn_d, d // tn_d))],
                        [(jax.ShapeDtypeStruct((n, d), BF16), _tile_spec(tm, tn_d))],
                        sum3_prologue, merge_epi, m=n, k=hv, n=d, tm=tm, tn=tn_d)

    tm_ln = _pick_tile(t, 256)
    alpha = p["alpha"]

    def ln_epi(acc, er, orr):
        out = _layer_norm(alpha * er[0][...] + acc, er[1][...], er[2][...])
        orr[0][...] = out
        orr[1][...] = out.astype(BF16)

    vec_spec = pl.BlockSpec((1, d), lambda i, j: (0, 0))
    x1, x1b = _matmul("mixer_out_ln", [(merged, _row_spec(tm_ln, d))], p["w_out"].astype(BF16),
                      [(x, _row_spec(tm_ln, d)), (p["ln1_g"].reshape(1, d), vec_spec), (p["ln1_b"].reshape(1, d), vec_spec)],
                      [(jax.ShapeDtypeStruct((n, d), F32), _row_spec(tm_ln, d)), (jax.ShapeDtypeStruct((n, d), BF16), _row_spec(tm_ln, d))],
                      _cast_prologue, ln_epi, m=n, k=d, n=d, tm=tm_ln, tn=d)
    return x1, x1b


def _moe(x, p, *, n, d):
    n_exp, _, f2 = p["exp_w_up"].shape
    f = f2 // 2
    tm_e = p["tm_e"]
    rw = jnp.pad(p["router_w"], ((0, 0), (0, LANES - n_exp)))
    rw_hi = rw.astype(BF16)
    rw_lo = (rw - rw_hi.astype(F32)).astype(BF16)
    rb = jnp.pad(p["router_b"], (0, LANES - n_exp)).reshape(1, LANES)
    top_e, gates = _router(x, rw_hi, rw_lo, rb, n=n, d=d, n_exp=n_exp, tm=_pick_tile(n, 512))

    n_pairs = n * TOP_K
    e_flat = top_e[:, :TOP_K].reshape(-1)
    onehot = (e_flat[:, None] == jnp.arange(n_exp, dtype=I32)[None, :]).astype(I32)
    csum = jnp.cumsum(onehot, axis=0)
    rank = jnp.take_along_axis(csum, e_flat[:, None], axis=1)[:, 0] - 1
    counts = csum[-1]
    padded = (counts + tm_e - 1) // tm_e * tm_e
    pad_end = jnp.cumsum(padded)
    dest = (pad_end - padded)[e_flat] + rank
    n_rows = n_pairs + n_exp * tm_e
    n_blk = n_rows // tm_e
    row_tok = jnp.zeros((n_rows,), I32).at[dest].set(jnp.arange(n_pairs, dtype=I32) // TOP_K)
    n_valid = (pad_end[-1] // tm_e).astype(I32)
    blk_e = jnp.searchsorted(pad_end, jnp.arange(n_blk, dtype=I32) * tm_e, side="right").astype(I32)
    blk_e = jnp.minimum(blk_e, blk_e[jnp.maximum(n_valid - 1, 0)])
    n_valid = n_valid.reshape(1)

    r_blk = _pick_tile(n_rows, 256)
    xs = _gather_rows(x.reshape(n * (d // LANES), LANES), row_tok.reshape(n_rows // r_blk, 1, r_blk),
                      n_rows=n_rows, d=d, rows_per_step=r_blk)

    w_up = p["exp_w_up"].reshape(n_exp, d, f, 2)
    b_up = p["exp_b_up"].reshape(n_exp, 1, f, 2)
    tf = _pick_tile(f, 512)
    a = _moe_up(blk_e, n_valid, xs, w_up[..., 0].astype(BF16), w_up[..., 1].astype(BF16), b_up[..., 0], b_up[..., 1],
                n_rows=n_rows, d=d, f=f, tm=tm_e, tf=tf)
    ys = _moe_down(blk_e, n_valid, a, p["exp_w_down"].astype(BF16), p["exp_b_down"].reshape(n_exp, 1, d),
                   n_rows=n_rows, d=d, f=f, tm=tm_e)

    tq = _pick_tile(n, 128)
    pos3 = dest.reshape(n // tq, tq, TOP_K).transpose(0, 2, 1).reshape(n // tq, 1, TOP_K * tq)
    return _moe_combine(pos3, ys, gates, x, p["ln2_g"].reshape(1, d), p["ln2_b"].reshape(1, d),
                        n=n, d=d, tq=tq, alpha=p["alpha"])


def kernel(x, positions, rel_bias, w_in, mla_q_norm, mla_w_uq, mla_kv_norm, mla_w_ukv, cmp_pe_k, cmp_w1_k, cmp_w2_k, cmp_pe_v, cmp_w1_v, cmp_w2_v, w_branch_a, w_branch_b, w_out, ln1_g, ln1_b, router_w, router_b, exp_w_up, exp_b_up, exp_w_down, exp_b_down, ln2_g, ln2_b):
    b, t, d = x.shape
    depth = w_in.shape[0]
    n = b * t
    alpha = (2 * depth) ** 0.25
    tabs = _tables(positions, rel_bias, t)
    tabs["cos_n"] = jnp.tile(tabs["cos"], (b, 1))
    tabs["sin_n"] = jnp.tile(tabs["sin"], (b, 1))
    xf = x.reshape(n, d)
    xb = xf.astype(BF16)
    for l in range(depth):
        p = dict(w_in=w_in[l], mla_q_norm=mla_q_norm[l], mla_w_uq=mla_w_uq[l], mla_kv_norm=mla_kv_norm[l],
                 mla_w_ukv=mla_w_ukv[l], cmp_pe_k=cmp_pe_k[l], cmp_w1_k=cmp_w1_k[l], cmp_w2_k=cmp_w2_k[l],
                 cmp_pe_v=cmp_pe_v[l], cmp_w1_v=cmp_w1_v[l], cmp_w2_v=cmp_w2_v[l], w_branch_a=w_branch_a[l],
                 w_branch_b=w_branch_b[l], w_out=w_out[l], ln1_g=ln1_g[l], ln1_b=ln1_b[l], router_w=router_w[l],
                 router_b=router_b[l], exp_w_up=exp_w_up[l], exp_b_up=exp_b_up[l], exp_w_down=exp_w_down[l],
                 exp_b_down=exp_b_down[l], ln2_g=ln2_g[l], ln2_b=ln2_b[l], alpha=alpha, tm_e=_pick_tile(n * TOP_K, 512))
        x1, _ = _token_mixer(xf, xb, tabs, p, b=b, t=t, d=d)
        xf, xb = _moe(x1, p, n=n, d=d)
    return xf.reshape(b, t, d)
```

```python
import functools
import math

import numpy as np
import jax
import jax.numpy as jnp
from jax import lax
from jax.experimental import pallas as pl
from jax.experimental.pallas import tpu as pltpu

F32, BF16, I32 = jnp.float32, jnp.bfloat16, jnp.int32

MLA_HEADS = 16
MLA_Q_LORA = 1536
MLA_KV_LORA = 512
MLA_NOPE = 128
MLA_ROPE = 64
MLA_V = 128
ROPE_THETA = 10000.0
NSA_HEADS = 16
NSA_GROUPS = 2
NSA_HPG = NSA_HEADS // NSA_GROUPS
NSA_DK = 192
NSA_DV = 128
CMP_LEN = 32
CMP_STRIDE = 16
SEL_BLOCK = 64
SEL_COUNT = 16
WINDOW = 512
NSA_BRANCHES = 3
SEL_FORCE = 1.0e4
REL_BUCKETS = 32
REL_MAX_DIST = 128
TOP_K = 4
SWIGLU_LIMIT = 7.0
SWIGLU_ALPHA = 1.702
NEG_INF = -1.0e30
LN_EPS = 1e-5
RMS_EPS = 1e-6

LANES = 128
HEAD_PAD = 256
NSA_TILE = 128
GATHER_UNROLL = 8
VMEM_LIMIT_BYTES = 56 * 1024 * 1024


def _pcall(body, *, name, grid, in_specs, out_specs, out_shape, scratch=(), prefetch=0):
    gs = pltpu.PrefetchScalarGridSpec(num_scalar_prefetch=prefetch, grid=grid, in_specs=in_specs,
                                      out_specs=out_specs, scratch_shapes=list(scratch))
    return pl.pallas_call(
        body, grid_spec=gs, out_shape=out_shape, name=name,
        compiler_params=pltpu.CompilerParams(dimension_semantics=("arbitrary",) * len(grid),
                                             vmem_limit_bytes=VMEM_LIMIT_BYTES))


def _sigmoid(x):
    return 1.0 / (1.0 + jnp.exp(-x))


def _dot_t(a, b):
    return lax.dot_general(a, b, (((1,), (1,)), ((), ())), preferred_element_type=F32)


def _dot(a, b):
    return jnp.dot(a, b, preferred_element_type=F32)


def _layer_norm(r, g, b):
    mu = jnp.mean(r, axis=-1, keepdims=True)
    c = r - mu
    var = jnp.mean(c * c, axis=-1, keepdims=True)
    return c * lax.rsqrt(var + LN_EPS) * g + b


def _matmul(name, lhs, w, extras, outs, prologue, epilogue, *, m, k, n, tm, tn):
    nl, ne, no = len(lhs), len(extras), len(outs)

    def body(*refs):
        lr = refs[:nl]
        wr = refs[nl]
        er = refs[nl + 1:nl + 1 + ne]
        orr = refs[nl + 1 + ne:nl + 1 + ne + no]
        xs = refs[-1]

        @pl.when(pl.program_id(1) == 0)
        def _():
            xs[...] = prologue(*lr)

        epilogue(_dot(xs[...], wr[...]), er, orr)

    return _pcall(
        body, name=name, grid=(m // tm, n // tn),
        in_specs=[s for _, s in lhs] + [pl.BlockSpec((k, tn), lambda i, j: (0, j))] + [s for _, s in extras],
        out_specs=[s for _, s in outs], out_shape=[o for o, _ in outs],
        scratch=[pltpu.VMEM((tm, k), BF16)],
    )(*[a for a, _ in lhs], w, *[a for a, _ in extras])


def _row_spec(tm, width, col=0):
    return pl.BlockSpec((tm, width), lambda i, j: (i, col))


def _tile_spec(tm, tn, col_off=0):
    return pl.BlockSpec((tm, tn), lambda i, j: (i, j + col_off))


def _head_spec(tm, hb, width, tpb):
    return pl.BlockSpec((1, hb, tm, width), lambda i, j: (i // tpb, j, i % tpb, 0))


def _cast_prologue(x_ref):
    return x_ref[...].astype(BF16)


def _rms_prologue(x_ref, g_ref):
    xf = x_ref[...].astype(F32)
    ms = jnp.mean(xf * xf, axis=-1, keepdims=True)
    return (xf * lax.rsqrt(ms + RMS_EPS) * g_ref[...]).astype(BF16)


def _rope_slab(t, cos, sin):
    lane = lax.broadcasted_iota(I32, t.shape, 1)
    rot = jnp.where(lane < MLA_ROPE // 2, -pltpu.roll(t, LANES - MLA_ROPE // 2, 1), pltpu.roll(t, MLA_ROPE // 2, 1))
    return t * cos + rot * sin


def _heads_epilogue(hb, width, scale=None):
    def epi(acc, er, orr):
        for c in range(hb):
            v = acc[:, c * width:(c + 1) * width]
            if scale is not None:
                v = v * scale
            orr[0][0, c] = v.astype(BF16)
    return epi


def _softmax_init(rows, width):
    return (jnp.full((rows, 1), NEG_INF, F32), jnp.zeros((rows, 1), F32), jnp.zeros((rows, width), F32))


def _softmax_step(s, v, carry):
    m, l, acc = carry
    m_new = jnp.maximum(m, jnp.max(s, axis=-1, keepdims=True))
    a = jnp.exp(m - m_new)
    p = jnp.exp(s - m_new)
    return m_new, a * l + jnp.sum(p, axis=-1, keepdims=True), a * acc + _dot(p.astype(BF16), v)


def _mla_attention(q, k, v_ones, causal, *, b, t, tq):
    nq = t // tq

    def body(q_ref, k_ref, v_ref, c_ref, o_ref):
        i = pl.program_id(2)
        qv = q_ref[0, 0]

        def step(koff, carry, diagonal):
            s = _dot_t(qv, k_ref[0, 0, pl.ds(koff, tq), :])
            if diagonal:
                s = s + c_ref[...]
            return _softmax_step(s, v_ref[0, 0, pl.ds(koff, tq), :], carry)

        carry = lax.fori_loop(0, i, lambda j, c: step(pl.multiple_of(j * tq, tq), c, False), _softmax_init(tq, MLA_V))
        _, l, acc = step(pl.multiple_of(i * tq, tq), carry, True)
        o_ref[0] = (acc / l).astype(BF16)

    return _pcall(
        body, name="mla_attention", grid=(b, MLA_HEADS, nq),
        in_specs=[pl.BlockSpec((1, 1, tq, HEAD_PAD), lambda bi, h, i: (bi, h, i, 0)),
                  pl.BlockSpec((1, 1, t, HEAD_PAD), lambda bi, h, i: (bi, h, 0, 0)),
                  pl.BlockSpec((1, 1, t, MLA_V), lambda bi, h, i: (bi, h, 0, 0)),
                  pl.BlockSpec((tq, tq), lambda bi, h, i: (0, 0))],
        out_specs=pl.BlockSpec((1, tq, MLA_V), lambda bi, h, i: (bi, i, h)),
        out_shape=jax.ShapeDtypeStruct((b, t, MLA_HEADS * MLA_V), BF16),
    )(q, k, v_ones, causal)


def _gelu_tanh(x):
    return 0.5 * x * (1.0 + jnp.tanh(math.sqrt(2.0 / math.pi) * (x + 0.044715 * (x * x * x))))


def _compress(kv6, pe2, w1lo, w1hi, w2, *, b, t, width):
    nb = t // CMP_STRIDE
    half = CMP_STRIDE * width
    h = kv6.reshape(b, kv6.shape[1], nb, half)

    def body(h_ref, pe_ref, lo_ref, hi_ref, w2_ref, o_ref):
        hv = h_ref[0, 0]
        lo, hi = lo_ref[...], hi_ref[...]
        a = _dot(hv, lo)
        bb = _dot(hv, hi)
        pe = pe_ref[...]
        c = _dot(pe, lo)[0:1] + _dot(pe, hi)[1:2]
        z = a + pltpu.roll(bb, nb - 1, 0) + c
        o_ref[0, 0] = _dot(_gelu_tanh(z).astype(BF16), w2_ref[...]).astype(BF16)

    return _pcall(
        body, name=f"nsa_compress_{width}", grid=(b, NSA_GROUPS),
        in_specs=[
            pl.BlockSpec((1, 1, nb, half), lambda bi, g: (bi, g, 0, 0)),
            pl.BlockSpec((8, half), lambda bi, g: (0, 0)),
            pl.BlockSpec((half, width), lambda bi, g: (0, 0)),
            pl.BlockSpec((half, width), lambda bi, g: (0, 0)),
            pl.BlockSpec((width, width), lambda bi, g: (0, 0)),
        ],
        out_specs=pl.BlockSpec((1, 1, nb, width), lambda bi, g: (bi, g, 0, 0)),
        out_shape=jax.ShapeDtypeStruct((b, NSA_GROUPS, nb, width), BF16),
    )(h, pe2, w1lo, w1hi, w2)


def _gate_col(gates, col):
    lane = lax.broadcasted_iota(I32, gates.shape, 1)
    return jnp.sum(jnp.where(lane == col, gates, 0.0), axis=-1, keepdims=True)


def _nsa_cmp(q, kcmp, vcmp, bias_c, cover, gates, *, b, t):
    tq = NSA_TILE
    nb = t // CMP_STRIDE
    n_slc = t // SEL_BLOCK
    n_sel = min(SEL_COUNT, n_slc)
    hpg = NSA_HPG

    def body(q_ref, k_ref, v_ref, bias_ref, cov_ref, g_ref, o_ref, sel_ref):
        g = pl.program_id(1)
        i = pl.program_id(2)
        q2 = q_ref[0].reshape(hpg * tq, HEAD_PAD)
        s3 = _dot_t(q2, k_ref[0, 0]).reshape(hpg, tq, nb) + bias_ref[...]
        p = jnp.exp(s3 - jnp.max(s3, axis=-1, keepdims=True))
        has_key = jnp.where(i * tq + lax.broadcasted_iota(I32, (tq, 1), 0) >= CMP_LEN - 1, 1.0, 0.0)
        p = p * (has_key / jnp.maximum(jnp.sum(p, axis=-1, keepdims=True), 1e-30))
        o3 = _dot(p.reshape(hpg * tq, nb).astype(BF16), v_ref[0, 0]).reshape(hpg, tq, NSA_DV)
        _write_gated_heads(o_ref, o3, g_ref[0], g, 0)

        psum = jnp.sum(p, axis=0)
        p_hi = psum.astype(BF16)
        p_lo = (psum - p_hi.astype(F32)).astype(BF16)
        cov = cov_ref[...]
        imp = _dot(p_hi, cov) + _dot(p_lo, cov)

        jl = lax.broadcasted_iota(I32, (tq, LANES), 1)
        cur = (i * tq + lax.broadcasted_iota(I32, (tq, LANES), 0)) // SEL_BLOCK
        forced = (jl == 0) | (jl == cur) | (jl == cur - 1)
        score = jnp.where(forced, SEL_FORCE, jnp.where(jl <= cur, imp, -1.0))
        rank = jnp.zeros((tq, LANES), F32)
        for kk in range(n_slc):
            col = score[:, kk:kk + 1]
            ahead = (col > score) | ((col == score) & (kk < jl))
            rank = rank + jnp.where(ahead, 1.0, 0.0)
        chosen = (rank < n_sel) & (score >= 0.0) & (jl < n_slc)
        sel_ref[0, 0] = jnp.where(chosen, 1.0, 0.0).astype(BF16)

    return _pcall(
        body, name="nsa_cmp_select", grid=(b, NSA_GROUPS, t // tq),
        in_specs=[
            pl.BlockSpec((1, hpg, tq, HEAD_PAD), lambda bi, g, i: (bi, g, i, 0)),
            pl.BlockSpec((1, 1, nb, HEAD_PAD), lambda bi, g, i: (bi, g, 0, 0)),
            pl.BlockSpec((1, 1, nb, NSA_DV), lambda bi, g, i: (bi, g, 0, 0)),
            pl.BlockSpec((hpg, tq, nb), lambda bi, g, i: (g, i, 0)),
            pl.BlockSpec((nb, LANES), lambda bi, g, i: (0, 0)),
            pl.BlockSpec((1, tq, LANES), lambda bi, g, i: (bi, i, 0)),
        ],
        out_specs=[
            pl.BlockSpec((1, tq, hpg * NSA_DV), lambda bi, g, i: (bi, i, g)),
            pl.BlockSpec((1, 1, tq, LANES), lambda bi, g, i: (bi, g, i, 0)),
        ],
        out_shape=[jax.ShapeDtypeStruct((b, t, NSA_HEADS * NSA_DV), BF16),
                   jax.ShapeDtypeStruct((b, NSA_GROUPS, t, LANES), BF16)],
    )(q, kcmp, vcmp, bias_c, cover, gates)


def _write_gated_heads(o_ref, o3, gates_v, g, branch):
    for hh in range(NSA_HPG):
        gc = _gate_col(gates_v, (g * NSA_HPG + hh) * NSA_BRANCHES + branch)
        o_ref[0, :, hh * NSA_DV:(hh + 1) * NSA_DV] = (o3[hh] * gc).astype(BF16)


def _nsa_sel(q, k6, v4, stab, cvec, gates, sel, expand_t, *, b, t):
    tb = NSA_TILE
    hpg = NSA_HPG
    wide = 4 * tb
    span = WINDOW + tb
    back = WINDOW // tb
    rows = hpg * tb

    def body(q_ref, k_ref, v_ref, tab_ref, cv_ref, g_ref, sel_ref, et_ref, o_ref):
        g = pl.program_id(1)
        i = pl.program_id(2)
        q2 = q_ref[0].reshape(rows, HEAD_PAD)
        sel_v = sel_ref[0, 0]
        far_end = jnp.maximum(i - back, 0) * tb

        def far(j, carry):
            koff = pl.multiple_of(j * wide, wide)
            s3 = _dot_t(q2, k_ref[0, 0, pl.ds(koff, wide), :]).reshape(hpg, tb, wide) + cv_ref[0][:, None, :]
            key = koff + lax.broadcasted_iota(I32, (tb, wide), 1)
            chosen = jnp.where(key < far_end, _dot_t(sel_v, et_ref[pl.ds(koff, wide), :]), 0.0) > 0.5
            s3 = jnp.where(chosen[None], s3, NEG_INF)
            return _softmax_step(s3.reshape(rows, wide), v_ref[0, 0, pl.ds(koff, wide), :], carry)

        carry = lax.fori_loop(0, (far_end + wide - 1) // wide, far, _softmax_init(rows, NSA_DV))

        koff = pl.multiple_of(far_end, tb)
        s3 = _dot_t(q2, k_ref[0, 0, pl.ds(koff, span), :]).reshape(hpg, tb, span) + tab_ref[0]
        chosen = _dot_t(sel_v, et_ref[pl.ds(koff, span), :]) > 0.5
        s3 = jnp.where(chosen[None], s3, NEG_INF)
        _, l, acc = _softmax_step(s3.reshape(rows, span), v_ref[0, 0, pl.ds(koff, span), :], carry)
        _write_gated_heads(o_ref, (acc / l).reshape(hpg, tb, NSA_DV), g_ref[0], g, 1)

    return _pcall(
        body, name="nsa_sel_attention", grid=(b, NSA_GROUPS, t // tb),
        in_specs=[
            pl.BlockSpec((1, hpg, tb, HEAD_PAD), lambda bi, g, i: (bi, g, i, 0)),
            pl.BlockSpec((1, 1, t, HEAD_PAD), lambda bi, g, i: (bi, NSA_GROUPS + g, 0, 0)),
            pl.BlockSpec((1, 1, t, NSA_DV), lambda bi, g, i: (bi, g, 0, 0)),
            pl.BlockSpec((1, hpg, tb, span), lambda bi, g, i: (jnp.minimum(i, back), g, 0, 0)),
            pl.BlockSpec((1, hpg, wide), lambda bi, g, i: (g, 0, 0)),
            pl.BlockSpec((1, tb, LANES), lambda bi, g, i: (bi, i, 0)),
            pl.BlockSpec((1, 1, tb, LANES), lambda bi, g, i: (bi, g, i, 0)),
            pl.BlockSpec((t, LANES), lambda bi, g, i: (0, 0)),
        ],
        out_specs=pl.BlockSpec((1, tb, hpg * NSA_DV), lambda bi, g, i: (bi, i, g)),
        out_shape=jax.ShapeDtypeStruct((b, t, NSA_HEADS * NSA_DV), BF16),
    )(q, k6, v4, stab, cvec, gates, sel, expand_t)


def _nsa_win(q, k6, v4, wtab, gates, *, b, t):
    tb = NSA_TILE
    hpg = NSA_HPG
    span = WINDOW + tb
    back = WINDOW // tb
    rows = hpg * tb

    def body(q_ref, k_ref, v_ref, w_ref, g_ref, o_ref):
        g = pl.program_id(1)
        i = pl.program_id(2)
        koff = pl.multiple_of(jnp.maximum(i - back, 0) * tb, tb)
        q2 = q_ref[0].reshape(rows, HEAD_PAD)
        s3 = _dot_t(q2, k_ref[0, 0, pl.ds(koff, span), :]).reshape(hpg, tb, span) + w_ref[0]
        s = s3.reshape(rows, span)
        p = jnp.exp(s - jnp.max(s, axis=-1, keepdims=True))
        acc = _dot(p.astype(BF16), v_ref[0, 0, pl.ds(koff, span), :])
        o3 = (acc[:, :NSA_DV] / acc[:, NSA_DV:NSA_DV + 1]).reshape(hpg, tb, NSA_DV)
        _write_gated_heads(o_ref, o3, g_ref[0], g, 2)

    return _pcall(
        body, name="nsa_win_attention", grid=(b, NSA_GROUPS, t // tb),
        in_specs=[
            pl.BlockSpec((1, hpg, tb, HEAD_PAD), lambda bi, g, i: (bi, g, i, 0)),
            pl.BlockSpec((1, 1, t, HEAD_PAD), lambda bi, g, i: (bi, 2 * NSA_GROUPS + g, 0, 0)),
            pl.BlockSpec((1, 1, t, HEAD_PAD), lambda bi, g, i: (bi, NSA_GROUPS + g, 0, 0)),
            pl.BlockSpec((1, hpg, tb, span), lambda bi, g, i: (jnp.minimum(i, back), g, 0, 0)),
            pl.BlockSpec((1, tb, LANES), lambda bi, g, i: (bi, i, 0)),
        ],
        out_specs=pl.BlockSpec((1, tb, hpg * NSA_DV), lambda bi, g, i: (bi, i, g)),
        out_shape=jax.ShapeDtypeStruct((b, t, NSA_HEADS * NSA_DV), BF16),
    )(q, k6, v4, wtab, gates)


def _router(x, w_hi, w_lo, bias, *, n, d, n_exp, tm):
    def body(x_ref, wh_ref, wl_ref, b_ref, e_ref, gate_ref):
        xv = x_ref[...]
        x_hi = xv.astype(BF16)
        x_lo = (xv - x_hi.astype(F32)).astype(BF16)
        logits = _dot(x_hi, wh_ref[...]) + _dot(x_lo, wh_ref[...]) + _dot(x_hi, wl_ref[...]) + b_ref[...]
        lane = lax.broadcasted_iota(I32, (tm, LANES), 1)
        logits = jnp.where(lane < n_exp, logits, NEG_INF)
        e_out = jnp.zeros((tm, LANES), I32)
        v_out = jnp.full((tm, LANES), NEG_INF, F32)
        for kk in range(TOP_K):
            mx = jnp.max(logits, axis=-1, keepdims=True)
            idx = jnp.min(jnp.where(logits == mx, lane, LANES), axis=-1, keepdims=True)
            e_out = jnp.where(lane == kk, idx, e_out)
            v_out = jnp.where(lane == kk, mx, v_out)
            logits = jnp.where(lane == idx, NEG_INF, logits)
        ex = jnp.where(lane < TOP_K, jnp.exp(v_out - jnp.max(v_out, axis=-1, keepdims=True)), 0.0)
        e_ref[...] = e_out
        gate_ref[...] = ex / jnp.sum(ex, axis=-1, keepdims=True)

    return _pcall(
        body, name="moe_router", grid=(n // tm,),
        in_specs=[pl.BlockSpec((tm, d), lambda i: (i, 0)),
                  pl.BlockSpec((d, LANES), lambda i: (0, 0)),
                  pl.BlockSpec((d, LANES), lambda i: (0, 0)),
                  pl.BlockSpec((1, LANES), lambda i: (0, 0))],
        out_specs=[pl.BlockSpec((tm, LANES), lambda i: (i, 0)), pl.BlockSpec((tm, LANES), lambda i: (i, 0))],
        out_shape=[jax.ShapeDtypeStruct((n, LANES), I32), jax.ShapeDtypeStruct((n, LANES), F32)],
    )(x, w_hi, w_lo, bias)


def _row_copy(src_hbm, row, dst, slot, sem, chunks):
    src = src_hbm.at[pl.ds(pl.multiple_of(row * chunks, chunks), chunks)]
    return pltpu.make_async_copy(src, dst.at[pl.ds(pl.multiple_of(slot * chunks, chunks), chunks)], sem)


def _gather_rows(x_chunks, idx3, *, n_rows, d, rows_per_step):
    r_blk = rows_per_step
    chunks = d // LANES

    def body(idx_ref, x_hbm, o_ref, buf, sem):
        def issue(r8, c):
            for u in range(GATHER_UNROLL):
                r = r8 * GATHER_UNROLL + u
                _row_copy(x_hbm, idx_ref[0, 0, r], buf, r, sem, chunks).start(priority=u % 2)
            return c
        lax.fori_loop(0, r_blk // GATHER_UNROLL, issue, 0)

        def wait(r8, c):
            for u in range(GATHER_UNROLL):
                _row_copy(x_hbm, 0, buf, r8 * GATHER_UNROLL + u, sem, chunks).wait()
            return c
        lax.fori_loop(0, r_blk // GATHER_UNROLL, wait, 0)
        for c in range(chunks):
            o_ref[:, c * LANES:(c + 1) * LANES] = buf[pl.ds(c, r_blk, stride=chunks), :].astype(BF16)

    return _pcall(
        body, name="moe_gather", grid=(n_rows // r_blk,),
        in_specs=[pl.BlockSpec((1, 1, r_blk), lambda i: (i, 0, 0), memory_space=pltpu.SMEM),
                  pl.BlockSpec(memory_space=pl.ANY)],
        out_specs=pl.BlockSpec((r_blk, d), lambda i: (i, 0)),
        out_shape=jax.ShapeDtypeStruct((n_rows, d), BF16),
        scratch=[pltpu.VMEM((r_blk * chunks, LANES), F32), pltpu.SemaphoreType.DMA],
    )(idx3, x_chunks)


def _moe_up(blk_e, n_valid, xs, w_glu, w_lin, b_glu, b_lin, *, n_rows, d, f, tm, tf):
    def body(be_ref, nv_ref, x_ref, wg_ref, wl_ref, bg_ref, bl_ref, o_ref):
        i = pl.program_id(1)

        @pl.when(i < nv_ref[0])
        def _():
            xv = x_ref[...]
            h_glu = jnp.minimum(_dot(xv, wg_ref[0]) + bg_ref[0], SWIGLU_LIMIT)
            h_lin = jnp.clip(_dot(xv, wl_ref[0]) + bl_ref[0], -SWIGLU_LIMIT, SWIGLU_LIMIT)
            o_ref[...] = (h_glu * _sigmoid(SWIGLU_ALPHA * h_glu) * (h_lin + 1.0)).astype(BF16)

        @pl.when(i >= nv_ref[0])
        def _():
            o_ref[...] = jnp.zeros(o_ref.shape, BF16)

    return _pcall(
        body, name="moe_up", grid=(f // tf, n_rows // tm), prefetch=2,
        in_specs=[pl.BlockSpec((tm, d), lambda j, i, be, nv: (i, 0)),
                  pl.BlockSpec((1, d, tf), lambda j, i, be, nv: (be[i], 0, j)),
                  pl.BlockSpec((1, d, tf), lambda j, i, be, nv: (be[i], 0, j)),
                  pl.BlockSpec((1, 1, tf), lambda j, i, be, nv: (be[i], 0, j)),
                  pl.BlockSpec((1, 1, tf), lambda j, i, be, nv: (be[i], 0, j))],
        out_specs=pl.BlockSpec((tm, tf), lambda j, i, be, nv: (i, j)),
        out_shape=jax.ShapeDtypeStruct((n_rows, f), BF16),
    )(blk_e, n_valid, xs, w_glu, w_lin, b_glu, b_lin)


def _moe_down(blk_e, fresh, n_valid, a, w_down, b_down, *, n_rows, d, f, tm):
    chunks = d // LANES

    def body(be_ref, fr_ref, nv_ref, a_ref, w_ref, b_ref, o_ref, wb_ref):
        i = pl.program_id(0)

        @pl.when(fr_ref[i] == 1)
        def _():
            wb_ref[...] = w_ref[0].astype(BF16)

        @pl.when(i < nv_ref[0])
        def _():
            res = _dot(a_ref[...], wb_ref[...]) + b_ref[0]
            for c in range(chunks):
                o_ref[pl.ds(c, tm, stride=chunks), :] = res[:, c * LANES:(c + 1) * LANES]

        @pl.when(i >= nv_ref[0])
        def _():
            o_ref[...] = jnp.zeros(o_ref.shape, F32)

    return _pcall(
        body, name="moe_down", grid=(n_rows // tm,), prefetch=3,
        in_specs=[pl.BlockSpec((tm, f), lambda i, be, fr, nv: (i, 0)),
                  pl.BlockSpec((1, f, d), lambda i, be, fr, nv: (be[i], 0, 0)),
                  pl.BlockSpec((1, 1, d), lambda i, be, fr, nv: (be[i], 0, 0))],
        out_specs=pl.BlockSpec((tm * chunks, LANES), lambda i, be, fr, nv: (i, 0)),
        out_shape=jax.ShapeDtypeStruct((n_rows * chunks, LANES), F32),
        scratch=[pltpu.VMEM((f, d), BF16)],
    )(blk_e, fresh, n_valid, a, w_down, b_down)


def _moe_combine(pos3, ys, gates, x, ln_g, ln_b, *, n, d, tq, alpha):
    chunks = d // LANES

    def body(pos_ref, y_hbm, g_ref, x_ref, lg_ref, lb_ref, of_ref, ob_ref, buf, y_ref, sem):
        def issue(r, c):
            for kk in range(TOP_K):
                _row_copy(y_hbm, pos_ref[0, 0, kk * tq + r], buf.at[kk], r, sem, chunks).start(priority=kk % 2)
            return c
        lax.fori_loop(0, tq, issue, 0)

        def wait(r, c):
            for kk in range(TOP_K):
                _row_copy(y_hbm, 0, buf.at[kk], r, sem, chunks).wait()
            return c
        lax.fori_loop(0, tq, wait, 0)
        gv = g_ref[...]
        for c in range(chunks):
            piece = gv[:, 0:1] * buf[0, pl.ds(c, tq, stride=chunks), :]
            for kk in range(1, TOP_K):
                piece = piece + gv[:, kk:kk + 1] * buf[kk, pl.ds(c, tq, stride=chunks), :]
            y_ref[:, c * LANES:(c + 1) * LANES] = piece
        out = _layer_norm(alpha * x_ref[...] + y_ref[...], lg_ref[...], lb_ref[...])
        of_ref[...] = out
        ob_ref[...] = out.astype(BF16)

    return _pcall(
        body, name="moe_combine_ln", grid=(n // tq,),
        in_specs=[pl.BlockSpec((1, 1, TOP_K * tq), lambda i: (i, 0, 0), memory_space=pltpu.SMEM),
                  pl.BlockSpec(memory_space=pl.ANY),
                  pl.BlockSpec((tq, LANES), lambda i: (i, 0)),
                  pl.BlockSpec((tq, d), lambda i: (i, 0)),
                  pl.BlockSpec((1, d), lambda i: (0, 0)),
                  pl.BlockSpec((1, d), lambda i: (0, 0))],
        out_specs=[pl.BlockSpec((tq, d), lambda i: (i, 0)), pl.BlockSpec((tq, d), lambda i: (i, 0))],
        out_shape=[jax.ShapeDtypeStruct((n, d), F32), jax.ShapeDtypeStruct((n, d), BF16)],
        scratch=[pltpu.VMEM((TOP_K, tq * chunks, LANES), F32), pltpu.VMEM((tq, d), F32), pltpu.SemaphoreType.DMA],
    )(pos3, ys, gates, x, ln_g, ln_b)


def _t5_bucket(dist):
    max_exact = REL_BUCKETS // 2
    dd = jnp.maximum(dist, 1).astype(F32)
    large = max_exact + (jnp.log(dd / max_exact) / math.log(REL_MAX_DIST / max_exact)
                         * (REL_BUCKETS - max_exact)).astype(I32)
    large = jnp.minimum(large, REL_BUCKETS - 1)
    return jnp.where(dist < max_exact, dist, large)


def _tables(positions, rel_bias, t):
    half = MLA_ROPE // 2
    inv = 1.0 / (ROPE_THETA ** (jnp.arange(0, MLA_ROPE, 2, dtype=F32) / MLA_ROPE))
    ang = positions.astype(F32)[:, None] * inv[None, :]
    zeros = jnp.zeros((t, LANES - MLA_ROPE), F32)
    cos = jnp.concatenate([jnp.cos(ang), jnp.cos(ang), zeros], axis=-1)
    sin = jnp.concatenate([jnp.sin(ang), jnp.sin(ang), zeros], axis=-1)
    assert half * 2 == MLA_ROPE

    first = jnp.searchsorted(_t5_bucket(jnp.arange(t)), jnp.arange(REL_BUCKETS), side="left").astype(I32)
    rb = rel_bias.astype(F32)

    def bias_of(dist, valid):
        out = jnp.broadcast_to(rb[0].reshape((NSA_HEADS,) + (1,) * dist.ndim), (NSA_HEADS,) + dist.shape)
        for bkt in range(1, REL_BUCKETS):
            out = jnp.where((dist >= first[bkt])[None], rb[bkt].reshape((NSA_HEADS,) + (1,) * dist.ndim), out)
        return jnp.where(valid[None], out, NEG_INF)

    tb = NSA_TILE
    sat = rb[REL_BUCKETS - 1]
    cvec = jnp.broadcast_to(sat.reshape(NSA_GROUPS, NSA_HPG, 1), (NSA_GROUPS, NSA_HPG, 4 * tb))
    span = WINDOW + tb
    wd = (jnp.arange(WINDOW // tb + 1, dtype=I32) * tb)[:, None, None] + jnp.arange(tb, dtype=I32)[None, :, None] \
        - jnp.arange(span, dtype=I32)[None, None, :]
    wtab = bias_of(wd, (wd >= 0) & (wd < WINDOW)).transpose(1, 0, 2, 3)
    stab = bias_of(wd, wd >= 0).transpose(1, 0, 2, 3)
    nb = t // CMP_STRIDE
    dist_c = jnp.arange(t, dtype=I32)[:, None] - (jnp.arange(nb, dtype=I32) * CMP_STRIDE + CMP_LEN - 1)[None, :]
    bias_c = bias_of(dist_c, dist_c >= 0)
    tq_mla = _pick_tile(t, 512)
    cm = jnp.arange(tq_mla, dtype=I32)
    causal = jnp.where(cm[None, :] <= cm[:, None], 0.0, NEG_INF).astype(F32)

    n_slc = t // SEL_BLOCK
    tok = np.arange(nb)[:, None] * CMP_STRIDE + np.arange(CMP_LEN)[None, :]
    cover = (tok[:, :, None] // SEL_BLOCK == np.arange(LANES)[None, None, :]).sum(1) / CMP_LEN
    cover[nb - 1] = 0.0
    cover[:, n_slc:] = 0.0
    expand_t = (np.arange(t)[:, None] // SEL_BLOCK == np.arange(LANES)[None, :])
    return dict(cos=cos, sin=sin, stab=stab, cvec=cvec, wtab=wtab, bias_c=bias_c, causal=causal,
                cover=jnp.asarray(cover, BF16), expand_t=jnp.asarray(expand_t, BF16))


def _pad_heads(w, heads, width):
    k = w.shape[0]
    w = w.reshape(k, heads, width)
    return jnp.pad(w, ((0, 0), (0, 0), (0, HEAD_PAD - width))).reshape(k, heads * HEAD_PAD)


def _pick_tile(total, want):
    tile = min(total, want)
    while total % tile:
        tile //= 2
    return tile


def _token_mixer(x, xb, tabs, p, *, b, t, d):
    n = b * t
    tm = _pick_tile(t, 512)
    tpb = t // tm
    offs = np.cumsum([0, MLA_Q_LORA, MLA_KV_LORA, MLA_ROPE, NSA_HEADS * NSA_DK,
                      NSA_GROUPS * NSA_DK, NSA_GROUPS * NSA_DV, NSA_GROUPS * NSA_DK, NSA_GROUPS * NSA_DV,
                      NSA_GROUPS * NSA_DK, NSA_GROUPS * NSA_DV, NSA_HEADS * NSA_BRANCHES, 2 * d])
    w_in = p["w_in"]
    seg = [w_in[:, offs[s]:offs[s + 1]] for s in range(12)]
    x_lhs = [(xb, _row_spec(tm, d))]
    mla_scale = (MLA_NOPE + MLA_ROPE) ** -0.5
    nsa_scale = NSA_DK ** -0.5

    def cast_epi(acc, er, orr):
        orr[0][...] = acc.astype(BF16)

    lat_w = MLA_Q_LORA + MLA_KV_LORA
    w_lat = jnp.concatenate([seg[0], seg[1]], axis=1).astype(BF16)
    tn_lat = _pick_tile(lat_w, 512)
    (lat,) = _matmul("proj_latents", x_lhs, w_lat, [], [(jax.ShapeDtypeStruct((n, lat_w), BF16), _tile_spec(tm, tn_lat))],
                     _cast_prologue, cast_epi, m=n, k=d, n=lat_w, tm=tm, tn=tn_lat)

    w_small = jnp.concatenate([seg[2], jnp.zeros((d, LANES - MLA_ROPE), F32), seg[10],
                               jnp.zeros((d, LANES - NSA_HEADS * NSA_BRANCHES), F32)], axis=1).astype(BF16)

    def small_epi(acc, er, orr):
        orr[0][...] = _rope_slab(acc[:, :LANES], er[0][...], er[1][...]).astype(BF16)
        orr[1][...] = _sigmoid(acc[:, LANES:])

    cs_extras = [(tabs["cos_n"], _row_spec(tm, LANES)), (tabs["sin_n"], _row_spec(tm, LANES))]
    kr, nsa_gates = _matmul(
        "proj_rope_key_gates", x_lhs, w_small, cs_extras,
        [(jax.ShapeDtypeStruct((n, LANES), BF16), _row_spec(tm, LANES)),
         (jax.ShapeDtypeStruct((n, LANES), F32), _row_spec(tm, LANES))],
        _cast_prologue, small_epi, m=n, k=d, n=2 * LANES, tm=tm, tn=2 * LANES)

    hb = 2
    w_qn = _pad_heads(seg[3], NSA_HEADS, NSA_DK).astype(BF16)
    (q_nsa,) = _matmul("proj_nsa_q", x_lhs, w_qn, [],
                       [(jax.ShapeDtypeStruct((b, NSA_HEADS, t, HEAD_PAD), BF16), _head_spec(tm, hb, HEAD_PAD, tpb))],
                       _cast_prologue, _heads_epilogue(hb, HEAD_PAD, nsa_scale),
                       m=n, k=d, n=NSA_HEADS * HEAD_PAD, tm=tm, tn=hb * HEAD_PAD)

    w_k6 = jnp.concatenate([_pad_heads(seg[s], NSA_GROUPS, NSA_DK) for s in (4, 6, 8)], axis=1).astype(BF16)
    (k6,) = _matmul("proj_nsa_k", x_lhs, w_k6, [],
                    [(jax.ShapeDtypeStruct((b, 3 * NSA_GROUPS, t, HEAD_PAD), BF16), _head_spec(tm, hb, HEAD_PAD, tpb))],
                    _cast_prologue, _heads_epilogue(hb, HEAD_PAD),
                    m=n, k=d, n=3 * NSA_GROUPS * HEAD_PAD, tm=tm, tn=hb * HEAD_PAD)

    (v_c,) = _matmul("proj_nsa_v_cmp", x_lhs, seg[5].astype(BF16), [],
                     [(jax.ShapeDtypeStruct((b, NSA_GROUPS, t, NSA_DV), BF16), _head_spec(tm, hb, NSA_DV, tpb))],
                     _cast_prologue, _heads_epilogue(hb, NSA_DV),
                     m=n, k=d, n=NSA_GROUPS * NSA_DV, tm=tm, tn=hb * NSA_DV)

    def ones_lane(rows):
        return jnp.where(lax.broadcasted_iota(I32, (rows, HEAD_PAD - NSA_DV), 1) == 0, 1.0, 0.0).astype(BF16)

    def v_ones_epi(acc, er, orr):
        for c in range(hb):
            orr[0][0, c, :, :NSA_DV] = acc[:, c * NSA_DV:(c + 1) * NSA_DV].astype(BF16)
            orr[0][0, c, :, NSA_DV:] = ones_lane(acc.shape[0])

    w_v4 = jnp.concatenate([seg[7], seg[9]], axis=1).astype(BF16)
    (v4,) = _matmul("proj_nsa_v", x_lhs, w_v4, [],
                    [(jax.ShapeDtypeStruct((b, 2 * NSA_GROUPS, t, HEAD_PAD), BF16), _head_spec(tm, hb, HEAD_PAD, tpb))],
                    _cast_prologue, v_ones_epi, m=n, k=d, n=2 * NSA_GROUPS * NSA_DV, tm=tm, tn=hb * NSA_DV)

    def sig_epi(acc, er, orr):
        orr[0][...] = _sigmoid(acc).astype(BF16)

    tn_d = _pick_tile(d, 512)
    (merge,) = _matmul("proj_merge_gates", x_lhs, seg[11].astype(BF16), [],
                       [(jax.ShapeDtypeStruct((n, 2 * d), BF16), _tile_spec(tm, tn_d))],
                       _cast_prologue, sig_epi, m=n, k=d, n=2 * d, tm=tm, tn=tn_d)

    w_uq = p["mla_w_uq"].reshape(MLA_Q_LORA, MLA_HEADS, MLA_NOPE + MLA_ROPE)
    w_uq = jnp.pad(w_uq, ((0, 0), (0, 0), (0, HEAD_PAD - MLA_NOPE - MLA_ROPE))).reshape(MLA_Q_LORA, MLA_HEADS * HEAD_PAD)

    def q_epi(acc, er, orr):
        cos, sin = er[0][...], er[1][...]
        for c in range(hb):
            orr[0][0, c, :, :MLA_NOPE] = (acc[:, c * HEAD_PAD:c * HEAD_PAD + MLA_NOPE] * mla_scale).astype(BF16)
            slab = acc[:, c * HEAD_PAD + MLA_NOPE:(c + 1) * HEAD_PAD] * mla_scale
            orr[0][0, c, :, MLA_NOPE:] = _rope_slab(slab, cos, sin).astype(BF16)

    (q_mla,) = _matmul(
        "mla_q_up", [(lat, _row_spec(tm, MLA_Q_LORA)), (p["mla_q_norm"].reshape(1, -1), pl.BlockSpec((1, MLA_Q_LORA), lambda i, j: (0, 0)))],
        w_uq.astype(BF16), cs_extras,
        [(jax.ShapeDtypeStruct((b, MLA_HEADS, t, HEAD_PAD), BF16), _head_spec(tm, hb, HEAD_PAD, tpb))],
        _rms_prologue, q_epi, m=n, k=MLA_Q_LORA, n=MLA_HEADS * HEAD_PAD, tm=tm, tn=hb * HEAD_PAD)

    kv_w = MLA_NOPE + MLA_V

    def kv_epi(acc, er, orr):
        k_rope = er[0][...]
        for c in range(hb):
            orr[0][0, c, :, :MLA_NOPE] = acc[:, c * kv_w:c * kv_w + MLA_NOPE].astype(BF16)
            orr[0][0, c, :, MLA_NOPE:] = k_rope
            orr[1][0, c, :, :MLA_V] = acc[:, c * kv_w + MLA_NOPE:(c + 1) * kv_w].astype(BF16)
            orr[1][0, c, :, MLA_V:] = ones_lane(acc.shape[0])

    k_mla, v_mla = _matmul(
        "mla_kv_up", [(lat, _row_spec(tm, MLA_KV_LORA, MLA_Q_LORA // MLA_KV_LORA)),
                      (p["mla_kv_norm"].reshape(1, -1), pl.BlockSpec((1, MLA_KV_LORA), lambda i, j: (0, 0)))],
        p["mla_w_ukv"].astype(BF16), [(kr, _row_spec(tm, LANES))],
        [(jax.ShapeDtypeStruct((b, MLA_HEADS, t, HEAD_PAD), BF16), _head_spec(tm, hb, HEAD_PAD, tpb)),
         (jax.ShapeDtypeStruct((b, MLA_HEADS, t, HEAD_PAD), BF16), _head_spec(tm, hb, HEAD_PAD, tpb))],
        _rms_prologue, kv_epi, m=n, k=MLA_KV_LORA, n=MLA_HEADS * kv_w, tm=tm, tn=hb * kv_w)

    attn_a = _mla_attention(q_mla, k_mla, v_mla, tabs["causal"], b=b, t=t, tq=tabs["causal"].shape[0])

    def ya_epi(acc, er, orr):
        orr[0][...] = (er[0][...].astype(F32) * acc).astype(BF16)

    (y_a,) = _matmul("mla_branch_out", [(attn_a.reshape(n, MLA_HEADS * MLA_V), _row_spec(tm, MLA_HEADS * MLA_V))],
                     p["w_branch_a"].astype(BF16), [(merge, _tile_spec(tm, tn_d))],
                     [(jax.ShapeDtypeStruct((n, d), BF16), _tile_spec(tm, tn_d))],
                     _cast_prologue, ya_epi, m=n, k=MLA_HEADS * MLA_V, n=d, tm=tm, tn=tn_d)

    def cmp_weights(pe, w1, w2, width_in, width):
        pe_p = jnp.pad(pe, ((0, 0), (0, width - width_in)))
        pe2 = jnp.zeros((8, CMP_STRIDE * width), F32).at[0].set(pe_p[:CMP_STRIDE].reshape(-1)).at[1].set(pe_p[CMP_STRIDE:].reshape(-1))
        w1p = jnp.pad(w1.reshape(CMP_LEN, width_in, width_in), ((0, 0), (0, width - width_in), (0, width - width_in)))
        w2p = jnp.pad(w2, ((0, width - width_in), (0, width - width_in)))
        return (pe2.astype(BF16), w1p[:CMP_STRIDE].reshape(-1, width).astype(BF16),
                w1p[CMP_STRIDE:].reshape(-1, width).astype(BF16), w2p.astype(BF16))

    kcmp = _compress(k6, *cmp_weights(p["cmp_pe_k"], p["cmp_w1_k"], p["cmp_w2_k"], NSA_DK, HEAD_PAD), b=b, t=t, width=HEAD_PAD)
    vcmp = _compress(v_c, *cmp_weights(p["cmp_pe_v"], p["cmp_w1_v"], p["cmp_w2_v"], NSA_DV, NSA_DV), b=b, t=t, width=NSA_DV)
    gates3 = nsa_gates.reshape(b, t, LANES)
    o_cmp, sel = _nsa_cmp(q_nsa, kcmp, vcmp, tabs["bias_c"], tabs["cover"], gates3, b=b, t=t)
    o_slc = _nsa_sel(q_nsa, k6, v4, tabs["stab"], tabs["cvec"], gates3, sel, tabs["expand_t"], b=b, t=t)
    o_win = _nsa_win(q_nsa, k6, v4, tabs["wtab"], gates3, b=b, t=t)

    def sum3_prologue(a_ref, b_ref, c_ref):
        return (a_ref[...].astype(F32) + b_ref[...].astype(F32) + c_ref[...].astype(F32)).astype(BF16)

    def merge_epi(acc, er, orr):
        orr[0][...] = (er[0][...].astype(F32) + er[1][...].astype(F32) * acc).astype(BF16)

    hv = NSA_HEADS * NSA_DV
    (merged,) = _matmul("nsa_branch_out_merge", [(o.reshape(n, hv), _row_spec(tm, hv)) for o in (o_cmp, o_slc, o_win)],
                        p["w_branch_b"].astype(BF16), [(y_a, _tile_spec(tm, tn_d)), (merge, _tile_spec(tm, tn_d, d // tn_d))],
                        [(jax.ShapeDtypeStruct((n, d), BF16), _tile_spec(tm, tn_d))],
                        sum3_prologue, merge_epi, m=n, k=hv, n=d, tm=tm, tn=tn_d)

    tm_ln = _pick_tile(t, 256)
    alpha = p["alpha"]

    def ln_epi(acc, er, orr):
        out = _layer_norm(alpha * er[0][...] + acc, er[1][...], er[2][...])
        orr[0][...] = out
        orr[1][...] = out.astype(BF16)

    vec_spec = pl.BlockSpec((1, d), lambda i, j: (0, 0))
    x1, x1b = _matmul("mixer_out_ln", [(merged, _row_spec(tm_ln, d))], p["w_out"].astype(BF16),
                      [(x, _row_spec(tm_ln, d)), (p["ln1_g"].reshape(1, d), vec_spec), (p["ln1_b"].reshape(1, d), vec_spec)],
                      [(jax.ShapeDtypeStruct((n, d), F32), _row_spec(tm_ln, d)), (jax.ShapeDtypeStruct((n, d), BF16), _row_spec(tm_ln, d))],
                      _cast_prologue, ln_epi, m=n, k=d, n=d, tm=tm_ln, tn=d)
    return x1, x1b


def _moe(x, p, *, n, d):
    n_exp, _, f2 = p["exp_w_up"].shape
    f = f2 // 2
    tm_e = p["tm_e"]
    rw = jnp.pad(p["router_w"], ((0, 0), (0, LANES - n_exp)))
    rw_hi = rw.astype(BF16)
    rw_lo = (rw - rw_hi.astype(F32)).astype(BF16)
    rb = jnp.pad(p["router_b"], (0, LANES - n_exp)).reshape(1, LANES)
    top_e, gates = _router(x, rw_hi, rw_lo, rb, n=n, d=d, n_exp=n_exp, tm=_pick_tile(n, 512))

    n_pairs = n * TOP_K
    e_flat = top_e[:, :TOP_K].reshape(-1)
    onehot = (e_flat[:, None] == jnp.arange(n_exp, dtype=I32)[None, :]).astype(I32)
    csum = jnp.cumsum(onehot, axis=0)
    rank = jnp.take_along_axis(csum, e_flat[:, None], axis=1)[:, 0] - 1
    counts = csum[-1]
    padded = (counts + tm_e - 1) // tm_e * tm_e
    pad_end = jnp.cumsum(padded)
    dest = (pad_end - padded)[e_flat] + rank
    n_rows = n_pairs + n_exp * tm_e
    n_blk = n_rows // tm_e
    row_tok = jnp.zeros((n_rows,), I32).at[dest].set(jnp.arange(n_pairs, dtype=I32) // TOP_K)
    n_valid = (pad_end[-1] // tm_e).astype(I32)
    blk_e = jnp.searchsorted(pad_end, jnp.arange(n_blk, dtype=I32) * tm_e, side="right").astype(I32)
    blk_e = jnp.minimum(blk_e, blk_e[jnp.maximum(n_valid - 1, 0)])
    n_valid = n_valid.reshape(1)

    r_blk = _pick_tile(n_rows, 256)
    xs = _gather_rows(x.reshape(n * (d // LANES), LANES), row_tok.reshape(n_rows // r_blk, 1, r_blk),
                      n_rows=n_rows, d=d, rows_per_step=r_blk)

    w_up = p["exp_w_up"].reshape(n_exp, d, f, 2)
    b_up = p["exp_b_up"].reshape(n_exp, 1, f, 2)
    tf = _pick_tile(f, 512)
    a = _moe_up(blk_e, n_valid, xs, w_up[..., 0].astype(BF16), w_up[..., 1].astype(BF16), b_up[..., 0], b_up[..., 1],
                n_rows=n_rows, d=d, f=f, tm=tm_e, tf=tf)
    split = max(1, tm_e // 256)
    tm_d = tm_e // split
    blk_d = jnp.repeat(blk_e, split)
    fresh = jnp.concatenate([jnp.ones((1,), I32), (blk_d[1:] != blk_d[:-1]).astype(I32)])
    ys = _moe_down(blk_d, fresh, n_valid * split, a, p["exp_w_down"], p["exp_b_down"].reshape(n_exp, 1, d),
                   n_rows=n_rows, d=d, f=f, tm=tm_d)

    tq = _pick_tile(n, 128)
    pos3 = dest.reshape(n // tq, tq, TOP_K).transpose(0, 2, 1).reshape(n // tq, 1, TOP_K * tq)
    return _moe_combine(pos3, ys, gates, x, p["ln2_g"].reshape(1, d), p["ln2_b"].reshape(1, d),
                        n=n, d=d, tq=tq, alpha=p["alpha"])


def kernel(x, positions, rel_bias, w_in, mla_q_norm, mla_w_uq, mla_kv_norm, mla_w_ukv, cmp_pe_k, cmp_w1_k, cmp_w2_k, cmp_pe_v, cmp_w1_v, cmp_w2_v, w_branch_a, w_branch_b, w_out, ln1_g, ln1_b, router_w, router_b, exp_w_up, exp_b_up, exp_w_down, exp_b_down, ln2_g, ln2_b):
    b, t, d = x.shape
    depth = w_in.shape[0]
    n = b * t
    alpha = (2 * depth) ** 0.25
    tabs = _tables(positions, rel_bias, t)
    tabs["cos_n"] = jnp.tile(tabs["cos"], (b, 1))
    tabs["sin_n"] = jnp.tile(tabs["sin"], (b, 1))
    xf = x.reshape(n, d)
    xb = xf.astype(BF16)
    for l in range(depth):
        p = dict(w_in=w_in[l], mla_q_norm=mla_q_norm[l], mla_w_uq=mla_w_uq[l], mla_kv_norm=mla_kv_norm[l],
                 mla_w_ukv=mla_w_ukv[l], cmp_pe_k=cmp_pe_k[l], cmp_w1_k=cmp_w1_k[l], cmp_w2_k=cmp_w2_k[l],
                 cmp_pe_v=cmp_pe_v[l], cmp_w1_v=cmp_w1_v[l], cmp_w2_v=cmp_w2_v[l], w_branch_a=w_branch_a[l],
                 w_branch_b=w_branch_b[l], w_out=w_out[l], ln1_g=ln1_g[l], ln1_b=ln1_b[l], router_w=router_w[l],
                 router_b=router_b[l], exp_w_up=exp_w_up[l], exp_b_up=exp_b_up[l], exp_w_down=exp_w_down[l],
                 exp_b_down=exp_b_down[l], ln2_g=ln2_g[l], ln2_b=ln2_b[l], alpha=alpha, tm_e=_pick_tile(n * TOP_K, 512))
        x1, _ = _token_mixer(xf, xb, tabs, p, b=b, t=t, d=d)
        xf, xb = _moe(x1, p, n=n, d=d)
    return xf.reshape(b, t, d)
```

```python
import functools
import math

import numpy as np
import jax
import jax.numpy as jnp
from jax import lax
from jax.experimental import pallas as pl
from jax.experimental.pallas import tpu as pltpu

F32, BF16, I32 = jnp.float32, jnp.bfloat16, jnp.int32

MLA_HEADS = 16
MLA_Q_LORA = 1536
MLA_KV_LORA = 512
MLA_NOPE = 128
MLA_ROPE = 64
MLA_V = 128
ROPE_THETA = 10000.0
NSA_HEADS = 16
NSA_GROUPS = 2
NSA_HPG = NSA_HEADS // NSA_GROUPS
NSA_DK = 192
NSA_DV = 128
CMP_LEN = 32
CMP_STRIDE = 16
SEL_BLOCK = 64
SEL_COUNT = 16
WINDOW = 512
NSA_BRANCHES = 3
SEL_FORCE = 1.0e4
REL_BUCKETS = 32
REL_MAX_DIST = 128
TOP_K = 4
SWIGLU_LIMIT = 7.0
SWIGLU_ALPHA = 1.702
NEG_INF = -1.0e30
LN_EPS = 1e-5
RMS_EPS = 1e-6

LANES = 128
HEAD_PAD = 256
NSA_TILE = 128
GATHER_UNROLL = 8
VMEM_LIMIT_BYTES = 56 * 1024 * 1024


def _pcall(body, *, name, grid, in_specs, out_specs, out_shape, scratch=(), prefetch=0):
    gs = pltpu.PrefetchScalarGridSpec(num_scalar_prefetch=prefetch, grid=grid, in_specs=in_specs,
                                      out_specs=out_specs, scratch_shapes=list(scratch))
    return pl.pallas_call(
        body, grid_spec=gs, out_shape=out_shape, name=name,
        compiler_params=pltpu.CompilerParams(dimension_semantics=("arbitrary",) * len(grid),
                                             vmem_limit_bytes=VMEM_LIMIT_BYTES))


def _sigmoid(x):
    return 1.0 / (1.0 + jnp.exp(-x))


def _dot_t(a, b):
    return lax.dot_general(a, b, (((1,), (1,)), ((), ())), preferred_element_type=F32)


def _dot(a, b):
    return jnp.dot(a, b, preferred_element_type=F32)


def _layer_norm(r, g, b):
    mu = jnp.mean(r, axis=-1, keepdims=True)
    c = r - mu
    var = jnp.mean(c * c, axis=-1, keepdims=True)
    return c * lax.rsqrt(var + LN_EPS) * g + b


def _matmul(name, lhs, w, extras, outs, prologue, epilogue, *, m, k, n, tm, tn):
    nl, ne, no = len(lhs), len(extras), len(outs)

    def body(*refs):
        lr = refs[:nl]
        wr = refs[nl]
        er = refs[nl + 1:nl + 1 + ne]
        orr = refs[nl + 1 + ne:nl + 1 + ne + no]
        xs = refs[-1]

        @pl.when(pl.program_id(1) == 0)
        def _():
            xs[...] = prologue(*lr)

        epilogue(_dot(xs[...], wr[...]), er, orr)

    return _pcall(
        body, name=name, grid=(m // tm, n // tn),
        in_specs=[s for _, s in lhs] + [pl.BlockSpec((k, tn), lambda i, j: (0, j))] + [s for _, s in extras],
        out_specs=[s for _, s in outs], out_shape=[o for o, _ in outs],
        scratch=[pltpu.VMEM((tm, k), BF16)],
    )(*[a for a, _ in lhs], w, *[a for a, _ in extras])


def _row_spec(tm, width, col=0):
    return pl.BlockSpec((tm, width), lambda i, j: (i, col))


def _tile_spec(tm, tn, col_off=0):
    return pl.BlockSpec((tm, tn), lambda i, j: (i, j + col_off))


def _head_spec(tm, hb, width, tpb):
    return pl.BlockSpec((1, hb, tm, width), lambda i, j: (i // tpb, j, i % tpb, 0))


def _cast_prologue(x_ref):
    return x_ref[...].astype(BF16)


def _rms_prologue(x_ref, g_ref):
    xf = x_ref[...].astype(F32)
    ms = jnp.mean(xf * xf, axis=-1, keepdims=True)
    return (xf * lax.rsqrt(ms + RMS_EPS) * g_ref[...]).astype(BF16)


def _rope_slab(t, cos, sin):
    lane = lax.broadcasted_iota(I32, t.shape, 1)
    rot = jnp.where(lane < MLA_ROPE // 2, -pltpu.roll(t, LANES - MLA_ROPE // 2, 1), pltpu.roll(t, MLA_ROPE // 2, 1))
    return t * cos + rot * sin


def _heads_epilogue(hb, width, scale=None):
    def epi(acc, er, orr):
        for c in range(hb):
            v = acc[:, c * width:(c + 1) * width]
            if scale is not None:
                v = v * scale
            orr[0][0, c] = v.astype(BF16)
    return epi


def _softmax_init(rows, width):
    return (jnp.full((rows, 1), NEG_INF, F32), jnp.zeros((rows, 1), F32), jnp.zeros((rows, width), F32))


def _softmax_step(s, v, carry):
    m, l, acc = carry
    m_new = jnp.maximum(m, jnp.max(s, axis=-1, keepdims=True))
    a = jnp.exp(m - m_new)
    p = jnp.exp(s - m_new)
    return m_new, a * l + jnp.sum(p, axis=-1, keepdims=True), a * acc + _dot(p.astype(BF16), v)


def _mla_attention(q, k, v_ones, causal, *, b, t, tq):
    nq = t // tq

    def body(q_ref, k_ref, v_ref, c_ref, o_ref):
        i = pl.program_id(2)
        qv = q_ref[0, 0]

        def step(koff, carry, diagonal):
            s = _dot_t(qv, k_ref[0, 0, pl.ds(koff, tq), :])
            if diagonal:
                s = s + c_ref[...]
            return _softmax_step(s, v_ref[0, 0, pl.ds(koff, tq), :], carry)

        carry = lax.fori_loop(0, i, lambda j, c: step(pl.multiple_of(j * tq, tq), c, False), _softmax_init(tq, MLA_V))
        _, l, acc = step(pl.multiple_of(i * tq, tq), carry, True)
        o_ref[0] = (acc / l).astype(BF16)

    return _pcall(
        body, name="mla_attention", grid=(b, MLA_HEADS, nq),
        in_specs=[pl.BlockSpec((1, 1, tq, HEAD_PAD), lambda bi, h, i: (bi, h, i, 0)),
                  pl.BlockSpec((1, 1, t, HEAD_PAD), lambda bi, h, i: (bi, h, 0, 0)),
                  pl.BlockSpec((1, 1, t, MLA_V), lambda bi, h, i: (bi, h, 0, 0)),
                  pl.BlockSpec((tq, tq), lambda bi, h, i: (0, 0))],
        out_specs=pl.BlockSpec((1, tq, MLA_V), lambda bi, h, i: (bi, i, h)),
        out_shape=jax.ShapeDtypeStruct((b, t, MLA_HEADS * MLA_V), BF16),
    )(q, k, v_ones, causal)


def _gelu_tanh(x):
    return 0.5 * x * (1.0 + jnp.tanh(math.sqrt(2.0 / math.pi) * (x + 0.044715 * (x * x * x))))


def _compress(kv6, pe2, w1lo, w1hi, w2, *, b, t, width):
    nb = t // CMP_STRIDE
    half = CMP_STRIDE * width
    h = kv6.reshape(b, kv6.shape[1], nb, half)

    def body(h_ref, pe_ref, lo_ref, hi_ref, w2_ref, o_ref):
        hv = h_ref[0, 0]
        lo, hi = lo_ref[...], hi_ref[...]
        a = _dot(hv, lo)
        bb = _dot(hv, hi)
        pe = pe_ref[...]
        c = _dot(pe, lo)[0:1] + _dot(pe, hi)[1:2]
        z = a + pltpu.roll(bb, nb - 1, 0) + c
        o_ref[0, 0] = _dot(_gelu_tanh(z).astype(BF16), w2_ref[...]).astype(BF16)

    return _pcall(
        body, name=f"nsa_compress_{width}", grid=(b, NSA_GROUPS),
        in_specs=[
            pl.BlockSpec((1, 1, nb, half), lambda bi, g: (bi, g, 0, 0)),
            pl.BlockSpec((8, half), lambda bi, g: (0, 0)),
            pl.BlockSpec((half, width), lambda bi, g: (0, 0)),
            pl.BlockSpec((half, width), lambda bi, g: (0, 0)),
            pl.BlockSpec((width, width), lambda bi, g: (0, 0)),
        ],
        out_specs=pl.BlockSpec((1, 1, nb, width), lambda bi, g: (bi, g, 0, 0)),
        out_shape=jax.ShapeDtypeStruct((b, NSA_GROUPS, nb, width), BF16),
    )(h, pe2, w1lo, w1hi, w2)


def _gate_col(gates, col):
    lane = lax.broadcasted_iota(I32, gates.shape, 1)
    return jnp.sum(jnp.where(lane == col, gates, 0.0), axis=-1, keepdims=True)


def _nsa_cmp(q, kcmp, vcmp, bias_c, cover, gates, *, b, t):
    tq = NSA_TILE
    nb = t // CMP_STRIDE
    n_slc = t // SEL_BLOCK
    n_sel = min(SEL_COUNT, n_slc)
    hpg = NSA_HPG

    def body(q_ref, k_ref, v_ref, bias_ref, cov_ref, g_ref, o_ref, sel_ref):
        g = pl.program_id(1)
        i = pl.program_id(2)
        q2 = q_ref[0].reshape(hpg * tq, HEAD_PAD)
        s3 = _dot_t(q2, k_ref[0, 0]).reshape(hpg, tq, nb) + bias_ref[...]
        p = jnp.exp(s3 - jnp.max(s3, axis=-1, keepdims=True))
        has_key = jnp.where(i * tq + lax.broadcasted_iota(I32, (tq, 1), 0) >= CMP_LEN - 1, 1.0, 0.0)
        p = p * (has_key / jnp.maximum(jnp.sum(p, axis=-1, keepdims=True), 1e-30))
        o3 = _dot(p.reshape(hpg * tq, nb).astype(BF16), v_ref[0, 0]).reshape(hpg, tq, NSA_DV)
        _write_gated_heads(o_ref, o3, g_ref[0], g, 0)

        psum = jnp.sum(p, axis=0)
        p_hi = psum.astype(BF16)
        p_lo = (psum - p_hi.astype(F32)).astype(BF16)
        cov = cov_ref[...]
        imp = _dot(p_hi, cov) + _dot(p_lo, cov)

        jl = lax.broadcasted_iota(I32, (tq, LANES), 1)
        cur = (i * tq + lax.broadcasted_iota(I32, (tq, LANES), 0)) // SEL_BLOCK
        forced = (jl == 0) | (jl == cur) | (jl == cur - 1)
        score = jnp.where(forced, SEL_FORCE, jnp.where(jl <= cur, imp, -1.0))
        n_used = -(-n_slc // 8) * 8
        st = jnp.transpose(score)[:n_used]
        jrow = lax.broadcasted_iota(I32, (n_used, tq), 0)
        rank = jnp.zeros((n_used, tq), F32)
        for kk in range(n_slc):
            row = st[kk:kk + 1, :]
            rank = rank + jnp.where(jrow > kk, jnp.where(row >= st, 1.0, 0.0), jnp.where(row > st, 1.0, 0.0))
        chosen_t = jnp.where((rank < n_sel) & (st >= 0.0), 1.0, 0.0)
        if n_used < LANES:
            chosen_t = jnp.concatenate([chosen_t, jnp.zeros((LANES - n_used, tq), F32)], axis=0)
        sel_ref[0, 0] = jnp.transpose(chosen_t).astype(BF16)

    return _pcall(
        body, name="nsa_cmp_select", grid=(b, NSA_GROUPS, t // tq),
        in_specs=[
            pl.BlockSpec((1, hpg, tq, HEAD_PAD), lambda bi, g, i: (bi, g, i, 0)),
            pl.BlockSpec((1, 1, nb, HEAD_PAD), lambda bi, g, i: (bi, g, 0, 0)),
            pl.BlockSpec((1, 1, nb, NSA_DV), lambda bi, g, i: (bi, g, 0, 0)),
            pl.BlockSpec((hpg, tq, nb), lambda bi, g, i: (g, i, 0)),
            pl.BlockSpec((nb, LANES), lambda bi, g, i: (0, 0)),
            pl.BlockSpec((1, tq, LANES), lambda bi, g, i: (bi, i, 0)),
        ],
        out_specs=[
            pl.BlockSpec((1, tq, hpg * NSA_DV), lambda bi, g, i: (bi, i, g)),
            pl.BlockSpec((1, 1, tq, LANES), lambda bi, g, i: (bi, g, i, 0)),
        ],
        out_shape=[jax.ShapeDtypeStruct((b, t, NSA_HEADS * NSA_DV), BF16),
                   jax.ShapeDtypeStruct((b, NSA_GROUPS, t, LANES), BF16)],
    )(q, kcmp, vcmp, bias_c, cover, gates)


def _write_gated_heads(o_ref, o3, gates_v, g, branch):
    for hh in range(NSA_HPG):
        gc = _gate_col(gates_v, (g * NSA_HPG + hh) * NSA_BRANCHES + branch)
        o_ref[0, :, hh * NSA_DV:(hh + 1) * NSA_DV] = (o3[hh] * gc).astype(BF16)


def _nsa_sel(q, k6, v4, stab, cvec, gates, sel, expand_t, *, b, t):
    tb = NSA_TILE
    hpg = NSA_HPG
    wide = 4 * tb
    span = WINDOW + tb
    back = WINDOW // tb
    rows = hpg * tb

    def body(q_ref, k_ref, v_ref, tab_ref, cv_ref, g_ref, sel_ref, et_ref, o_ref):
        g = pl.program_id(1)
        i = pl.program_id(2)
        q2 = q_ref[0].reshape(rows, HEAD_PAD)
        sel_v = sel_ref[0, 0]
        far_end = jnp.maximum(i - back, 0) * tb

        def far(j, carry):
            koff = pl.multiple_of(j * wide, wide)
            s3 = _dot_t(q2, k_ref[0, 0, pl.ds(koff, wide), :]).reshape(hpg, tb, wide) + cv_ref[0][:, None, :]
            key = koff + lax.broadcasted_iota(I32, (tb, wide), 1)
            chosen = jnp.where(key < far_end, _dot_t(sel_v, et_ref[pl.ds(koff, wide), :]), 0.0) > 0.5
            s3 = jnp.where(chosen[None], s3, NEG_INF)
            return _softmax_step(s3.reshape(rows, wide), v_ref[0, 0, pl.ds(koff, wide), :], carry)

        carry = lax.fori_loop(0, (far_end + wide - 1) // wide, far, _softmax_init(rows, NSA_DV))

        koff = pl.multiple_of(far_end, tb)
        s3 = _dot_t(q2, k_ref[0, 0, pl.ds(koff, span), :]).reshape(hpg, tb, span) + tab_ref[0]
        chosen = _dot_t(sel_v, et_ref[pl.ds(koff, span), :]) > 0.5
        s3 = jnp.where(chosen[None], s3, NEG_INF)
        _, l, acc = _softmax_step(s3.reshape(rows, span), v_ref[0, 0, pl.ds(koff, span), :], carry)
        _write_gated_heads(o_ref, (acc / l).reshape(hpg, tb, NSA_DV), g_ref[0], g, 1)

    return _pcall(
        body, name="nsa_sel_attention", grid=(b, NSA_GROUPS, t // tb),
        in_specs=[
            pl.BlockSpec((1, hpg, tb, HEAD_PAD), lambda bi, g, i: (bi, g, i, 0)),
            pl.BlockSpec((1, 1, t, HEAD_PAD), lambda bi, g, i: (bi, NSA_GROUPS + g, 0, 0)),
            pl.BlockSpec((1, 1, t, NSA_DV), lambda bi, g, i: (bi, g, 0, 0)),
            pl.BlockSpec((1, hpg, tb, span), lambda bi, g, i: (jnp.minimum(i, back), g, 0, 0)),
            pl.BlockSpec((1, hpg, wide), lambda bi, g, i: (g, 0, 0)),
            pl.BlockSpec((1, tb, LANES), lambda bi, g, i: (bi, i, 0)),
            pl.BlockSpec((1, 1, tb, LANES), lambda bi, g, i: (bi, g, i, 0)),
            pl.BlockSpec((t, LANES), lambda bi, g, i: (0, 0)),
        ],
        out_specs=pl.BlockSpec((1, tb, hpg * NSA_DV), lambda bi, g, i: (bi, i, g)),
        out_shape=jax.ShapeDtypeStruct((b, t, NSA_HEADS * NSA_DV), BF16),
    )(q, k6, v4, stab, cvec, gates, sel, expand_t)


def _nsa_win(q, k6, v4, wtab, gates, *, b, t):
    tb = NSA_TILE
    hpg = NSA_HPG
    span = WINDOW + tb
    back = WINDOW // tb
    rows = hpg * tb

    def body(q_ref, k_ref, v_ref, w_ref, g_ref, o_ref):
        g = pl.program_id(1)
        i = pl.program_id(2)
        koff = pl.multiple_of(jnp.maximum(i - back, 0) * tb, tb)
        q2 = q_ref[0].reshape(rows, HEAD_PAD)
        s3 = _dot_t(q2, k_ref[0, 0, pl.ds(koff, span), :]).reshape(hpg, tb, span) + w_ref[0]
        s = s3.reshape(rows, span)
        p = jnp.exp(s - jnp.max(s, axis=-1, keepdims=True))
        acc = _dot(p.astype(BF16), v_ref[0, 0, pl.ds(koff, span), :])
        o3 = (acc[:, :NSA_DV] / acc[:, NSA_DV:NSA_DV + 1]).reshape(hpg, tb, NSA_DV)
        _write_gated_heads(o_ref, o3, g_ref[0], g, 2)

    return _pcall(
        body, name="nsa_win_attention", grid=(b, NSA_GROUPS, t // tb),
        in_specs=[
            pl.BlockSpec((1, hpg, tb, HEAD_PAD), lambda bi, g, i: (bi, g, i, 0)),
            pl.BlockSpec((1, 1, t, HEAD_PAD), lambda bi, g, i: (bi, 2 * NSA_GROUPS + g, 0, 0)),
            pl.BlockSpec((1, 1, t, HEAD_PAD), lambda bi, g, i: (bi, NSA_GROUPS + g, 0, 0)),
            pl.BlockSpec((1, hpg, tb, span), lambda bi, g, i: (jnp.minimum(i, back), g, 0, 0)),
            pl.BlockSpec((1, tb, LANES), lambda bi, g, i: (bi, i, 0)),
        ],
        out_specs=pl.BlockSpec((1, tb, hpg * NSA_DV), lambda bi, g, i: (bi, i, g)),
        out_shape=jax.ShapeDtypeStruct((b, t, NSA_HEADS * NSA_DV), BF16),
    )(q, k6, v4, wtab, gates)


def _router(x, w_hi, w_lo, bias, *, n, d, n_exp, tm):
    def body(x_ref, wh_ref, wl_ref, b_ref, e_ref, gate_ref):
        xv = x_ref[...]
        x_hi = xv.astype(BF16)
        x_lo = (xv - x_hi.astype(F32)).astype(BF16)
        logits = _dot(x_hi, wh_ref[...]) + _dot(x_lo, wh_ref[...]) + _dot(x_hi, wl_ref[...]) + b_ref[...]
        lane = lax.broadcasted_iota(I32, (tm, LANES), 1)
        logits = jnp.where(lane < n_exp, logits, NEG_INF)
        e_out = jnp.zeros((tm, LANES), I32)
        v_out = jnp.full((tm, LANES), NEG_INF, F32)
        for kk in range(TOP_K):
            mx = jnp.max(logits, axis=-1, keepdims=True)
            idx = jnp.min(jnp.where(logits == mx, lane, LANES), axis=-1, keepdims=True)
            e_out = jnp.where(lane == kk, idx, e_out)
            v_out = jnp.where(lane == kk, mx, v_out)
            logits = jnp.where(lane == idx, NEG_INF, logits)
        ex = jnp.where(lane < TOP_K, jnp.exp(v_out - jnp.max(v_out, axis=-1, keepdims=True)), 0.0)
        e_ref[...] = e_out
        gate_ref[...] = ex / jnp.sum(ex, axis=-1, keepdims=True)

    return _pcall(
        body, name="moe_router", grid=(n // tm,),
        in_specs=[pl.BlockSpec((tm, d), lambda i: (i, 0)),
                  pl.BlockSpec((d, LANES), lambda i: (0, 0)),
                  pl.BlockSpec((d, LANES), lambda i: (0, 0)),
                  pl.BlockSpec((1, LANES), lambda i: (0, 0))],
        out_specs=[pl.BlockSpec((tm, LANES), lambda i: (i, 0)), pl.BlockSpec((tm, LANES), lambda i: (i, 0))],
        out_shape=[jax.ShapeDtypeStruct((n, LANES), I32), jax.ShapeDtypeStruct((n, LANES), F32)],
    )(x, w_hi, w_lo, bias)


def _row_copy(src_hbm, row, dst, slot, sem, chunks):
    src = src_hbm.at[pl.ds(pl.multiple_of(row * chunks, chunks), chunks)]
    return pltpu.make_async_copy(src, dst.at[pl.ds(pl.multiple_of(slot * chunks, chunks), chunks)], sem)


def _moe_up(blk_e, n_valid, x_chunks, idx3, w_glu, w_lin, b_glu, b_lin, *, n_rows, d, f, tm, tf):
    chunks = d // LANES
    n_blk = n_rows // tm

    def body(be_ref, nv_ref, idx_ref, nxt_ref, x_hbm, wg_ref, wl_ref, bg_ref, bl_ref, o_ref, buf, xs, sem):
        i = pl.program_id(0)
        j = pl.program_id(1)
        nv = nv_ref[0]
        slot = i % 2

        def start_rows(src_idx, s):
            def issue(r8, c):
                for u in range(GATHER_UNROLL):
                    r = r8 * GATHER_UNROLL + u
                    _row_copy(x_hbm, src_idx[0, 0, r], buf.at[s], r, sem.at[s], chunks).start(priority=u % 2)
                return c
            lax.fori_loop(0, tm // GATHER_UNROLL, issue, 0)

        @pl.when((j == 0) & (i < nv))
        def _():
            @pl.when(i == 0)
            def _():
                start_rows(idx_ref, 0)

            def wait(r8, c):
                for u in range(GATHER_UNROLL):
                    _row_copy(x_hbm, 0, buf.at[slot], r8 * GATHER_UNROLL + u, sem.at[slot], chunks).wait()
                return c
            lax.fori_loop(0, tm // GATHER_UNROLL, wait, 0)

            @pl.when(i + 1 < nv)
            def _():
                start_rows(nxt_ref, 1 - slot)

            for c in range(chunks):
                xs[:, c * LANES:(c + 1) * LANES] = buf[slot, pl.ds(c, tm, stride=chunks), :].astype(BF16)

        @pl.when(i < nv)
        def _():
            xv = xs[...]
            h_glu = jnp.minimum(_dot(xv, wg_ref[0]) + bg_ref[0], SWIGLU_LIMIT)
            h_lin = jnp.clip(_dot(xv, wl_ref[0]) + bl_ref[0], -SWIGLU_LIMIT, SWIGLU_LIMIT)
            o_ref[...] = (h_glu * _sigmoid(SWIGLU_ALPHA * h_glu) * (h_lin + 1.0)).astype(BF16)

        @pl.when(i >= nv)
        def _():
            o_ref[...] = jnp.zeros(o_ref.shape, BF16)

    idx_spec = pl.BlockSpec((1, 1, tm), lambda i, j, be, nv: (i, 0, 0), memory_space=pltpu.SMEM)
    nxt_spec = pl.BlockSpec((1, 1, tm), lambda i, j, be, nv: (jnp.minimum(i + 1, n_blk - 1), 0, 0), memory_space=pltpu.SMEM)
    return _pcall(
        body, name="moe_up", grid=(n_blk, f // tf), prefetch=2,
        in_specs=[idx_spec, nxt_spec, pl.BlockSpec(memory_space=pl.ANY),
                  pl.BlockSpec((1, d, tf), lambda i, j, be, nv: (be[i], 0, j)),
                  pl.BlockSpec((1, d, tf), lambda i, j, be, nv: (be[i], 0, j)),
                  pl.BlockSpec((1, 1, tf), lambda i, j, be, nv: (be[i], 0, j)),
                  pl.BlockSpec((1, 1, tf), lambda i, j, be, nv: (be[i], 0, j))],
        out_specs=pl.BlockSpec((tm, tf), lambda i, j, be, nv: (i, j)),
        out_shape=jax.ShapeDtypeStruct((n_rows, f), BF16),
        scratch=[pltpu.VMEM((2, tm * chunks, LANES), F32), pltpu.VMEM((tm, d), BF16), pltpu.SemaphoreType.DMA((2,))],
    )(blk_e, n_valid, idx3, idx3, x_chunks, w_glu, w_lin, b_glu, b_lin)


def _moe_down(blk_e, fresh, n_valid, a, w_down, b_down, *, n_rows, d, f, tm):
    chunks = d // LANES

    def body(be_ref, fr_ref, nv_ref, a_ref, w_ref, b_ref, o_ref, wb_ref):
        i = pl.program_id(0)

        @pl.when(fr_ref[i] == 1)
        def _():
            wb_ref[...] = w_ref[0].astype(BF16)

        @pl.when(i < nv_ref[0])
        def _():
            res = _dot(a_ref[...], wb_ref[...]) + b_ref[0]
            for c in range(chunks):
                o_ref[pl.ds(c, tm, stride=chunks), :] = res[:, c * LANES:(c + 1) * LANES]

        @pl.when(i >= nv_ref[0])
        def _():
            o_ref[...] = jnp.zeros(o_ref.shape, F32)

    return _pcall(
        body, name="moe_down", grid=(n_rows // tm,), prefetch=3,
        in_specs=[pl.BlockSpec((tm, f), lambda i, be, fr, nv: (i, 0)),
                  pl.BlockSpec((1, f, d), lambda i, be, fr, nv: (be[i], 0, 0)),
                  pl.BlockSpec((1, 1, d), lambda i, be, fr, nv: (be[i], 0, 0))],
        out_specs=pl.BlockSpec((tm * chunks, LANES), lambda i, be, fr, nv: (i, 0)),
        out_shape=jax.ShapeDtypeStruct((n_rows * chunks, LANES), F32),
        scratch=[pltpu.VMEM((f, d), BF16)],
    )(blk_e, fresh, n_valid, a, w_down, b_down)


def _moe_combine(pos3, ys, gates, x, ln_g, ln_b, *, n, d, tq, alpha):
    chunks = d // LANES

    def body(pos_ref, y_hbm, g_ref, x_ref, lg_ref, lb_ref, of_ref, ob_ref, buf, y_ref, sem):
        def issue(r, c):
            for kk in range(TOP_K):
                _row_copy(y_hbm, pos_ref[0, 0, kk * tq + r], buf.at[kk], r, sem, chunks).start(priority=kk % 2)
            return c
        lax.fori_loop(0, tq, issue, 0)

        def wait(r, c):
            for kk in range(TOP_K):
                _row_copy(y_hbm, 0, buf.at[kk], r, sem, chunks).wait()
            return c
        lax.fori_loop(0, tq, wait, 0)
        gv = g_ref[...]
        for c in range(chunks):
            piece = gv[:, 0:1] * buf[0, pl.ds(c, tq, stride=chunks), :]
            for kk in range(1, TOP_K):
                piece = piece + gv[:, kk:kk + 1] * buf[kk, pl.ds(c, tq, stride=chunks), :]
            y_ref[:, c * LANES:(c + 1) * LANES] = piece
        out = _layer_norm(alpha * x_ref[...] + y_ref[...], lg_ref[...], lb_ref[...])
        of_ref[...] = out
        ob_ref[...] = out.astype(BF16)

    return _pcall(
        body, name="moe_combine_ln", grid=(n // tq,),
        in_specs=[pl.BlockSpec((1, 1, TOP_K * tq), lambda i: (i, 0, 0), memory_space=pltpu.SMEM),
                  pl.BlockSpec(memory_space=pl.ANY),
                  pl.BlockSpec((tq, LANES), lambda i: (i, 0)),
                  pl.BlockSpec((tq, d), lambda i: (i, 0)),
                  pl.BlockSpec((1, d), lambda i: (0, 0)),
                  pl.BlockSpec((1, d), lambda i: (0, 0))],
        out_specs=[pl.BlockSpec((tq, d), lambda i: (i, 0)), pl.BlockSpec((tq, d), lambda i: (i, 0))],
        out_shape=[jax.ShapeDtypeStruct((n, d), F32), jax.ShapeDtypeStruct((n, d), BF16)],
        scratch=[pltpu.VMEM((TOP_K, tq * chunks, LANES), F32), pltpu.VMEM((tq, d), F32), pltpu.SemaphoreType.DMA],
    )(pos3, ys, gates, x, ln_g, ln_b)


def _t5_bucket(dist):
    max_exact = REL_BUCKETS // 2
    dd = jnp.maximum(dist, 1).astype(F32)
    large = max_exact + (jnp.log(dd / max_exact) / math.log(REL_MAX_DIST / max_exact)
                         * (REL_BUCKETS - max_exact)).astype(I32)
    large = jnp.minimum(large, REL_BUCKETS - 1)
    return jnp.where(dist < max_exact, dist, large)


def _tables(positions, rel_bias, t):
    half = MLA_ROPE // 2
    inv = 1.0 / (ROPE_THETA ** (jnp.arange(0, MLA_ROPE, 2, dtype=F32) / MLA_ROPE))
    ang = positions.astype(F32)[:, None] * inv[None, :]
    zeros = jnp.zeros((t, LANES - MLA_ROPE), F32)
    cos = jnp.concatenate([jnp.cos(ang), jnp.cos(ang), zeros], axis=-1)
    sin = jnp.concatenate([jnp.sin(ang), jnp.sin(ang), zeros], axis=-1)
    assert half * 2 == MLA_ROPE

    first = jnp.searchsorted(_t5_bucket(jnp.arange(t)), jnp.arange(REL_BUCKETS), side="left").astype(I32)
    rb = rel_bias.astype(F32)

    def bias_of(dist, valid):
        out = jnp.broadcast_to(rb[0].reshape((NSA_HEADS,) + (1,) * dist.ndim), (NSA_HEADS,) + dist.shape)
        for bkt in range(1, REL_BUCKETS):
            out = jnp.where((dist >= first[bkt])[None], rb[bkt].reshape((NSA_HEADS,) + (1,) * dist.ndim), out)
        return jnp.where(valid[None], out, NEG_INF)

    tb = NSA_TILE
    sat = rb[REL_BUCKETS - 1]
    cvec = jnp.broadcast_to(sat.reshape(NSA_GROUPS, NSA_HPG, 1), (NSA_GROUPS, NSA_HPG, 4 * tb))
    span = WINDOW + tb
    wd = (jnp.arange(WINDOW // tb + 1, dtype=I32) * tb)[:, None, None] + jnp.arange(tb, dtype=I32)[None, :, None] \
        - jnp.arange(span, dtype=I32)[None, None, :]
    wtab = bias_of(wd, (wd >= 0) & (wd < WINDOW)).transpose(1, 0, 2, 3)
    stab = bias_of(wd, wd >= 0).transpose(1, 0, 2, 3)
    nb = t // CMP_STRIDE
    dist_c = jnp.arange(t, dtype=I32)[:, None] - (jnp.arange(nb, dtype=I32) * CMP_STRIDE + CMP_LEN - 1)[None, :]
    bias_c = bias_of(dist_c, dist_c >= 0)
    tq_mla = _pick_tile(t, 512)
    cm = jnp.arange(tq_mla, dtype=I32)
    causal = jnp.where(cm[None, :] <= cm[:, None], 0.0, NEG_INF).astype(F32)

    n_slc = t // SEL_BLOCK
    tok = np.arange(nb)[:, None] * CMP_STRIDE + np.arange(CMP_LEN)[None, :]
    cover = (tok[:, :, None] // SEL_BLOCK == np.arange(LANES)[None, None, :]).sum(1) / CMP_LEN
    cover[nb - 1] = 0.0
    cover[:, n_slc:] = 0.0
    expand_t = (np.arange(t)[:, None] // SEL_BLOCK == np.arange(LANES)[None, :])
    return dict(cos=cos, sin=sin, stab=stab, cvec=cvec, wtab=wtab, bias_c=bias_c, causal=causal,
                cover=jnp.asarray(cover, BF16), expand_t=jnp.asarray(expand_t, BF16))


def _pad_heads(w, heads, width):
    k = w.shape[0]
    w = w.reshape(k, heads, width)
    return jnp.pad(w, ((0, 0), (0, 0), (0, HEAD_PAD - width))).reshape(k, heads * HEAD_PAD)


def _pick_tile(total, want):
    tile = min(total, want)
    while total % tile:
        tile //= 2
    return tile


def _token_mixer(x, xb, tabs, p, *, b, t, d):
    n = b * t
    tm = _pick_tile(t, 512)
    tpb = t // tm
    offs = np.cumsum([0, MLA_Q_LORA, MLA_KV_LORA, MLA_ROPE, NSA_HEADS * NSA_DK,
                      NSA_GROUPS * NSA_DK, NSA_GROUPS * NSA_DV, NSA_GROUPS * NSA_DK, NSA_GROUPS * NSA_DV,
                      NSA_GROUPS * NSA_DK, NSA_GROUPS * NSA_DV, NSA_HEADS * NSA_BRANCHES, 2 * d])
    w_in = p["w_in"]
    seg = [w_in[:, offs[s]:offs[s + 1]] for s in range(12)]
    x_lhs = [(xb, _row_spec(tm, d))]
    mla_scale = (MLA_NOPE + MLA_ROPE) ** -0.5
    nsa_scale = NSA_DK ** -0.5

    def cast_epi(acc, er, orr):
        orr[0][...] = acc.astype(BF16)

    lat_w = MLA_Q_LORA + MLA_KV_LORA
    w_lat = jnp.concatenate([seg[0], seg[1]], axis=1).astype(BF16)
    tn_lat = _pick_tile(lat_w, 512)
    (lat,) = _matmul("proj_latents", x_lhs, w_lat, [], [(jax.ShapeDtypeStruct((n, lat_w), BF16), _tile_spec(tm, tn_lat))],
                     _cast_prologue, cast_epi, m=n, k=d, n=lat_w, tm=tm, tn=tn_lat)

    w_small = jnp.concatenate([seg[2], jnp.zeros((d, LANES - MLA_ROPE), F32), seg[10],
                               jnp.zeros((d, LANES - NSA_HEADS * NSA_BRANCHES), F32)], axis=1).astype(BF16)

    def small_epi(acc, er, orr):
        orr[0][...] = _rope_slab(acc[:, :LANES], er[0][...], er[1][...]).astype(BF16)
        orr[1][...] = _sigmoid(acc[:, LANES:])

    cs_extras = [(tabs["cos_n"], _row_spec(tm, LANES)), (tabs["sin_n"], _row_spec(tm, LANES))]
    kr, nsa_gates = _matmul(
        "proj_rope_key_gates", x_lhs, w_small, cs_extras,
        [(jax.ShapeDtypeStruct((n, LANES), BF16), _row_spec(tm, LANES)),
         (jax.ShapeDtypeStruct((n, LANES), F32), _row_spec(tm, LANES))],
        _cast_prologue, small_epi, m=n, k=d, n=2 * LANES, tm=tm, tn=2 * LANES)

    hb = 2
    w_qn = _pad_heads(seg[3], NSA_HEADS, NSA_DK).astype(BF16)
    (q_nsa,) = _matmul("proj_nsa_q", x_lhs, w_qn, [],
                       [(jax.ShapeDtypeStruct((b, NSA_HEADS, t, HEAD_PAD), BF16), _head_spec(tm, hb, HEAD_PAD, tpb))],
                       _cast_prologue, _heads_epilogue(hb, HEAD_PAD, nsa_scale),
                       m=n, k=d, n=NSA_HEADS * HEAD_PAD, tm=tm, tn=hb * HEAD_PAD)

    w_k6 = jnp.concatenate([_pad_heads(seg[s], NSA_GROUPS, NSA_DK) for s in (4, 6, 8)], axis=1).astype(BF16)
    (k6,) = _matmul("proj_nsa_k", x_lhs, w_k6, [],
                    [(jax.ShapeDtypeStruct((b, 3 * NSA_GROUPS, t, HEAD_PAD), BF16), _head_spec(tm, hb, HEAD_PAD, tpb))],
                    _cast_prologue, _heads_epilogue(hb, HEAD_PAD),
                    m=n, k=d, n=3 * NSA_GROUPS * HEAD_PAD, tm=tm, tn=hb * HEAD_PAD)

    (v_c,) = _matmul("proj_nsa_v_cmp", x_lhs, seg[5].astype(BF16), [],
                     [(jax.ShapeDtypeStruct((b, NSA_GROUPS, t, NSA_DV), BF16), _head_spec(tm, hb, NSA_DV, tpb))],
                     _cast_prologue, _heads_epilogue(hb, NSA_DV),
                     m=n, k=d, n=NSA_GROUPS * NSA_DV, tm=tm, tn=hb * NSA_DV)

    def ones_lane(rows):
        return jnp.where(lax.broadcasted_iota(I32, (rows, HEAD_PAD - NSA_DV), 1) == 0, 1.0, 0.0).astype(BF16)

    def v_ones_epi(acc, er, orr):
        for c in range(hb):
            orr[0][0, c, :, :NSA_DV] = acc[:, c * NSA_DV:(c + 1) * NSA_DV].astype(BF16)
            orr[0][0, c, :, NSA_DV:] = ones_lane(acc.shape[0])

    w_v4 = jnp.concatenate([seg[7], seg[9]], axis=1).astype(BF16)
    (v4,) = _matmul("proj_nsa_v", x_lhs, w_v4, [],
                    [(jax.ShapeDtypeStruct((b, 2 * NSA_GROUPS, t, HEAD_PAD), BF16), _head_spec(tm, hb, HEAD_PAD, tpb))],
                    _cast_prologue, v_ones_epi, m=n, k=d, n=2 * NSA_GROUPS * NSA_DV, tm=tm, tn=hb * NSA_DV)

    def sig_epi(acc, er, orr):
        orr[0][...] = _sigmoid(acc).astype(BF16)

    tn_d = _pick_tile(d, 512)
    (merge,) = _matmul("proj_merge_gates", x_lhs, seg[11].astype(BF16), [],
                       [(jax.ShapeDtypeStruct((n, 2 * d), BF16), _tile_spec(tm, tn_d))],
                       _cast_prologue, sig_epi, m=n, k=d, n=2 * d, tm=tm, tn=tn_d)

    w_uq = p["mla_w_uq"].reshape(MLA_Q_LORA, MLA_HEADS, MLA_NOPE + MLA_ROPE)
    w_uq = jnp.pad(w_uq, ((0, 0), (0, 0), (0, HEAD_PAD - MLA_NOPE - MLA_ROPE))).reshape(MLA_Q_LORA, MLA_HEADS * HEAD_PAD)

    def q_epi(acc, er, orr):
        cos, sin = er[0][...], er[1][...]
        for c in range(hb):
            orr[0][0, c, :, :MLA_NOPE] = (acc[:, c * HEAD_PAD:c * HEAD_PAD + MLA_NOPE] * mla_scale).astype(BF16)
            slab = acc[:, c * HEAD_PAD + MLA_NOPE:(c + 1) * HEAD_PAD] * mla_scale
            orr[0][0, c, :, MLA_NOPE:] = _rope_slab(slab, cos, sin).astype(BF16)

    (q_mla,) = _matmul(
        "mla_q_up", [(lat, _row_spec(tm, MLA_Q_LORA)), (p["mla_q_norm"].reshape(1, -1), pl.BlockSpec((1, MLA_Q_LORA), lambda i, j: (0, 0)))],
        w_uq.astype(BF16), cs_extras,
        [(jax.ShapeDtypeStruct((b, MLA_HEADS, t, HEAD_PAD), BF16), _head_spec(tm, hb, HEAD_PAD, tpb))],
        _rms_prologue, q_epi, m=n, k=MLA_Q_LORA, n=MLA_HEADS * HEAD_PAD, tm=tm, tn=hb * HEAD_PAD)

    kv_w = MLA_NOPE + MLA_V

    def kv_epi(acc, er, orr):
        k_rope = er[0][...]
        for c in range(hb):
            orr[0][0, c, :, :MLA_NOPE] = acc[:, c * kv_w:c * kv_w + MLA_NOPE].astype(BF16)
            orr[0][0, c, :, MLA_NOPE:] = k_rope
            orr[1][0, c, :, :MLA_V] = acc[:, c * kv_w + MLA_NOPE:(c + 1) * kv_w].astype(BF16)
            orr[1][0, c, :, MLA_V:] = ones_lane(acc.shape[0])

    k_mla, v_mla = _matmul(
        "mla_kv_up", [(lat, _row_spec(tm, MLA_KV_LORA, MLA_Q_LORA // MLA_KV_LORA)),
                      (p["mla_kv_norm"].reshape(1, -1), pl.BlockSpec((1, MLA_KV_LORA), lambda i, j: (0, 0)))],
        p["mla_w_ukv"].astype(BF16), [(kr, _row_spec(tm, LANES))],
        [(jax.ShapeDtypeStruct((b, MLA_HEADS, t, HEAD_PAD), BF16), _head_spec(tm, hb, HEAD_PAD, tpb)),
         (jax.ShapeDtypeStruct((b, MLA_HEADS, t, HEAD_PAD), BF16), _head_spec(tm, hb, HEAD_PAD, tpb))],
        _rms_prologue, kv_epi, m=n, k=MLA_KV_LORA, n=MLA_HEADS * kv_w, tm=tm, tn=hb * kv_w)

    attn_a = _mla_attention(q_mla, k_mla, v_mla, tabs["causal"], b=b, t=t, tq=tabs["causal"].shape[0])

    def ya_epi(acc, er, orr):
        orr[0][...] = (er[0][...].astype(F32) * acc).astype(BF16)

    (y_a,) = _matmul("mla_branch_out", [(attn_a.reshape(n, MLA_HEADS * MLA_V), _row_spec(tm, MLA_HEADS * MLA_V))],
                     p["w_branch_a"].astype(BF16), [(merge, _tile_spec(tm, tn_d))],
                     [(jax.ShapeDtypeStruct((n, d), BF16), _tile_spec(tm, tn_d))],
                     _cast_prologue, ya_epi, m=n, k=MLA_HEADS * MLA_V, n=d, tm=tm, tn=tn_d)

    def cmp_weights(pe, w1, w2, width_in, width):
        pe_p = jnp.pad(pe, ((0, 0), (0, width - width_in)))
        pe2 = jnp.zeros((8, CMP_STRIDE * width), F32).at[0].set(pe_p[:CMP_STRIDE].reshape(-1)).at[1].set(pe_p[CMP_STRIDE:].reshape(-1))
        w1p = jnp.pad(w1.reshape(CMP_LEN, width_in, width_in), ((0, 0), (0, width - width_in), (0, width - width_in)))
        w2p = jnp.pad(w2, ((0, width - width_in), (0, width - width_in)))
        return (pe2.astype(BF16), w1p[:CMP_STRIDE].reshape(-1, width).astype(BF16),
                w1p[CMP_STRIDE:].reshape(-1, width).astype(BF16), w2p.astype(BF16))

    kcmp = _compress(k6, *cmp_weights(p["cmp_pe_k"], p["cmp_w1_k"], p["cmp_w2_k"], NSA_DK, HEAD_PAD), b=b, t=t, width=HEAD_PAD)
    vcmp = _compress(v_c, *cmp_weights(p["cmp_pe_v"], p["cmp_w1_v"], p["cmp_w2_v"], NSA_DV, NSA_DV), b=b, t=t, width=NSA_DV)
    gates3 = nsa_gates.reshape(b, t, LANES)
    o_cmp, sel = _nsa_cmp(q_nsa, kcmp, vcmp, tabs["bias_c"], tabs["cover"], gates3, b=b, t=t)
    o_slc = _nsa_sel(q_nsa, k6, v4, tabs["stab"], tabs["cvec"], gates3, sel, tabs["expand_t"], b=b, t=t)
    o_win = _nsa_win(q_nsa, k6, v4, tabs["wtab"], gates3, b=b, t=t)

    def sum3_prologue(a_ref, b_ref, c_ref):
        return (a_ref[...].astype(F32) + b_ref[...].astype(F32) + c_ref[...].astype(F32)).astype(BF16)

    def merge_epi(acc, er, orr):
        orr[0][...] = (er[0][...].astype(F32) + er[1][...].astype(F32) * acc).astype(BF16)

    hv = NSA_HEADS * NSA_DV
    (merged,) = _matmul("nsa_branch_out_merge", [(o.reshape(n, hv), _row_spec(tm, hv)) for o in (o_cmp, o_slc, o_win)],
                        p["w_branch_b"].astype(BF16), [(y_a, _tile_spec(tm, tn_d)), (merge, _tile_spec(tm, tn_d, d // tn_d))],
                        [(jax.ShapeDtypeStruct((n, d), BF16), _tile_spec(tm, tn_d))],
                        sum3_prologue, merge_epi, m=n, k=hv, n=d, tm=tm, tn=tn_d)

    tm_ln = _pick_tile(t, 256)
    alpha = p["alpha"]

    chunks = d // LANES

    def ln_epi(acc, er, orr):
        out = _layer_norm(alpha * er[0][...] + acc, er[1][...], er[2][...])
        orr[0][...] = out
        for c in range(chunks):
            orr[1][pl.ds(c, tm_ln, stride=chunks), :] = out[:, c * LANES:(c + 1) * LANES]

    vec_spec = pl.BlockSpec((1, d), lambda i, j: (0, 0))
    x1, x1_chunks = _matmul(
        "mixer_out_ln", [(merged, _row_spec(tm_ln, d))], p["w_out"].astype(BF16),
        [(x, _row_spec(tm_ln, d)), (p["ln1_g"].reshape(1, d), vec_spec), (p["ln1_b"].reshape(1, d), vec_spec)],
        [(jax.ShapeDtypeStruct((n, d), F32), _row_spec(tm_ln, d)),
         (jax.ShapeDtypeStruct((n * chunks, LANES), F32), pl.BlockSpec((tm_ln * chunks, LANES), lambda i, j: (i, 0)))],
        _cast_prologue, ln_epi, m=n, k=d, n=d, tm=tm_ln, tn=d)
    return x1, x1_chunks


def _moe(x, x_chunks, p, *, n, d):
    n_exp, _, f2 = p["exp_w_up"].shape
    f = f2 // 2
    tm_e = p["tm_e"]
    rw = jnp.pad(p["router_w"], ((0, 0), (0, LANES - n_exp)))
    rw_hi = rw.astype(BF16)
    rw_lo = (rw - rw_hi.astype(F32)).astype(BF16)
    rb = jnp.pad(p["router_b"], (0, LANES - n_exp)).reshape(1, LANES)
    top_e, gates = _router(x, rw_hi, rw_lo, rb, n=n, d=d, n_exp=n_exp, tm=_pick_tile(n, 512))

    n_pairs = n * TOP_K
    e_flat = top_e[:, :TOP_K].reshape(-1)
    grp = LANES
    onehot = (e_flat[:, None] == jnp.arange(n_exp, dtype=I32)[None, :]).astype(F32).reshape(n_pairs // grp, grp, n_exp)
    tri = jnp.tril(jnp.ones((grp, grp), F32))
    within = jnp.einsum("ij,gje->gie", tri, onehot)
    totals = within[:, -1, :]
    before = jnp.cumsum(totals, axis=0) - totals
    rank = (jnp.sum((within + before[:, None, :]) * onehot, axis=-1).reshape(-1) - 1.0).astype(I32)
    counts = jnp.sum(totals, axis=0).astype(I32)
    padded = (counts + tm_e - 1) // tm_e * tm_e
    pad_end = jnp.cumsum(padded)
    dest = (pad_end - padded)[e_flat] + rank
    n_rows = n_pairs + n_exp * tm_e
    n_blk = n_rows // tm_e
    row_tok = jnp.zeros((n_rows,), I32).at[dest].set(jnp.arange(n_pairs, dtype=I32) // TOP_K)
    n_valid = (pad_end[-1] // tm_e).astype(I32)
    blk_e = jnp.searchsorted(pad_end, jnp.arange(n_blk, dtype=I32) * tm_e, side="right").astype(I32)
    blk_e = jnp.minimum(blk_e, blk_e[jnp.maximum(n_valid - 1, 0)])
    n_valid = n_valid.reshape(1)

    w_up = p["exp_w_up"].reshape(n_exp, d, f, 2)
    b_up = p["exp_b_up"].reshape(n_exp, 1, f, 2)
    tf = _pick_tile(f, 512)
    a = _moe_up(blk_e, n_valid, x_chunks, row_tok.reshape(n_blk, 1, tm_e),
                w_up[..., 0].astype(BF16), w_up[..., 1].astype(BF16), b_up[..., 0], b_up[..., 1],
                n_rows=n_rows, d=d, f=f, tm=tm_e, tf=tf)
    split = max(1, tm_e // 256)
    tm_d = tm_e // split
    blk_d = jnp.repeat(blk_e, split)
    fresh = jnp.concatenate([jnp.ones((1,), I32), (blk_d[1:] != blk_d[:-1]).astype(I32)])
    ys = _moe_down(blk_d, fresh, n_valid * split, a, p["exp_w_down"], p["exp_b_down"].reshape(n_exp, 1, d),
                   n_rows=n_rows, d=d, f=f, tm=tm_d)

    tq = _pick_tile(n, 128)
    pos3 = dest.reshape(n // tq, tq, TOP_K).transpose(0, 2, 1).reshape(n // tq, 1, TOP_K * tq)
    return _moe_combine(pos3, ys, gates, x, p["ln2_g"].reshape(1, d), p["ln2_b"].reshape(1, d),
                        n=n, d=d, tq=tq, alpha=p["alpha"])


def kernel(x, positions, rel_bias, w_in, mla_q_norm, mla_w_uq, mla_kv_norm, mla_w_ukv, cmp_pe_k, cmp_w1_k, cmp_w2_k, cmp_pe_v, cmp_w1_v, cmp_w2_v, w_branch_a, w_branch_b, w_out, ln1_g, ln1_b, router_w, router_b, exp_w_up, exp_b_up, exp_w_down, exp_b_down, ln2_g, ln2_b):
    b, t, d = x.shape
    depth = w_in.shape[0]
    n = b * t
    alpha = (2 * depth) ** 0.25
    tabs = _tables(positions, rel_bias, t)
    tabs["cos_n"] = jnp.tile(tabs["cos"], (b, 1))
    tabs["sin_n"] = jnp.tile(tabs["sin"], (b, 1))
    xf = x.reshape(n, d)
    xb = xf.astype(BF16)
    for l in range(depth):
        p = dict(w_in=w_in[l], mla_q_norm=mla_q_norm[l], mla_w_uq=mla_w_uq[l], mla_kv_norm=mla_kv_norm[l],
                 mla_w_ukv=mla_w_ukv[l], cmp_pe_k=cmp_pe_k[l], cmp_w1_k=cmp_w1_k[l], cmp_w2_k=cmp_w2_k[l],
                 cmp_pe_v=cmp_pe_v[l], cmp_w1_v=cmp_w1_v[l], cmp_w2_v=cmp_w2_v[l], w_branch_a=w_branch_a[l],
                 w_branch_b=w_branch_b[l], w_out=w_out[l], ln1_g=ln1_g[l], ln1_b=ln1_b[l], router_w=router_w[l],
                 router_b=router_b[l], exp_w_up=exp_w_up[l], exp_b_up=exp_b_up[l], exp_w_down=exp_w_down[l],
                 exp_b_down=exp_b_down[l], ln2_g=ln2_g[l], ln2_b=ln2_b[l], alpha=alpha, tm_e=_pick_tile(n * TOP_K, 512))
        x1, x1_chunks = _token_mixer(xf, xb, tabs, p, b=b, t=t, d=d)
        xf, xb = _moe(x1, x1_chunks, p, n=n, d=d)
    return xf.reshape(b, t, d)
```

```python
import functools
import math

import numpy as np
import jax
import jax.numpy as jnp
from jax import lax
from jax.experimental import pallas as pl
from jax.experimental.pallas import tpu as pltpu

F32, BF16, I32 = jnp.float32, jnp.bfloat16, jnp.int32

MLA_HEADS = 16
MLA_Q_LORA = 1536
MLA_KV_LORA = 512
MLA_NOPE = 128
MLA_ROPE = 64
MLA_V = 128
ROPE_THETA = 10000.0
NSA_HEADS = 16
NSA_GROUPS = 2
NSA_HPG = NSA_HEADS // NSA_GROUPS
NSA_DK = 192
NSA_DV = 128
CMP_LEN = 32
CMP_STRIDE = 16
SEL_BLOCK = 64
SEL_COUNT = 16
WINDOW = 512
NSA_BRANCHES = 3
SEL_FORCE = 1.0e4
REL_BUCKETS = 32
REL_MAX_DIST = 128
TOP_K = 4
SWIGLU_LIMIT = 7.0
SWIGLU_ALPHA = 1.702
NEG_INF = -1.0e30
LN_EPS = 1e-5
RMS_EPS = 1e-6

LANES = 128
HEAD_PAD = 256
NSA_TILE = 128
GATHER_UNROLL = 8
VMEM_LIMIT_BYTES = 56 * 1024 * 1024


def _pcall(body, *, name, grid, in_specs, out_specs, out_shape, scratch=(), prefetch=0):
    gs = pltpu.PrefetchScalarGridSpec(num_scalar_prefetch=prefetch, grid=grid, in_specs=in_specs,
                                      out_specs=out_specs, scratch_shapes=list(scratch))
    return pl.pallas_call(
        body, grid_spec=gs, out_shape=out_shape, name=name,
        compiler_params=pltpu.CompilerParams(dimension_semantics=("arbitrary",) * len(grid),
                                             vmem_limit_bytes=VMEM_LIMIT_BYTES))


def _sigmoid(x):
    return 1.0 / (1.0 + jnp.exp(-x))


def _dot_t(a, b):
    return lax.dot_general(a, b, (((1,), (1,)), ((), ())), preferred_element_type=F32)


def _dot(a, b):
    return jnp.dot(a, b, preferred_element_type=F32)


def _layer_norm(r, g, b):
    mu = jnp.mean(r, axis=-1, keepdims=True)
    c = r - mu
    var = jnp.mean(c * c, axis=-1, keepdims=True)
    return c * lax.rsqrt(var + LN_EPS) * g + b


def _matmul(name, lhs, w, extras, outs, prologue, epilogue, *, m, k, n, tm, tn):
    nl, ne, no = len(lhs), len(extras), len(outs)

    def body(*refs):
        lr = refs[:nl]
        wr = refs[nl]
        er = refs[nl + 1:nl + 1 + ne]
        orr = refs[nl + 1 + ne:nl + 1 + ne + no]
        xs = refs[-1]

        @pl.when(pl.program_id(1) == 0)
        def _():
            xs[...] = prologue(*lr)

        epilogue(_dot(xs[...], wr[...]), er, orr)

    return _pcall(
        body, name=name, grid=(m // tm, n // tn),
        in_specs=[s for _, s in lhs] + [pl.BlockSpec((k, tn), lambda i, j: (0, j))] + [s for _, s in extras],
        out_specs=[s for _, s in outs], out_shape=[o for o, _ in outs],
        scratch=[pltpu.VMEM((tm, k), BF16)],
    )(*[a for a, _ in lhs], w, *[a for a, _ in extras])


def _row_spec(tm, width, col=0):
    return pl.BlockSpec((tm, width), lambda i, j: (i, col))


def _tile_spec(tm, tn, col_off=0):
    return pl.BlockSpec((tm, tn), lambda i, j: (i, j + col_off))


def _head_spec(tm, hb, width, tpb):
    return pl.BlockSpec((1, hb, tm, width), lambda i, j: (i // tpb, j, i % tpb, 0))


def _cast_prologue(x_ref):
    return x_ref[...].astype(BF16)


def _rms_prologue(x_ref, g_ref):
    xf = x_ref[...].astype(F32)
    ms = jnp.mean(xf * xf, axis=-1, keepdims=True)
    return (xf * lax.rsqrt(ms + RMS_EPS) * g_ref[...]).astype(BF16)


def _rope_slab(t, cos, sin):
    lane = lax.broadcasted_iota(I32, t.shape, 1)
    rot = jnp.where(lane < MLA_ROPE // 2, -pltpu.roll(t, LANES - MLA_ROPE // 2, 1), pltpu.roll(t, MLA_ROPE // 2, 1))
    return t * cos + rot * sin


def _heads_epilogue(hb, width, scale=None):
    def epi(acc, er, orr):
        for c in range(hb):
            v = acc[:, c * width:(c + 1) * width]
            if scale is not None:
                v = v * scale
            orr[0][0, c] = v.astype(BF16)
    return epi


def _softmax_init(rows, width):
    return (jnp.full((rows, 1), NEG_INF, F32), jnp.zeros((rows, 1), F32), jnp.zeros((rows, width), F32))


def _softmax_step(s, v, carry):
    m, l, acc = carry
    m_new = jnp.maximum(m, jnp.max(s, axis=-1, keepdims=True))
    a = jnp.exp(m - m_new)
    p = jnp.exp(s - m_new)
    return m_new, a * l + jnp.sum(p, axis=-1, keepdims=True), a * acc + _dot(p.astype(BF16), v)


def _mla_attention(q, k, v_ones, causal, *, b, t, tq):
    nq = t // tq

    def body(q_ref, k_ref, v_ref, c_ref, o_ref):
        i = pl.program_id(2)
        qv = q_ref[0, 0]

        def step(koff, carry, diagonal):
            s = _dot_t(qv, k_ref[0, 0, pl.ds(koff, tq), :])
            if diagonal:
                s = s + c_ref[...]
            return _softmax_step(s, v_ref[0, 0, pl.ds(koff, tq), :], carry)

        carry = lax.fori_loop(0, i, lambda j, c: step(pl.multiple_of(j * tq, tq), c, False), _softmax_init(tq, MLA_V))
        _, l, acc = step(pl.multiple_of(i * tq, tq), carry, True)
        o_ref[0] = (acc / l).astype(BF16)

    return _pcall(
        body, name="mla_attention", grid=(b, MLA_HEADS, nq),
        in_specs=[pl.BlockSpec((1, 1, tq, HEAD_PAD), lambda bi, h, i: (bi, h, i, 0)),
                  pl.BlockSpec((1, 1, t, HEAD_PAD), lambda bi, h, i: (bi, h, 0, 0)),
                  pl.BlockSpec((1, 1, t, MLA_V), lambda bi, h, i: (bi, h, 0, 0)),
                  pl.BlockSpec((tq, tq), lambda bi, h, i: (0, 0))],
        out_specs=pl.BlockSpec((1, tq, MLA_V), lambda bi, h, i: (bi, i, h)),
        out_shape=jax.ShapeDtypeStruct((b, t, MLA_HEADS * MLA_V), BF16),
    )(q, k, v_ones, causal)


def _gelu_tanh(x):
    return 0.5 * x * (1.0 + jnp.tanh(math.sqrt(2.0 / math.pi) * (x + 0.044715 * (x * x * x))))


def _compress(kv6, pe2, w1lo, w1hi, w2, *, b, t, width):
    nb = t // CMP_STRIDE
    half = CMP_STRIDE * width
    h = kv6.reshape(b, kv6.shape[1], nb, half)

    def body(h_ref, pe_ref, lo_ref, hi_ref, w2_ref, o_ref):
        hv = h_ref[0, 0]
        lo, hi = lo_ref[...], hi_ref[...]
        a = _dot(hv, lo)
        bb = _dot(hv, hi)
        pe = pe_ref[...]
        c = _dot(pe, lo)[0:1] + _dot(pe, hi)[1:2]
        z = a + pltpu.roll(bb, nb - 1, 0) + c
        o_ref[0, 0] = _dot(_gelu_tanh(z).astype(BF16), w2_ref[...]).astype(BF16)

    return _pcall(
        body, name=f"nsa_compress_{width}", grid=(b, NSA_GROUPS),
        in_specs=[
            pl.BlockSpec((1, 1, nb, half), lambda bi, g: (bi, g, 0, 0)),
            pl.BlockSpec((8, half), lambda bi, g: (0, 0)),
            pl.BlockSpec((half, width), lambda bi, g: (0, 0)),
            pl.BlockSpec((half, width), lambda bi, g: (0, 0)),
            pl.BlockSpec((width, width), lambda bi, g: (0, 0)),
        ],
        out_specs=pl.BlockSpec((1, 1, nb, width), lambda bi, g: (bi, g, 0, 0)),
        out_shape=jax.ShapeDtypeStruct((b, NSA_GROUPS, nb, width), BF16),
    )(h, pe2, w1lo, w1hi, w2)


def _gate_col(gates, col):
    lane = lax.broadcasted_iota(I32, gates.shape, 1)
    return jnp.sum(jnp.where(lane == col, gates, 0.0), axis=-1, keepdims=True)


def _nsa_cmp(q, kcmp, vcmp, bias_c, cover, gates, *, b, t):
    tq = NSA_TILE
    nb = t // CMP_STRIDE
    n_slc = t // SEL_BLOCK
    n_sel = min(SEL_COUNT, n_slc)
    hpg = NSA_HPG

    def body(q_ref, k_ref, v_ref, bias_ref, cov_ref, g_ref, o_ref, sel_ref):
        g = pl.program_id(1)
        i = pl.program_id(2)
        q2 = q_ref[0].reshape(hpg * tq, HEAD_PAD)
        s3 = _dot_t(q2, k_ref[0, 0]).reshape(hpg, tq, nb) + bias_ref[...]
        p = jnp.exp(s3 - jnp.max(s3, axis=-1, keepdims=True))
        has_key = jnp.where(i * tq + lax.broadcasted_iota(I32, (tq, 1), 0) >= CMP_LEN - 1, 1.0, 0.0)
        p = p * (has_key / jnp.maximum(jnp.sum(p, axis=-1, keepdims=True), 1e-30))
        o3 = _dot(p.reshape(hpg * tq, nb).astype(BF16), v_ref[0, 0]).reshape(hpg, tq, NSA_DV)
        _write_gated_heads(o_ref, o3, g_ref[0], g, 0)

        psum = jnp.sum(p, axis=0)
        p_hi = psum.astype(BF16)
        p_lo = (psum - p_hi.astype(F32)).astype(BF16)
        cov = cov_ref[...]
        imp = _dot(p_hi, cov) + _dot(p_lo, cov)

        jl = lax.broadcasted_iota(I32, (tq, LANES), 1)
        cur = (i * tq + lax.broadcasted_iota(I32, (tq, LANES), 0)) // SEL_BLOCK
        forced = (jl == 0) | (jl == cur) | (jl == cur - 1)
        score = jnp.where(forced, SEL_FORCE, jnp.where(jl <= cur, imp, -1.0))
        n_used = -(-n_slc // 8) * 8
        st = jnp.transpose(score)[:n_used]
        jrow = lax.broadcasted_iota(I32, (n_used, tq), 0)
        rank = jnp.zeros((n_used, tq), F32)
        for kk in range(n_slc):
            row = st[kk:kk + 1, :]
            rank = rank + jnp.where(jrow > kk, jnp.where(row >= st, 1.0, 0.0), jnp.where(row > st, 1.0, 0.0))
        chosen_t = jnp.where((rank < n_sel) & (st >= 0.0), 1.0, 0.0)
        if n_used < LANES:
            chosen_t = jnp.concatenate([chosen_t, jnp.zeros((LANES - n_used, tq), F32)], axis=0)
        sel_ref[0, 0] = jnp.transpose(chosen_t).astype(BF16)

    return _pcall(
        body, name="nsa_cmp_select", grid=(b, NSA_GROUPS, t // tq),
        in_specs=[
            pl.BlockSpec((1, hpg, tq, HEAD_PAD), lambda bi, g, i: (bi, g, i, 0)),
            pl.BlockSpec((1, 1, nb, HEAD_PAD), lambda bi, g, i: (bi, g, 0, 0)),
            pl.BlockSpec((1, 1, nb, NSA_DV), lambda bi, g, i: (bi, g, 0, 0)),
            pl.BlockSpec((hpg, tq, nb), lambda bi, g, i: (g, i, 0)),
            pl.BlockSpec((nb, LANES), lambda bi, g, i: (0, 0)),
            pl.BlockSpec((1, tq, LANES), lambda bi, g, i: (bi, i, 0)),
        ],
        out_specs=[
            pl.BlockSpec((1, tq, hpg * NSA_DV), lambda bi, g, i: (bi, i, g)),
            pl.BlockSpec((1, 1, tq, LANES), lambda bi, g, i: (bi, g, i, 0)),
        ],
        out_shape=[jax.ShapeDtypeStruct((b, t, NSA_HEADS * NSA_DV), BF16),
                   jax.ShapeDtypeStruct((b, NSA_GROUPS, t, LANES), BF16)],
    )(q, kcmp, vcmp, bias_c, cover, gates)


def _write_gated_heads(o_ref, o3, gates_v, g, branch):
    for hh in range(NSA_HPG):
        gc = _gate_col(gates_v, (g * NSA_HPG + hh) * NSA_BRANCHES + branch)
        o_ref[0, :, hh * NSA_DV:(hh + 1) * NSA_DV] = (o3[hh] * gc).astype(BF16)


def _nsa_sel(q, k6, v4, stab, cvec, gates, sel, expand_t, *, b, t):
    tb = NSA_TILE
    hpg = NSA_HPG
    wide = 4 * tb
    span = WINDOW + tb
    back = WINDOW // tb
    rows = hpg * tb

    def body(q_ref, k_ref, v_ref, tab_ref, cv_ref, g_ref, sel_ref, et_ref, o_ref):
        g = pl.program_id(1)
        i = pl.program_id(2)
        q2 = q_ref[0].reshape(rows, HEAD_PAD)
        sel_v = sel_ref[0, 0]
        far_end = jnp.maximum(i - back, 0) * tb

        def far(j, carry):
            koff = pl.multiple_of(j * wide, wide)
            s3 = _dot_t(q2, k_ref[0, 0, pl.ds(koff, wide), :]).reshape(hpg, tb, wide) + cv_ref[0][:, None, :]
            key = koff + lax.broadcasted_iota(I32, (tb, wide), 1)
            chosen = jnp.where(key < far_end, _dot_t(sel_v, et_ref[pl.ds(koff, wide), :]), 0.0) > 0.5
            s3 = jnp.where(chosen[None], s3, NEG_INF)
            return _softmax_step(s3.reshape(rows, wide), v_ref[0, 0, pl.ds(koff, wide), :], carry)

        carry = lax.fori_loop(0, (far_end + wide - 1) // wide, far, _softmax_init(rows, NSA_DV))

        koff = pl.multiple_of(far_end, tb)
        s3 = _dot_t(q2, k_ref[0, 0, pl.ds(koff, span), :]).reshape(hpg, tb, span) + tab_ref[0]
        chosen = _dot_t(sel_v, et_ref[pl.ds(koff, span), :]) > 0.5
        s3 = jnp.where(chosen[None], s3, NEG_INF)
        _, l, acc = _softmax_step(s3.reshape(rows, span), v_ref[0, 0, pl.ds(koff, span), :], carry)
        _write_gated_heads(o_ref, (acc / l).reshape(hpg, tb, NSA_DV), g_ref[0], g, 1)

    return _pcall(
        body, name="nsa_sel_attention", grid=(b, NSA_GROUPS, t // tb),
        in_specs=[
            pl.BlockSpec((1, hpg, tb, HEAD_PAD), lambda bi, g, i: (bi, g, i, 0)),
            pl.BlockSpec((1, 1, t, HEAD_PAD), lambda bi, g, i: (bi, NSA_GROUPS + g, 0, 0)),
            pl.BlockSpec((1, 1, t, NSA_DV), lambda bi, g, i: (bi, g, 0, 0)),
            pl.BlockSpec((1, hpg, tb, span), lambda bi, g, i: (jnp.minimum(i, back), g, 0, 0)),
            pl.BlockSpec((1, hpg, wide), lambda bi, g, i: (g, 0, 0)),
            pl.BlockSpec((1, tb, LANES), lambda bi, g, i: (bi, i, 0)),
            pl.BlockSpec((1, 1, tb, LANES), lambda bi, g, i: (bi, g, i, 0)),
            pl.BlockSpec((t, LANES), lambda bi, g, i: (0, 0)),
        ],
        out_specs=pl.BlockSpec((1, tb, hpg * NSA_DV), lambda bi, g, i: (bi, i, g)),
        out_shape=jax.ShapeDtypeStruct((b, t, NSA_HEADS * NSA_DV), BF16),
    )(q, k6, v4, stab, cvec, gates, sel, expand_t)


def _nsa_win(q, k6, v4, wtab, gates, *, b, t):
    tb = NSA_TILE
    hpg = NSA_HPG
    span = WINDOW + tb
    back = WINDOW // tb
    rows = hpg * tb

    def body(q_ref, k_ref, v_ref, w_ref, g_ref, o_ref):
        g = pl.program_id(1)
        i = pl.program_id(2)
        koff = pl.multiple_of(jnp.maximum(i - back, 0) * tb, tb)
        q2 = q_ref[0].reshape(rows, HEAD_PAD)
        s3 = _dot_t(q2, k_ref[0, 0, pl.ds(koff, span), :]).reshape(hpg, tb, span) + w_ref[0]
        s = s3.reshape(rows, span)
        p = jnp.exp(s - jnp.max(s, axis=-1, keepdims=True))
        acc = _dot(p.astype(BF16), v_ref[0, 0, pl.ds(koff, span), :])
        o3 = (acc[:, :NSA_DV] / acc[:, NSA_DV:NSA_DV + 1]).reshape(hpg, tb, NSA_DV)
        _write_gated_heads(o_ref, o3, g_ref[0], g, 2)

    return _pcall(
        body, name="nsa_win_attention", grid=(b, NSA_GROUPS, t // tb),
        in_specs=[
            pl.BlockSpec((1, hpg, tb, HEAD_PAD), lambda bi, g, i: (bi, g, i, 0)),
            pl.BlockSpec((1, 1, t, HEAD_PAD), lambda bi, g, i: (bi, 2 * NSA_GROUPS + g, 0, 0)),
            pl.BlockSpec((1, 1, t, HEAD_PAD), lambda bi, g, i: (bi, NSA_GROUPS + g, 0, 0)),
            pl.BlockSpec((1, hpg, tb, span), lambda bi, g, i: (jnp.minimum(i, back), g, 0, 0)),
            pl.BlockSpec((1, tb, LANES), lambda bi, g, i: (bi, i, 0)),
        ],
        out_specs=pl.BlockSpec((1, tb, hpg * NSA_DV), lambda bi, g, i: (bi, i, g)),
        out_shape=jax.ShapeDtypeStruct((b, t, NSA_HEADS * NSA_DV), BF16),
    )(q, k6, v4, wtab, gates)


def _router(x, w_hi, w_lo, bias, *, n, d, n_exp, tm):
    def body(x_ref, wh_ref, wl_ref, b_ref, e_ref, gate_ref):
        xv = x_ref[...]
        x_hi = xv.astype(BF16)
        x_lo = (xv - x_hi.astype(F32)).astype(BF16)
        logits = _dot(x_hi, wh_ref[...]) + _dot(x_lo, wh_ref[...]) + _dot(x_hi, wl_ref[...]) + b_ref[...]
        lane = lax.broadcasted_iota(I32, (tm, LANES), 1)
        logits = jnp.where(lane < n_exp, logits, NEG_INF)
        e_out = jnp.zeros((tm, LANES), I32)
        v_out = jnp.full((tm, LANES), NEG_INF, F32)
        for kk in range(TOP_K):
            mx = jnp.max(logits, axis=-1, keepdims=True)
            idx = jnp.min(jnp.where(logits == mx, lane, LANES), axis=-1, keepdims=True)
            e_out = jnp.where(lane == kk, idx, e_out)
            v_out = jnp.where(lane == kk, mx, v_out)
            logits = jnp.where(lane == idx, NEG_INF, logits)
        ex = jnp.where(lane < TOP_K, jnp.exp(v_out - jnp.max(v_out, axis=-1, keepdims=True)), 0.0)
        e_ref[...] = e_out
        gate_ref[...] = ex / jnp.sum(ex, axis=-1, keepdims=True)

    return _pcall(
        body, name="moe_router", grid=(n // tm,),
        in_specs=[pl.BlockSpec((tm, d), lambda i: (i, 0)),
                  pl.BlockSpec((d, LANES), lambda i: (0, 0)),
                  pl.BlockSpec((d, LANES), lambda i: (0, 0)),
                  pl.BlockSpec((1, LANES), lambda i: (0, 0))],
        out_specs=[pl.BlockSpec((tm, LANES), lambda i: (i, 0)), pl.BlockSpec((tm, LANES), lambda i: (i, 0))],
        out_shape=[jax.ShapeDtypeStruct((n, LANES), I32), jax.ShapeDtypeStruct((n, LANES), F32)],
    )(x, w_hi, w_lo, bias)


def _row_copy(src_hbm, row, dst, slot, sem, chunks):
    src = src_hbm.at[pl.ds(pl.multiple_of(row * chunks, chunks), chunks)]
    return pltpu.make_async_copy(src, dst.at[pl.ds(pl.multiple_of(slot * chunks, chunks), chunks)], sem)


def _moe_up(blk_e, n_valid, x_chunks, idx3, w_up, b_up, pick_even, *, n_rows, d, f, tm, tf):
    chunks = d // LANES
    n_blk = n_rows // tm

    def body(be_ref, nv_ref, idx_ref, nxt_ref, x_hbm, w_ref, b_ref, pick_ref, o_ref, buf, xs, sem):
        i = pl.program_id(0)
        j = pl.program_id(1)
        nv = nv_ref[0]
        slot = i % 2

        def start_rows(src_idx, s):
            def issue(r8, c):
                for u in range(GATHER_UNROLL):
                    r = r8 * GATHER_UNROLL + u
                    _row_copy(x_hbm, src_idx[0, 0, r], buf.at[s], r, sem.at[s], chunks).start(priority=u % 2)
                return c
            lax.fori_loop(0, tm // GATHER_UNROLL, issue, 0)

        @pl.when((j == 0) & (i < nv))
        def _():
            @pl.when(i == 0)
            def _():
                start_rows(idx_ref, 0)

            def wait(r8, c):
                for u in range(GATHER_UNROLL):
                    _row_copy(x_hbm, 0, buf.at[slot], r8 * GATHER_UNROLL + u, sem.at[slot], chunks).wait()
                return c
            lax.fori_loop(0, tm // GATHER_UNROLL, wait, 0)

            @pl.when(i + 1 < nv)
            def _():
                start_rows(nxt_ref, 1 - slot)

            for c in range(chunks):
                xs[:, c * LANES:(c + 1) * LANES] = buf[slot, pl.ds(c, tm, stride=chunks), :].astype(BF16)

        @pl.when(i < nv)
        def _():
            h = _dot(xs[...], w_ref[0]) + b_ref[0]
            g = jnp.minimum(h, SWIGLU_LIMIT)
            glu = g * _sigmoid(SWIGLU_ALPHA * g)
            lin = jnp.clip(h, -SWIGLU_LIMIT, SWIGLU_LIMIT) + 1.0
            pairs = [glu[:, s * LANES:(s + 1) * LANES] * pltpu.roll(lin[:, s * LANES:(s + 1) * LANES], LANES - 1, 1)
                     for s in range(2 * tf // LANES)]
            o_ref[...] = _dot(jnp.concatenate(pairs, axis=1).astype(BF16), pick_ref[...]).astype(BF16)

        @pl.when(i >= nv)
        def _():
            o_ref[...] = jnp.zeros(o_ref.shape, BF16)

    idx_spec = pl.BlockSpec((1, 1, tm), lambda i, j, be, nv: (i, 0, 0), memory_space=pltpu.SMEM)
    nxt_spec = pl.BlockSpec((1, 1, tm), lambda i, j, be, nv: (jnp.minimum(i + 1, n_blk - 1), 0, 0), memory_space=pltpu.SMEM)
    return _pcall(
        body, name="moe_up", grid=(n_blk, f // tf), prefetch=2,
        in_specs=[idx_spec, nxt_spec, pl.BlockSpec(memory_space=pl.ANY),
                  pl.BlockSpec((1, d, 2 * tf), lambda i, j, be, nv: (be[i], 0, j)),
                  pl.BlockSpec((1, 1, 2 * tf), lambda i, j, be, nv: (be[i], 0, j)),
                  pl.BlockSpec((2 * tf, tf), lambda i, j, be, nv: (0, 0))],
        out_specs=pl.BlockSpec((tm, tf), lambda i, j, be, nv: (i, j)),
        out_shape=jax.ShapeDtypeStruct((n_rows, f), BF16),
        scratch=[pltpu.VMEM((2, tm * chunks, LANES), F32), pltpu.VMEM((tm, d), BF16), pltpu.SemaphoreType.DMA((2,))],
    )(blk_e, n_valid, idx3, idx3, x_chunks, w_up, b_up, pick_even)


def _moe_down(blk_e, fresh, n_valid, a, w_down, b_down, *, n_rows, d, f, tm):
    chunks = d // LANES

    def body(be_ref, fr_ref, nv_ref, a_ref, w_ref, b_ref, o_ref, wb_ref):
        i = pl.program_id(0)

        @pl.when(fr_ref[i] == 1)
        def _():
            wb_ref[...] = w_ref[0].astype(BF16)

        @pl.when(i < nv_ref[0])
        def _():
            res = _dot(a_ref[...], wb_ref[...]) + b_ref[0]
            for c in range(chunks):
                o_ref[pl.ds(c, tm, stride=chunks), :] = res[:, c * LANES:(c + 1) * LANES]

        @pl.when(i >= nv_ref[0])
        def _():
            o_ref[...] = jnp.zeros(o_ref.shape, F32)

    return _pcall(
        body, name="moe_down", grid=(n_rows // tm,), prefetch=3,
        in_specs=[pl.BlockSpec((tm, f), lambda i, be, fr, nv: (i, 0)),
                  pl.BlockSpec((1, f, d), lambda i, be, fr, nv: (be[i], 0, 0)),
                  pl.BlockSpec((1, 1, d), lambda i, be, fr, nv: (be[i], 0, 0))],
        out_specs=pl.BlockSpec((tm * chunks, LANES), lambda i, be, fr, nv: (i, 0)),
        out_shape=jax.ShapeDtypeStruct((n_rows * chunks, LANES), F32),
        scratch=[pltpu.VMEM((f, d), BF16)],
    )(blk_e, fresh, n_valid, a, w_down, b_down)


def _moe_combine(pos3, ys, gates, x, ln_g, ln_b, *, n, d, tq, alpha):
    chunks = d // LANES
    n_tiles = n // tq

    def body(pos_ref, nxt_ref, y_hbm, g_ref, x_ref, lg_ref, lb_ref, of_ref, ob_ref, buf, y_ref, sem):
        i = pl.program_id(0)
        slot = i % 2

        def start_rows(src_pos, s):
            def issue(r, c):
                for kk in range(TOP_K):
                    _row_copy(y_hbm, src_pos[0, 0, kk * tq + r], buf.at[s, kk], r, sem.at[s], chunks).start(priority=kk % 2)
                return c
            lax.fori_loop(0, tq, issue, 0)

        @pl.when(i == 0)
        def _():
            start_rows(pos_ref, 0)

        def wait(r, c):
            for kk in range(TOP_K):
                _row_copy(y_hbm, 0, buf.at[slot, kk], r, sem.at[slot], chunks).wait()
            return c
        lax.fori_loop(0, tq, wait, 0)

        @pl.when(i + 1 < n_tiles)
        def _():
            start_rows(nxt_ref, 1 - slot)

        gv = g_ref[...]
        for c in range(chunks):
            piece = gv[:, 0:1] * buf[slot, 0, pl.ds(c, tq, stride=chunks), :]
            for kk in range(1, TOP_K):
                piece = piece + gv[:, kk:kk + 1] * buf[slot, kk, pl.ds(c, tq, stride=chunks), :]
            y_ref[:, c * LANES:(c + 1) * LANES] = piece
        out = _layer_norm(alpha * x_ref[...] + y_ref[...], lg_ref[...], lb_ref[...])
        of_ref[...] = out
        ob_ref[...] = out.astype(BF16)

    return _pcall(
        body, name="moe_combine_ln", grid=(n_tiles,),
        in_specs=[pl.BlockSpec((1, 1, TOP_K * tq), lambda i: (i, 0, 0), memory_space=pltpu.SMEM),
                  pl.BlockSpec((1, 1, TOP_K * tq), lambda i: (jnp.minimum(i + 1, n_tiles - 1), 0, 0), memory_space=pltpu.SMEM),
                  pl.BlockSpec(memory_space=pl.ANY),
                  pl.BlockSpec((tq, LANES), lambda i: (i, 0)),
                  pl.BlockSpec((tq, d), lambda i: (i, 0)),
                  pl.BlockSpec((1, d), lambda i: (0, 0)),
                  pl.BlockSpec((1, d), lambda i: (0, 0))],
        out_specs=[pl.BlockSpec((tq, d), lambda i: (i, 0)), pl.BlockSpec((tq, d), lambda i: (i, 0))],
        out_shape=[jax.ShapeDtypeStruct((n, d), F32), jax.ShapeDtypeStruct((n, d), BF16)],
        scratch=[pltpu.VMEM((2, TOP_K, tq * chunks, LANES), F32), pltpu.VMEM((tq, d), F32), pltpu.SemaphoreType.DMA((2,))],
    )(pos3, pos3, ys, gates, x, ln_g, ln_b)


def _t5_bucket(dist):
    max_exact = REL_BUCKETS // 2
    dd = jnp.maximum(dist, 1).astype(F32)
    large = max_exact + (jnp.log(dd / max_exact) / math.log(REL_MAX_DIST / max_exact)
                         * (REL_BUCKETS - max_exact)).astype(I32)
    large = jnp.minimum(large, REL_BUCKETS - 1)
    return jnp.where(dist < max_exact, dist, large)


def _tables(positions, rel_bias, t):
    half = MLA_ROPE // 2
    inv = 1.0 / (ROPE_THETA ** (jnp.arange(0, MLA_ROPE, 2, dtype=F32) / MLA_ROPE))
    ang = positions.astype(F32)[:, None] * inv[None, :]
    zeros = jnp.zeros((t, LANES - MLA_ROPE), F32)
    cos = jnp.concatenate([jnp.cos(ang), jnp.cos(ang), zeros], axis=-1)
    sin = jnp.concatenate([jnp.sin(ang), jnp.sin(ang), zeros], axis=-1)
    assert half * 2 == MLA_ROPE

    first = jnp.searchsorted(_t5_bucket(jnp.arange(t)), jnp.arange(REL_BUCKETS), side="left").astype(I32)
    rb = rel_bias.astype(F32)

    def bias_of(dist, valid):
        out = jnp.broadcast_to(rb[0].reshape((NSA_HEADS,) + (1,) * dist.ndim), (NSA_HEADS,) + dist.shape)
        for bkt in range(1, REL_BUCKETS):
            out = jnp.where((dist >= first[bkt])[None], rb[bkt].reshape((NSA_HEADS,) + (1,) * dist.ndim), out)
        return jnp.where(valid[None], out, NEG_INF)

    tb = NSA_TILE
    sat = rb[REL_BUCKETS - 1]
    cvec = jnp.broadcast_to(sat.reshape(NSA_GROUPS, NSA_HPG, 1), (NSA_GROUPS, NSA_HPG, 4 * tb))
    span = WINDOW + tb
    wd = (jnp.arange(WINDOW // tb + 1, dtype=I32) * tb)[:, None, None] + jnp.arange(tb, dtype=I32)[None, :, None] \
        - jnp.arange(span, dtype=I32)[None, None, :]
    wtab = bias_of(wd, (wd >= 0) & (wd < WINDOW)).transpose(1, 0, 2, 3)
    stab = bias_of(wd, wd >= 0).transpose(1, 0, 2, 3)
    nb = t // CMP_STRIDE
    dist_c = jnp.arange(t, dtype=I32)[:, None] - (jnp.arange(nb, dtype=I32) * CMP_STRIDE + CMP_LEN - 1)[None, :]
    bias_c = bias_of(dist_c, dist_c >= 0)
    tq_mla = _pick_tile(t, 512)
    cm = jnp.arange(tq_mla, dtype=I32)
    causal = jnp.where(cm[None, :] <= cm[:, None], 0.0, NEG_INF).astype(F32)

    n_slc = t // SEL_BLOCK
    tok = np.arange(nb)[:, None] * CMP_STRIDE + np.arange(CMP_LEN)[None, :]
    cover = (tok[:, :, None] // SEL_BLOCK == np.arange(LANES)[None, None, :]).sum(1) / CMP_LEN
    cover[nb - 1] = 0.0
    cover[:, n_slc:] = 0.0
    expand_t = (np.arange(t)[:, None] // SEL_BLOCK == np.arange(LANES)[None, :])
    return dict(cos=cos, sin=sin, stab=stab, cvec=cvec, wtab=wtab, bias_c=bias_c, causal=causal,
                cover=jnp.asarray(cover, BF16), expand_t=jnp.asarray(expand_t, BF16))


def _pad_heads(w, heads, width):
    k = w.shape[0]
    w = w.reshape(k, heads, width)
    return jnp.pad(w, ((0, 0), (0, 0), (0, HEAD_PAD - width))).reshape(k, heads * HEAD_PAD)


def _pick_tile(total, want):
    tile = min(total, want)
    while total % tile:
        tile //= 2
    return tile


def _token_mixer(x, xb, tabs, p, *, b, t, d):
    n = b * t
    tm = _pick_tile(t, 512)
    tpb = t // tm
    offs = np.cumsum([0, MLA_Q_LORA, MLA_KV_LORA, MLA_ROPE, NSA_HEADS * NSA_DK,
                      NSA_GROUPS * NSA_DK, NSA_GROUPS * NSA_DV, NSA_GROUPS * NSA_DK, NSA_GROUPS * NSA_DV,
                      NSA_GROUPS * NSA_DK, NSA_GROUPS * NSA_DV, NSA_HEADS * NSA_BRANCHES, 2 * d])
    w_in = p["w_in"]
    seg = [w_in[:, offs[s]:offs[s + 1]] for s in range(12)]
    x_lhs = [(xb, _row_spec(tm, d))]
    mla_scale = (MLA_NOPE + MLA_ROPE) ** -0.5
    nsa_scale = NSA_DK ** -0.5

    def cast_epi(acc, er, orr):
        orr[0][...] = acc.astype(BF16)

    lat_w = MLA_Q_LORA + MLA_KV_LORA
    w_lat = jnp.concatenate([seg[0], seg[1]], axis=1).astype(BF16)
    tn_lat = _pick_tile(lat_w, 1024)
    (lat,) = _matmul("proj_latents", x_lhs, w_lat, [], [(jax.ShapeDtypeStruct((n, lat_w), BF16), _tile_spec(tm, tn_lat))],
                     _cast_prologue, cast_epi, m=n, k=d, n=lat_w, tm=tm, tn=tn_lat)

    w_small = jnp.concatenate([seg[2], jnp.zeros((d, LANES - MLA_ROPE), F32), seg[10],
                               jnp.zeros((d, LANES - NSA_HEADS * NSA_BRANCHES), F32)], axis=1).astype(BF16)

    def small_epi(acc, er, orr):
        orr[0][...] = _rope_slab(acc[:, :LANES], er[0][...], er[1][...]).astype(BF16)
        orr[1][...] = _sigmoid(acc[:, LANES:])

    cs_extras = [(tabs["cos_n"], _row_spec(tm, LANES)), (tabs["sin_n"], _row_spec(tm, LANES))]
    kr, nsa_gates = _matmul(
        "proj_rope_key_gates", x_lhs, w_small, cs_extras,
        [(jax.ShapeDtypeStruct((n, LANES), BF16), _row_spec(tm, LANES)),
         (jax.ShapeDtypeStruct((n, LANES), F32), _row_spec(tm, LANES))],
        _cast_prologue, small_epi, m=n, k=d, n=2 * LANES, tm=tm, tn=2 * LANES)

    hb = 2
    hq = 4
    w_qn = _pad_heads(seg[3], NSA_HEADS, NSA_DK).astype(BF16)
    (q_nsa,) = _matmul("proj_nsa_q", x_lhs, w_qn, [],
                       [(jax.ShapeDtypeStruct((b, NSA_HEADS, t, HEAD_PAD), BF16), _head_spec(tm, hq, HEAD_PAD, tpb))],
                       _cast_prologue, _heads_epilogue(hq, HEAD_PAD, nsa_scale),
                       m=n, k=d, n=NSA_HEADS * HEAD_PAD, tm=tm, tn=hq * HEAD_PAD)

    w_k6 = jnp.concatenate([_pad_heads(seg[s], NSA_GROUPS, NSA_DK) for s in (4, 6, 8)], axis=1).astype(BF16)
    (k6,) = _matmul("proj_nsa_k", x_lhs, w_k6, [],
                    [(jax.ShapeDtypeStruct((b, 3 * NSA_GROUPS, t, HEAD_PAD), BF16), _head_spec(tm, hb, HEAD_PAD, tpb))],
                    _cast_prologue, _heads_epilogue(hb, HEAD_PAD),
                    m=n, k=d, n=3 * NSA_GROUPS * HEAD_PAD, tm=tm, tn=hb * HEAD_PAD)

    (v_c,) = _matmul("proj_nsa_v_cmp", x_lhs, seg[5].astype(BF16), [],
                     [(jax.ShapeDtypeStruct((b, NSA_GROUPS, t, NSA_DV), BF16), _head_spec(tm, hb, NSA_DV, tpb))],
                     _cast_prologue, _heads_epilogue(hb, NSA_DV),
                     m=n, k=d, n=NSA_GROUPS * NSA_DV, tm=tm, tn=hb * NSA_DV)

    def ones_lane(rows):
        return jnp.where(lax.broadcasted_iota(I32, (rows, HEAD_PAD - NSA_DV), 1) == 0, 1.0, 0.0).astype(BF16)

    def v_ones_epi(acc, er, orr):
        for c in range(hb):
            orr[0][0, c, :, :NSA_DV] = acc[:, c * NSA_DV:(c + 1) * NSA_DV].astype(BF16)
            orr[0][0, c, :, NSA_DV:] = ones_lane(acc.shape[0])

    w_v4 = jnp.concatenate([seg[7], seg[9]], axis=1).astype(BF16)
    (v4,) = _matmul("proj_nsa_v", x_lhs, w_v4, [],
                    [(jax.ShapeDtypeStruct((b, 2 * NSA_GROUPS, t, HEAD_PAD), BF16), _head_spec(tm, hb, HEAD_PAD, tpb))],
                    _cast_prologue, v_ones_epi, m=n, k=d, n=2 * NSA_GROUPS * NSA_DV, tm=tm, tn=hb * NSA_DV)

    def sig_epi(acc, er, orr):
        orr[0][...] = _sigmoid(acc).astype(BF16)

    tn_d = _pick_tile(d, 1024)
    (merge,) = _matmul("proj_merge_gates", x_lhs, seg[11].astype(BF16), [],
                       [(jax.ShapeDtypeStruct((n, 2 * d), BF16), _tile_spec(tm, tn_d))],
                       _cast_prologue, sig_epi, m=n, k=d, n=2 * d, tm=tm, tn=tn_d)

    w_uq = p["mla_w_uq"].reshape(MLA_Q_LORA, MLA_HEADS, MLA_NOPE + MLA_ROPE)
    w_uq = jnp.pad(w_uq, ((0, 0), (0, 0), (0, HEAD_PAD - MLA_NOPE - MLA_ROPE))).reshape(MLA_Q_LORA, MLA_HEADS * HEAD_PAD)

    def q_epi(acc, er, orr):
        cos, sin = er[0][...], er[1][...]
        for c in range(hq):
            orr[0][0, c, :, :MLA_NOPE] = (acc[:, c * HEAD_PAD:c * HEAD_PAD + MLA_NOPE] * mla_scale).astype(BF16)
            slab = acc[:, c * HEAD_PAD + MLA_NOPE:(c + 1) * HEAD_PAD] * mla_scale
            orr[0][0, c, :, MLA_NOPE:] = _rope_slab(slab, cos, sin).astype(BF16)

    (q_mla,) = _matmul(
        "mla_q_up", [(lat, _row_spec(tm, MLA_Q_LORA)), (p["mla_q_norm"].reshape(1, -1), pl.BlockSpec((1, MLA_Q_LORA), lambda i, j: (0, 0)))],
        w_uq.astype(BF16), cs_extras,
        [(jax.ShapeDtypeStruct((b, MLA_HEADS, t, HEAD_PAD), BF16), _head_spec(tm, hq, HEAD_PAD, tpb))],
        _rms_prologue, q_epi, m=n, k=MLA_Q_LORA, n=MLA_HEADS * HEAD_PAD, tm=tm, tn=hq * HEAD_PAD)

    kv_w = MLA_NOPE + MLA_V

    def kv_epi(acc, er, orr):
        k_rope = er[0][...]
        for c in range(hq):
            orr[0][0, c, :, :MLA_NOPE] = acc[:, c * kv_w:c * kv_w + MLA_NOPE].astype(BF16)
            orr[0][0, c, :, MLA_NOPE:] = k_rope
            orr[1][0, c, :, :MLA_V] = acc[:, c * kv_w + MLA_NOPE:(c + 1) * kv_w].astype(BF16)
            orr[1][0, c, :, MLA_V:] = ones_lane(acc.shape[0])

    k_mla, v_mla = _matmul(
        "mla_kv_up", [(lat, _row_spec(tm, MLA_KV_LORA, MLA_Q_LORA // MLA_KV_LORA)),
                      (p["mla_kv_norm"].reshape(1, -1), pl.BlockSpec((1, MLA_KV_LORA), lambda i, j: (0, 0)))],
        p["mla_w_ukv"].astype(BF16), [(kr, _row_spec(tm, LANES))],
        [(jax.ShapeDtypeStruct((b, MLA_HEADS, t, HEAD_PAD), BF16), _head_spec(tm, hq, HEAD_PAD, tpb)),
         (jax.ShapeDtypeStruct((b, MLA_HEADS, t, HEAD_PAD), BF16), _head_spec(tm, hq, HEAD_PAD, tpb))],
        _rms_prologue, kv_epi, m=n, k=MLA_KV_LORA, n=MLA_HEADS * kv_w, tm=tm, tn=hq * kv_w)

    attn_a = _mla_attention(q_mla, k_mla, v_mla, tabs["causal"], b=b, t=t, tq=tabs["causal"].shape[0])

    def ya_epi(acc, er, orr):
        orr[0][...] = (er[0][...].astype(F32) * acc).astype(BF16)

    (y_a,) = _matmul("mla_branch_out", [(attn_a.reshape(n, MLA_HEADS * MLA_V), _row_spec(tm, MLA_HEADS * MLA_V))],
                     p["w_branch_a"].astype(BF16), [(merge, _tile_spec(tm, tn_d))],
                     [(jax.ShapeDtypeStruct((n, d), BF16), _tile_spec(tm, tn_d))],
                     _cast_prologue, ya_epi, m=n, k=MLA_HEADS * MLA_V, n=d, tm=tm, tn=tn_d)

    def cmp_weights(pe, w1, w2, width_in, width):
        pe_p = jnp.pad(pe, ((0, 0), (0, width - width_in)))
        pe2 = jnp.zeros((8, CMP_STRIDE * width), F32).at[0].set(pe_p[:CMP_STRIDE].reshape(-1)).at[1].set(pe_p[CMP_STRIDE:].reshape(-1))
        w1p = jnp.pad(w1.reshape(CMP_LEN, width_in, width_in), ((0, 0), (0, width - width_in), (0, width - width_in)))
        w2p = jnp.pad(w2, ((0, width - width_in), (0, width - width_in)))
        return (pe2.astype(BF16), w1p[:CMP_STRIDE].reshape(-1, width).astype(BF16),
                w1p[CMP_STRIDE:].reshape(-1, width).astype(BF16), w2p.astype(BF16))

    kcmp = _compress(k6, *cmp_weights(p["cmp_pe_k"], p["cmp_w1_k"], p["cmp_w2_k"], NSA_DK, HEAD_PAD), b=b, t=t, width=HEAD_PAD)
    vcmp = _compress(v_c, *cmp_weights(p["cmp_pe_v"], p["cmp_w1_v"], p["cmp_w2_v"], NSA_DV, NSA_DV), b=b, t=t, width=NSA_DV)
    gates3 = nsa_gates.reshape(b, t, LANES)
    o_cmp, sel = _nsa_cmp(q_nsa, kcmp, vcmp, tabs["bias_c"], tabs["cover"], gates3, b=b, t=t)
    o_slc = _nsa_sel(q_nsa, k6, v4, tabs["stab"], tabs["cvec"], gates3, sel, tabs["expand_t"], b=b, t=t)
    o_win = _nsa_win(q_nsa, k6, v4, tabs["wtab"], gates3, b=b, t=t)

    def sum3_prologue(a_ref, b_ref, c_ref):
        return (a_ref[...].astype(F32) + b_ref[...].astype(F32) + c_ref[...].astype(F32)).astype(BF16)

    def merge_epi(acc, er, orr):
        orr[0][...] = (er[0][...].astype(F32) + er[1][...].astype(F32) * acc).astype(BF16)

    hv = NSA_HEADS * NSA_DV
    (merged,) = _matmul("nsa_branch_out_merge", [(o.reshape(n, hv), _row_spec(tm, hv)) for o in (o_cmp, o_slc, o_win)],
                        p["w_branch_b"].astype(BF16), [(y_a, _tile_spec(tm, tn_d)), (merge, _tile_spec(tm, tn_d, d // tn_d))],
                        [(jax.ShapeDtypeStruct((n, d), BF16), _tile_spec(tm, tn_d))],
                        sum3_prologue, merge_epi, m=n, k=hv, n=d, tm=tm, tn=tn_d)

    tm_ln = _pick_tile(t, 256)
    alpha = p["alpha"]

    chunks = d // LANES

    def ln_epi(acc, er, orr):
        out = _layer_norm(alpha * er[0][...] + acc, er[1][...], er[2][...])
        orr[0][...] = out
        for c in range(chunks):
            orr[1][pl.ds(c, tm_ln, stride=chunks), :] = out[:, c * LANES:(c + 1) * LANES]

    vec_spec = pl.BlockSpec((1, d), lambda i, j: (0, 0))
    x1, x1_chunks = _matmul(
        "mixer_out_ln", [(merged, _row_spec(tm_ln, d))], p["w_out"].astype(BF16),
        [(x, _row_spec(tm_ln, d)), (p["ln1_g"].reshape(1, d), vec_spec), (p["ln1_b"].reshape(1, d), vec_spec)],
        [(jax.ShapeDtypeStruct((n, d), F32), _row_spec(tm_ln, d)),
         (jax.ShapeDtypeStruct((n * chunks, LANES), F32), pl.BlockSpec((tm_ln * chunks, LANES), lambda i, j: (i, 0)))],
        _cast_prologue, ln_epi, m=n, k=d, n=d, tm=tm_ln, tn=d)
    return x1, x1_chunks


def _moe(x, x_chunks, p, *, n, d):
    n_exp, _, f2 = p["exp_w_up"].shape
    f = f2 // 2
    tm_e = p["tm_e"]
    rw = jnp.pad(p["router_w"], ((0, 0), (0, LANES - n_exp)))
    rw_hi = rw.astype(BF16)
    rw_lo = (rw - rw_hi.astype(F32)).astype(BF16)
    rb = jnp.pad(p["router_b"], (0, LANES - n_exp)).reshape(1, LANES)
    top_e, gates = _router(x, rw_hi, rw_lo, rb, n=n, d=d, n_exp=n_exp, tm=_pick_tile(n, 512))

    n_pairs = n * TOP_K
    e_flat = top_e[:, :TOP_K].reshape(-1)
    grp = LANES
    onehot = (e_flat[:, None] == jnp.arange(n_exp, dtype=I32)[None, :]).astype(F32).reshape(n_pairs // grp, grp, n_exp)
    tri = jnp.tril(jnp.ones((grp, grp), F32))
    within = jnp.einsum("ij,gje->gie", tri, onehot)
    totals = within[:, -1, :]
    before = jnp.cumsum(totals, axis=0) - totals
    rank = (jnp.sum((within + before[:, None, :]) * onehot, axis=-1).reshape(-1) - 1.0).astype(I32)
    counts = jnp.sum(totals, axis=0).astype(I32)
    padded = (counts + tm_e - 1) // tm_e * tm_e
    pad_end = jnp.cumsum(padded)
    dest = (pad_end - padded)[e_flat] + rank
    n_rows = n_pairs + n_exp * tm_e
    n_blk = n_rows // tm_e
    row_tok = jnp.zeros((n_rows,), I32).at[dest].set(jnp.arange(n_pairs, dtype=I32) // TOP_K)
    n_valid = (pad_end[-1] // tm_e).astype(I32)
    blk_e = jnp.searchsorted(pad_end, jnp.arange(n_blk, dtype=I32) * tm_e, side="right").astype(I32)
    blk_e = jnp.minimum(blk_e, blk_e[jnp.maximum(n_valid - 1, 0)])
    n_valid = n_valid.reshape(1)

    tf = _pick_tile(f, 512)
    pick_even = jnp.asarray(np.arange(2 * tf)[:, None] == 2 * np.arange(tf)[None, :], BF16)
    a = _moe_up(blk_e, n_valid, x_chunks, row_tok.reshape(n_blk, 1, tm_e),
                p["exp_w_up"].astype(BF16), p["exp_b_up"].reshape(n_exp, 1, 2 * f), pick_even,
                n_rows=n_rows, d=d, f=f, tm=tm_e, tf=tf)
    split = max(1, tm_e // 256)
    tm_d = tm_e // split
    blk_d = jnp.repeat(blk_e, split)
    fresh = jnp.concatenate([jnp.ones((1,), I32), (blk_d[1:] != blk_d[:-1]).astype(I32)])
    ys = _moe_down(blk_d, fresh, n_valid * split, a, p["exp_w_down"], p["exp_b_down"].reshape(n_exp, 1, d),
                   n_rows=n_rows, d=d, f=f, tm=tm_d)

    tq = _pick_tile(n, 128)
    pos3 = dest.reshape(n // tq, tq, TOP_K).transpose(0, 2, 1).reshape(n // tq, 1, TOP_K * tq)
    return _moe_combine(pos3, ys, gates, x, p["ln2_g"].reshape(1, d), p["ln2_b"].reshape(1, d),
                        n=n, d=d, tq=tq, alpha=p["alpha"])


def kernel(x, positions, rel_bias, w_in, mla_q_norm, mla_w_uq, mla_kv_norm, mla_w_ukv, cmp_pe_k, cmp_w1_k, cmp_w2_k, cmp_pe_v, cmp_w1_v, cmp_w2_v, w_branch_a, w_branch_b, w_out, ln1_g, ln1_b, router_w, router_b, exp_w_up, exp_b_up, exp_w_down, exp_b_down, ln2_g, ln2_b):
    b, t, d = x.shape
    depth = w_in.shape[0]
    n = b * t
    alpha = (2 * depth) ** 0.25
    tabs = _tables(positions, rel_bias, t)
    tabs["cos_n"] = jnp.tile(tabs["cos"], (b, 1))
    tabs["sin_n"] = jnp.tile(tabs["sin"], (b, 1))
    xf = x.reshape(n, d)
    xb = xf.astype(BF16)
    for l in range(depth):
        p = dict(w_in=w_in[l], mla_q_norm=mla_q_norm[l], mla_w_uq=mla_w_uq[l], mla_kv_norm=mla_kv_norm[l],
                 mla_w_ukv=mla_w_ukv[l], cmp_pe_k=cmp_pe_k[l], cmp_w1_k=cmp_w1_k[l], cmp_w2_k=cmp_w2_k[l],
                 cmp_pe_v=cmp_pe_v[l], cmp_w1_v=cmp_w1_v[l], cmp_w2_v=cmp_w2_v[l], w_branch_a=w_branch_a[l],
                 w_branch_b=w_branch_b[l], w_out=w_out[l], ln1_g=ln1_g[l], ln1_b=ln1_b[l], router_w=router_w[l],
                 router_b=router_b[l], exp_w_up=exp_w_up[l], exp_b_up=exp_b_up[l], exp_w_down=exp_w_down[l],
                 exp_b_down=exp_b_down[l], ln2_g=ln2_g[l], ln2_b=ln2_b[l], alpha=alpha, tm_e=_pick_tile(n * TOP_K, 512))
        x1, x1_chunks = _token_mixer(xf, xb, tabs, p, b=b, t=t, d=d)
        xf, xb = _moe(x1, x1_chunks, p, n=n, d=d)
    return xf.reshape(b, t, d)
```

```python
import functools
import math

import numpy as np
import jax
import jax.numpy as jnp
from jax import lax
from jax.experimental import pallas as pl
from jax.experimental.pallas import tpu as pltpu

F32, BF16, I32 = jnp.float32, jnp.bfloat16, jnp.int32

MLA_HEADS = 16
MLA_Q_LORA = 1536
MLA_KV_LORA = 512
MLA_NOPE = 128
MLA_ROPE = 64
MLA_V = 128
ROPE_THETA = 10000.0
NSA_HEADS = 16
NSA_GROUPS = 2
NSA_HPG = NSA_HEADS // NSA_GROUPS
NSA_DK = 192
NSA_DV = 128
CMP_LEN = 32
CMP_STRIDE = 16
SEL_BLOCK = 64
SEL_COUNT = 16
WINDOW = 512
NSA_BRANCHES = 3
SEL_FORCE = 1.0e4
REL_BUCKETS = 32
REL_MAX_DIST = 128
TOP_K = 4
SWIGLU_LIMIT = 7.0
SWIGLU_ALPHA = 1.702
NEG_INF = -1.0e30
LN_EPS = 1e-5
RMS_EPS = 1e-6

LANES = 128
HEAD_PAD = 256
NSA_TILE = 128
GATHER_UNROLL = 8
VMEM_LIMIT_BYTES = 56 * 1024 * 1024


def _pcall(body, *, name, grid, in_specs, out_specs, out_shape, scratch=(), prefetch=0):
    gs = pltpu.PrefetchScalarGridSpec(num_scalar_prefetch=prefetch, grid=grid, in_specs=in_specs,
                                      out_specs=out_specs, scratch_shapes=list(scratch))
    return pl.pallas_call(
        body, grid_spec=gs, out_shape=out_shape, name=name,
        compiler_params=pltpu.CompilerParams(dimension_semantics=("arbitrary",) * len(grid),
                                             vmem_limit_bytes=VMEM_LIMIT_BYTES))


def _sigmoid(x):
    return 1.0 / (1.0 + jnp.exp(-x))


def _dot_t(a, b):
    return lax.dot_general(a, b, (((1,), (1,)), ((), ())), preferred_element_type=F32)


def _dot(a, b):
    return jnp.dot(a, b, preferred_element_type=F32)


def _layer_norm(r, g, b):
    mu = jnp.mean(r, axis=-1, keepdims=True)
    c = r - mu
    var = jnp.mean(c * c, axis=-1, keepdims=True)
    return c * lax.rsqrt(var + LN_EPS) * g + b


def _matmul(name, lhs, w, extras, outs, prologue, epilogue, *, m, k, n, tm, tn):
    nl, ne, no = len(lhs), len(extras), len(outs)

    def body(*refs):
        lr = refs[:nl]
        wr = refs[nl]
        er = refs[nl + 1:nl + 1 + ne]
        orr = refs[nl + 1 + ne:nl + 1 + ne + no]
        xs = refs[-1]

        @pl.when(pl.program_id(1) == 0)
        def _():
            xs[...] = prologue(*lr)

        epilogue(_dot(xs[...], wr[...]), er, orr)

    return _pcall(
        body, name=name, grid=(m // tm, n // tn),
        in_specs=[s for _, s in lhs] + [pl.BlockSpec((k, tn), lambda i, j: (0, j))] + [s for _, s in extras],
        out_specs=[s for _, s in outs], out_shape=[o for o, _ in outs],
        scratch=[pltpu.VMEM((tm, k), BF16)],
    )(*[a for a, _ in lhs], w, *[a for a, _ in extras])


def _row_spec(tm, width, col=0):
    return pl.BlockSpec((tm, width), lambda i, j: (i, col))


def _tile_spec(tm, tn, col_off=0):
    return pl.BlockSpec((tm, tn), lambda i, j: (i, j + col_off))


def _head_spec(tm, hb, width, tpb):
    return pl.BlockSpec((1, hb, tm, width), lambda i, j: (i // tpb, j, i % tpb, 0))


def _cast_prologue(x_ref):
    return x_ref[...].astype(BF16)


def _rms_prologue(x_ref, g_ref):
    xf = x_ref[...].astype(F32)
    ms = jnp.mean(xf * xf, axis=-1, keepdims=True)
    return (xf * lax.rsqrt(ms + RMS_EPS) * g_ref[...]).astype(BF16)


def _rope_slab(t, cos, sin):
    lane = lax.broadcasted_iota(I32, t.shape, 1)
    rot = jnp.where(lane < MLA_ROPE // 2, -pltpu.roll(t, LANES - MLA_ROPE // 2, 1), pltpu.roll(t, MLA_ROPE // 2, 1))
    return t * cos + rot * sin


def _heads_epilogue(hb, width, scale=None):
    def epi(acc, er, orr):
        for c in range(hb):
            v = acc[:, c * width:(c + 1) * width]
            if scale is not None:
                v = v * scale
            orr[0][0, c] = v.astype(BF16)
    return epi


def _softmax_init(rows, width):
    return (jnp.full((rows, 1), NEG_INF, F32), jnp.zeros((rows, 1), F32), jnp.zeros((rows, width), F32))


def _softmax_step(s, v, carry):
    m, l, acc = carry
    m_new = jnp.maximum(m, jnp.max(s, axis=-1, keepdims=True))
    a = jnp.exp(m - m_new)
    p = jnp.exp(s - m_new)
    return m_new, a * l + jnp.sum(p, axis=-1, keepdims=True), a * acc + _dot(p.astype(BF16), v)


def _mla_attention(q, k, v_ones, causal, *, b, t, tq):
    nq = t // tq

    def body(q_ref, k_ref, v_ref, c_ref, o_ref):
        i = pl.program_id(2)
        qv = q_ref[0, 0]

        def step(koff, carry, diagonal):
            s = _dot_t(qv, k_ref[0, 0, pl.ds(koff, tq), :])
            if diagonal:
                s = s + c_ref[...]
            return _softmax_step(s, v_ref[0, 0, pl.ds(koff, tq), :], carry)

        def pair(j, c):
            c = step(pl.multiple_of(2 * j * tq, tq), c, False)
            return step(pl.multiple_of((2 * j + 1) * tq, tq), c, False)
        carry = lax.fori_loop(0, i // 2, pair, _softmax_init(tq, MLA_V))
        carry = lax.fori_loop(0, i % 2, lambda j, c: step(pl.multiple_of((i - 1) * tq, tq), c, False), carry)
        _, l, acc = step(pl.multiple_of(i * tq, tq), carry, True)
        o_ref[0] = (acc / l).astype(BF16)

    return _pcall(
        body, name="mla_attention", grid=(b, MLA_HEADS, nq),
        in_specs=[pl.BlockSpec((1, 1, tq, HEAD_PAD), lambda bi, h, i: (bi, h, i, 0)),
                  pl.BlockSpec((1, 1, t, HEAD_PAD), lambda bi, h, i: (bi, h, 0, 0)),
                  pl.BlockSpec((1, 1, t, MLA_V), lambda bi, h, i: (bi, h, 0, 0)),
                  pl.BlockSpec((tq, tq), lambda bi, h, i: (0, 0))],
        out_specs=pl.BlockSpec((1, tq, MLA_V), lambda bi, h, i: (bi, i, h)),
        out_shape=jax.ShapeDtypeStruct((b, t, MLA_HEADS * MLA_V), BF16),
    )(q, k, v_ones, causal)


def _gelu_tanh(x):
    return 0.5 * x * (1.0 + jnp.tanh(math.sqrt(2.0 / math.pi) * (x + 0.044715 * (x * x * x))))


def _compress(kv6, pe2, w1lo, w1hi, w2, *, b, t, width):
    nb = t // CMP_STRIDE
    half = CMP_STRIDE * width
    h = kv6.reshape(b, kv6.shape[1], nb, half)

    def body(h_ref, pe_ref, lo_ref, hi_ref, w2_ref, o_ref):
        hv = h_ref[0, 0]
        lo, hi = lo_ref[...], hi_ref[...]
        a = _dot(hv, lo)
        bb = _dot(hv, hi)
        pe = pe_ref[...]
        c = _dot(pe, lo)[0:1] + _dot(pe, hi)[1:2]
        z = a + pltpu.roll(bb, nb - 1, 0) + c
        o_ref[0, 0] = _dot(_gelu_tanh(z).astype(BF16), w2_ref[...]).astype(BF16)

    return _pcall(
        body, name=f"nsa_compress_{width}", grid=(b, NSA_GROUPS),
        in_specs=[
            pl.BlockSpec((1, 1, nb, half), lambda bi, g: (bi, g, 0, 0)),
            pl.BlockSpec((8, half), lambda bi, g: (0, 0)),
            pl.BlockSpec((half, width), lambda bi, g: (0, 0)),
            pl.BlockSpec((half, width), lambda bi, g: (0, 0)),
            pl.BlockSpec((width, width), lambda bi, g: (0, 0)),
        ],
        out_specs=pl.BlockSpec((1, 1, nb, width), lambda bi, g: (bi, g, 0, 0)),
        out_shape=jax.ShapeDtypeStruct((b, NSA_GROUPS, nb, width), BF16),
    )(h, pe2, w1lo, w1hi, w2)


def _gate_col(gates, col):
    lane = lax.broadcasted_iota(I32, gates.shape, 1)
    return jnp.sum(jnp.where(lane == col, gates, 0.0), axis=-1, keepdims=True)


def _nsa_cmp(q, kcmp, vcmp, bias_c, cover, gates, *, b, t):
    tq = NSA_TILE
    nb = t // CMP_STRIDE
    n_slc = t // SEL_BLOCK
    n_sel = min(SEL_COUNT, n_slc)
    hpg = NSA_HPG

    def body(q_ref, k_ref, v_ref, bias_ref, cov_ref, g_ref, o_ref, sel_ref):
        g = pl.program_id(1)
        i = pl.program_id(2)
        q2 = q_ref[0].reshape(hpg * tq, HEAD_PAD)
        s3 = _dot_t(q2, k_ref[0, 0]).reshape(hpg, tq, nb) + bias_ref[...]
        p = jnp.exp(s3 - jnp.max(s3, axis=-1, keepdims=True))
        has_key = jnp.where(i * tq + lax.broadcasted_iota(I32, (tq, 1), 0) >= CMP_LEN - 1, 1.0, 0.0)
        p = p * (has_key / jnp.maximum(jnp.sum(p, axis=-1, keepdims=True), 1e-30))
        o3 = _dot(p.reshape(hpg * tq, nb).astype(BF16), v_ref[0, 0]).reshape(hpg, tq, NSA_DV)
        _write_gated_heads(o_ref, o3, g_ref[0], g, 0)

        psum = jnp.sum(p, axis=0)
        p_hi = psum.astype(BF16)
        p_lo = (psum - p_hi.astype(F32)).astype(BF16)
        cov = cov_ref[...]
        imp = _dot(p_hi, cov) + _dot(p_lo, cov)

        jl = lax.broadcasted_iota(I32, (tq, LANES), 1)
        cur = (i * tq + lax.broadcasted_iota(I32, (tq, LANES), 0)) // SEL_BLOCK
        forced = (jl == 0) | (jl == cur) | (jl == cur - 1)
        score = jnp.where(forced, SEL_FORCE, jnp.where(jl <= cur, imp, -1.0))
        n_used = -(-n_slc // 8) * 8
        st = jnp.transpose(score)[:n_used]
        jrow = lax.broadcasted_iota(I32, (n_used, tq), 0)
        rank = jnp.zeros((n_used, tq), F32)
        for kk in range(n_slc):
            row = st[kk:kk + 1, :]
            rank = rank + jnp.where(jrow > kk, jnp.where(row >= st, 1.0, 0.0), jnp.where(row > st, 1.0, 0.0))
        chosen_t = jnp.where((rank < n_sel) & (st >= 0.0), 1.0, 0.0)
        if n_used < LANES:
            chosen_t = jnp.concatenate([chosen_t, jnp.zeros((LANES - n_used, tq), F32)], axis=0)
        sel_ref[0, 0] = jnp.transpose(chosen_t).astype(BF16)

    return _pcall(
        body, name="nsa_cmp_select", grid=(b, NSA_GROUPS, t // tq),
        in_specs=[
            pl.BlockSpec((1, hpg, tq, HEAD_PAD), lambda bi, g, i: (bi, g, i, 0)),
            pl.BlockSpec((1, 1, nb, HEAD_PAD), lambda bi, g, i: (bi, g, 0, 0)),
            pl.BlockSpec((1, 1, nb, NSA_DV), lambda bi, g, i: (bi, g, 0, 0)),
            pl.BlockSpec((hpg, tq, nb), lambda bi, g, i: (g, i, 0)),
            pl.BlockSpec((nb, LANES), lambda bi, g, i: (0, 0)),
            pl.BlockSpec((1, tq, LANES), lambda bi, g, i: (bi, i, 0)),
        ],
        out_specs=[
            pl.BlockSpec((1, tq, hpg * NSA_DV), lambda bi, g, i: (bi, i, g)),
            pl.BlockSpec((1, 1, tq, LANES), lambda bi, g, i: (bi, g, i, 0)),
        ],
        out_shape=[jax.ShapeDtypeStruct((b, t, NSA_HEADS * NSA_DV), BF16),
                   jax.ShapeDtypeStruct((b, NSA_GROUPS, t, LANES), BF16)],
    )(q, kcmp, vcmp, bias_c, cover, gates)


def _write_gated_heads(o_ref, o3, gates_v, g, branch):
    for hh in range(NSA_HPG):
        gc = _gate_col(gates_v, (g * NSA_HPG + hh) * NSA_BRANCHES + branch)
        o_ref[0, :, hh * NSA_DV:(hh + 1) * NSA_DV] = (o3[hh] * gc).astype(BF16)


def _nsa_sel(q, k6, v4, stab, cvec, gates, sel, expand_t, *, b, t):
    tb = NSA_TILE
    hpg = NSA_HPG
    wide = 4 * tb
    span = WINDOW + tb
    back = WINDOW // tb
    rows = hpg * tb

    def body(q_ref, k_ref, v_ref, tab_ref, cv_ref, g_ref, sel_ref, et_ref, o_ref):
        g = pl.program_id(1)
        i = pl.program_id(2)
        q2 = q_ref[0].reshape(rows, HEAD_PAD)
        sel_v = sel_ref[0, 0]
        far_end = jnp.maximum(i - back, 0) * tb

        def far(j, carry):
            koff = pl.multiple_of(j * wide, wide)
            s3 = _dot_t(q2, k_ref[0, 0, pl.ds(koff, wide), :]).reshape(hpg, tb, wide) + cv_ref[0][:, None, :]
            key = koff + lax.broadcasted_iota(I32, (tb, wide), 1)
            chosen = jnp.where(key < far_end, _dot_t(sel_v, et_ref[pl.ds(koff, wide), :]), 0.0) > 0.5
            s3 = jnp.where(chosen[None], s3, NEG_INF)
            return _softmax_step(s3.reshape(rows, wide), v_ref[0, 0, pl.ds(koff, wide), :], carry)

        n_far = (far_end + wide - 1) // wide
        carry = lax.fori_loop(0, n_far // 2, lambda j, c: far(2 * j + 1, far(2 * j, c)), _softmax_init(rows, NSA_DV))
        carry = lax.fori_loop(0, n_far % 2, lambda j, c: far(n_far - 1, c), carry)

        koff = pl.multiple_of(far_end, tb)
        s3 = _dot_t(q2, k_ref[0, 0, pl.ds(koff, span), :]).reshape(hpg, tb, span) + tab_ref[0]
        chosen = _dot_t(sel_v, et_ref[pl.ds(koff, span), :]) > 0.5
        s3 = jnp.where(chosen[None], s3, NEG_INF)
        _, l, acc = _softmax_step(s3.reshape(rows, span), v_ref[0, 0, pl.ds(koff, span), :], carry)
        _write_gated_heads(o_ref, (acc / l).reshape(hpg, tb, NSA_DV), g_ref[0], g, 1)

    return _pcall(
        body, name="nsa_sel_attention", grid=(b, NSA_GROUPS, t // tb),
        in_specs=[
            pl.BlockSpec((1, hpg, tb, HEAD_PAD), lambda bi, g, i: (bi, g, i, 0)),
            pl.BlockSpec((1, 1, t, HEAD_PAD), lambda bi, g, i: (bi, NSA_GROUPS + g, 0, 0)),
            pl.BlockSpec((1, 1, t, NSA_DV), lambda bi, g, i: (bi, g, 0, 0)),
            pl.BlockSpec((1, hpg, tb, span), lambda bi, g, i: (jnp.minimum(i, back), g, 0, 0)),
            pl.BlockSpec((1, hpg, wide), lambda bi, g, i: (g, 0, 0)),
            pl.BlockSpec((1, tb, LANES), lambda bi, g, i: (bi, i, 0)),
            pl.BlockSpec((1, 1, tb, LANES), lambda bi, g, i: (bi, g, i, 0)),
            pl.BlockSpec((t, LANES), lambda bi, g, i: (0, 0)),
        ],
        out_specs=pl.BlockSpec((1, tb, hpg * NSA_DV), lambda bi, g, i: (bi, i, g)),
        out_shape=jax.ShapeDtypeStruct((b, t, NSA_HEADS * NSA_DV), BF16),
    )(q, k6, v4, stab, cvec, gates, sel, expand_t)


def _nsa_win(q, k6, v4, wtab, gates, *, b, t):
    tb = NSA_TILE
    hpg = NSA_HPG
    span = WINDOW + tb
    back = WINDOW // tb
    rows = hpg * tb

    def body(q_ref, k_ref, v_ref, w_ref, g_ref, o_ref):
        g = pl.program_id(1)
        i = pl.program_id(2)
        koff = pl.multiple_of(jnp.maximum(i - back, 0) * tb, tb)
        q2 = q_ref[0].reshape(rows, HEAD_PAD)
        s3 = _dot_t(q2, k_ref[0, 0, pl.ds(koff, span), :]).reshape(hpg, tb, span) + w_ref[0]
        s = s3.reshape(rows, span)
        p = jnp.exp(s - jnp.max(s, axis=-1, keepdims=True))
        acc = _dot(p.astype(BF16), v_ref[0, 0, pl.ds(koff, span), :])
        o3 = (acc[:, :NSA_DV] / acc[:, NSA_DV:NSA_DV + 1]).reshape(hpg, tb, NSA_DV)
        _write_gated_heads(o_ref, o3, g_ref[0], g, 2)

    return _pcall(
        body, name="nsa_win_attention", grid=(b, NSA_GROUPS, t // tb),
        in_specs=[
            pl.BlockSpec((1, hpg, tb, HEAD_PAD), lambda bi, g, i: (bi, g, i, 0)),
            pl.BlockSpec((1, 1, t, HEAD_PAD), lambda bi, g, i: (bi, 2 * NSA_GROUPS + g, 0, 0)),
            pl.BlockSpec((1, 1, t, HEAD_PAD), lambda bi, g, i: (bi, NSA_GROUPS + g, 0, 0)),
            pl.BlockSpec((1, hpg, tb, span), lambda bi, g, i: (jnp.minimum(i, back), g, 0, 0)),
            pl.BlockSpec((1, tb, LANES), lambda bi, g, i: (bi, i, 0)),
        ],
        out_specs=pl.BlockSpec((1, tb, hpg * NSA_DV), lambda bi, g, i: (bi, i, g)),
        out_shape=jax.ShapeDtypeStruct((b, t, NSA_HEADS * NSA_DV), BF16),
    )(q, k6, v4, wtab, gates)


def _router(x, w_hi, w_lo, bias, *, n, d, n_exp, tm):
    def body(x_ref, wh_ref, wl_ref, b_ref, e_ref, gate_ref):
        xv = x_ref[...]
        x_hi = xv.astype(BF16)
        x_lo = (xv - x_hi.astype(F32)).astype(BF16)
        logits = _dot(x_hi, wh_ref[...]) + _dot(x_lo, wh_ref[...]) + _dot(x_hi, wl_ref[...]) + b_ref[...]
        lane = lax.broadcasted_iota(I32, (tm, LANES), 1)
        logits = jnp.where(lane < n_exp, logits, NEG_INF)
        e_out = jnp.zeros((tm, LANES), I32)
        v_out = jnp.full((tm, LANES), NEG_INF, F32)
        for kk in range(TOP_K):
            mx = jnp.max(logits, axis=-1, keepdims=True)
            idx = jnp.min(jnp.where(logits == mx, lane, LANES), axis=-1, keepdims=True)
            e_out = jnp.where(lane == kk, idx, e_out)
            v_out = jnp.where(lane == kk, mx, v_out)
            logits = jnp.where(lane == idx, NEG_INF, logits)
        ex = jnp.where(lane < TOP_K, jnp.exp(v_out - jnp.max(v_out, axis=-1, keepdims=True)), 0.0)
        e_ref[...] = e_out
        gate_ref[...] = ex / jnp.sum(ex, axis=-1, keepdims=True)

    return _pcall(
        body, name="moe_router", grid=(n // tm,),
        in_specs=[pl.BlockSpec((tm, d), lambda i: (i, 0)),
                  pl.BlockSpec((d, LANES), lambda i: (0, 0)),
                  pl.BlockSpec((d, LANES), lambda i: (0, 0)),
                  pl.BlockSpec((1, LANES), lambda i: (0, 0))],
        out_specs=[pl.BlockSpec((tm, LANES), lambda i: (i, 0)), pl.BlockSpec((tm, LANES), lambda i: (i, 0))],
        out_shape=[jax.ShapeDtypeStruct((n, LANES), I32), jax.ShapeDtypeStruct((n, LANES), F32)],
    )(x, w_hi, w_lo, bias)


def _row_copy(src_hbm, row, dst, slot, sem, chunks):
    src = src_hbm.at[pl.ds(pl.multiple_of(row * chunks, chunks), chunks)]
    return pltpu.make_async_copy(src, dst.at[pl.ds(pl.multiple_of(slot * chunks, chunks), chunks)], sem)


def _moe_up(blk_e, n_valid, x_chunks, idx3, w_up, b_up, pick_even, *, layer, n_rows, d, f, tm, tf):
    chunks = d // LANES
    n_blk = n_rows // tm

    def body(be_ref, nv_ref, idx_ref, nxt_ref, x_hbm, w_ref, b_ref, pick_ref, o_ref, buf, xs, sem):
        i = pl.program_id(0)
        j = pl.program_id(1)
        nv = nv_ref[0]
        slot = i % 2

        def start_rows(src_idx, s):
            def issue(r8, c):
                for u in range(GATHER_UNROLL):
                    r = r8 * GATHER_UNROLL + u
                    _row_copy(x_hbm, src_idx[0, 0, r], buf.at[s], r, sem.at[s], chunks).start(priority=u % 2)
                return c
            lax.fori_loop(0, tm // GATHER_UNROLL, issue, 0)

        @pl.when((j == 0) & (i < nv))
        def _():
            @pl.when(i == 0)
            def _():
                start_rows(idx_ref, 0)

            def wait(r8, c):
                for u in range(GATHER_UNROLL):
                    _row_copy(x_hbm, 0, buf.at[slot], r8 * GATHER_UNROLL + u, sem.at[slot], chunks).wait()
                return c
            lax.fori_loop(0, tm // GATHER_UNROLL, wait, 0)

            @pl.when(i + 1 < nv)
            def _():
                start_rows(nxt_ref, 1 - slot)

            for c in range(chunks):
                xs[:, c * LANES:(c + 1) * LANES] = buf[slot, pl.ds(c, tm, stride=chunks), :].astype(BF16)

        @pl.when(i < nv)
        def _():
            h = _dot(xs[...], w_ref[0, 0].astype(BF16)) + b_ref[0]
            g = jnp.minimum(h, SWIGLU_LIMIT)
            glu = g * _sigmoid(SWIGLU_ALPHA * g)
            lin = jnp.clip(h, -SWIGLU_LIMIT, SWIGLU_LIMIT) + 1.0
            pairs = [glu[:, s * LANES:(s + 1) * LANES] * pltpu.roll(lin[:, s * LANES:(s + 1) * LANES], LANES - 1, 1)
                     for s in range(2 * tf // LANES)]
            o_ref[...] = _dot(jnp.concatenate(pairs, axis=1).astype(BF16), pick_ref[...]).astype(BF16)

        @pl.when(i >= nv)
        def _():
            o_ref[...] = jnp.zeros(o_ref.shape, BF16)

    idx_spec = pl.BlockSpec((1, 1, tm), lambda i, j, be, nv: (i, 0, 0), memory_space=pltpu.SMEM)
    nxt_spec = pl.BlockSpec((1, 1, tm), lambda i, j, be, nv: (jnp.minimum(i + 1, n_blk - 1), 0, 0), memory_space=pltpu.SMEM)
    return _pcall(
        body, name="moe_up", grid=(n_blk, f // tf), prefetch=2,
        in_specs=[idx_spec, nxt_spec, pl.BlockSpec(memory_space=pl.ANY),
                  pl.BlockSpec((1, 1, d, 2 * tf), lambda i, j, be, nv: (layer, be[i], 0, j)),
                  pl.BlockSpec((1, 1, 2 * tf), lambda i, j, be, nv: (be[i], 0, j)),
                  pl.BlockSpec((2 * tf, tf), lambda i, j, be, nv: (0, 0))],
        out_specs=pl.BlockSpec((tm, tf), lambda i, j, be, nv: (i, j)),
        out_shape=jax.ShapeDtypeStruct((n_rows, f), BF16),
        scratch=[pltpu.VMEM((2, tm * chunks, LANES), F32), pltpu.VMEM((tm, d), BF16), pltpu.SemaphoreType.DMA((2,))],
    )(blk_e, n_valid, idx3, idx3, x_chunks, w_up, b_up, pick_even)


def _moe_down(blk_e, fresh, n_valid, a, w_down, b_down, *, layer, n_rows, d, f, tm):
    chunks = d // LANES

    def body(be_ref, fr_ref, nv_ref, a_ref, w_ref, b_ref, o_ref, wb_ref):
        i = pl.program_id(0)

        @pl.when(fr_ref[i] == 1)
        def _():
            wb_ref[...] = w_ref[0, 0].astype(BF16)

        @pl.when(i < nv_ref[0])
        def _():
            res = _dot(a_ref[...], wb_ref[...]) + b_ref[0]
            for c in range(chunks):
                o_ref[pl.ds(c, tm, stride=chunks), :] = res[:, c * LANES:(c + 1) * LANES]

        @pl.when(i >= nv_ref[0])
        def _():
            o_ref[...] = jnp.zeros(o_ref.shape, F32)

    return _pcall(
        body, name="moe_down", grid=(n_rows // tm,), prefetch=3,
        in_specs=[pl.BlockSpec((tm, f), lambda i, be, fr, nv: (i, 0)),
                  pl.BlockSpec((1, 1, f, d), lambda i, be, fr, nv: (layer, be[i], 0, 0)),
                  pl.BlockSpec((1, 1, d), lambda i, be, fr, nv: (be[i], 0, 0))],
        out_specs=pl.BlockSpec((tm * chunks, LANES), lambda i, be, fr, nv: (i, 0)),
        out_shape=jax.ShapeDtypeStruct((n_rows * chunks, LANES), F32),
        scratch=[pltpu.VMEM((f, d), BF16)],
    )(blk_e, fresh, n_valid, a, w_down, b_down)


def _moe_combine(pos3, ys, gates, x, ln_g, ln_b, *, n, d, tq, alpha):
    chunks = d // LANES
    n_tiles = n // tq

    def body(pos_ref, nxt_ref, y_hbm, g_ref, x_ref, lg_ref, lb_ref, of_ref, ob_ref, buf, y_ref, sem):
        i = pl.program_id(0)
        slot = i % 2

        def start_rows(src_pos, s):
            def issue(r, c):
                for kk in range(TOP_K):
                    _row_copy(y_hbm, src_pos[0, 0, kk * tq + r], buf.at[s, kk], r, sem.at[s], chunks).start(priority=kk % 2)
                return c
            lax.fori_loop(0, tq, issue, 0)

        @pl.when(i == 0)
        def _():
            start_rows(pos_ref, 0)

        def wait(r, c):
            for kk in range(TOP_K):
                _row_copy(y_hbm, 0, buf.at[slot, kk], r, sem.at[slot], chunks).wait()
            return c
        lax.fori_loop(0, tq, wait, 0)

        @pl.when(i + 1 < n_tiles)
        def _():
            start_rows(nxt_ref, 1 - slot)

        gv = g_ref[...]
        for c in range(chunks):
            piece = gv[:, 0:1] * buf[slot, 0, pl.ds(c, tq, stride=chunks), :]
            for kk in range(1, TOP_K):
                piece = piece + gv[:, kk:kk + 1] * buf[slot, kk, pl.ds(c, tq, stride=chunks), :]
            y_ref[:, c * LANES:(c + 1) * LANES] = piece
        out = _layer_norm(alpha * x_ref[...] + y_ref[...], lg_ref[...], lb_ref[...])
        of_ref[...] = out
        ob_ref[...] = out.astype(BF16)

    return _pcall(
        body, name="moe_combine_ln", grid=(n_tiles,),
        in_specs=[pl.BlockSpec((1, 1, TOP_K * tq), lambda i: (i, 0, 0), memory_space=pltpu.SMEM),
                  pl.BlockSpec((1, 1, TOP_K * tq), lambda i: (jnp.minimum(i + 1, n_tiles - 1), 0, 0), memory_space=pltpu.SMEM),
                  pl.BlockSpec(memory_space=pl.ANY),
                  pl.BlockSpec((tq, LANES), lambda i: (i, 0)),
                  pl.BlockSpec((tq, d), lambda i: (i, 0)),
                  pl.BlockSpec((1, d), lambda i: (0, 0)),
                  pl.BlockSpec((1, d), lambda i: (0, 0))],
        out_specs=[pl.BlockSpec((tq, d), lambda i: (i, 0)), pl.BlockSpec((tq, d), lambda i: (i, 0))],
        out_shape=[jax.ShapeDtypeStruct((n, d), F32), jax.ShapeDtypeStruct((n, d), BF16)],
        scratch=[pltpu.VMEM((2, TOP_K, tq * chunks, LANES), F32), pltpu.VMEM((tq, d), F32), pltpu.SemaphoreType.DMA((2,))],
    )(pos3, pos3, ys, gates, x, ln_g, ln_b)


def _t5_bucket(dist):
    max_exact = REL_BUCKETS // 2
    dd = jnp.maximum(dist, 1).astype(F32)
    large = max_exact + (jnp.log(dd / max_exact) / math.log(REL_MAX_DIST / max_exact)
                         * (REL_BUCKETS - max_exact)).astype(I32)
    large = jnp.minimum(large, REL_BUCKETS - 1)
    return jnp.where(dist < max_exact, dist, large)


def _tables(positions, rel_bias, t):
    half = MLA_ROPE // 2
    inv = 1.0 / (ROPE_THETA ** (jnp.arange(0, MLA_ROPE, 2, dtype=F32) / MLA_ROPE))
    ang = positions.astype(F32)[:, None] * inv[None, :]
    zeros = jnp.zeros((t, LANES - MLA_ROPE), F32)
    cos = jnp.concatenate([jnp.cos(ang), jnp.cos(ang), zeros], axis=-1)
    sin = jnp.concatenate([jnp.sin(ang), jnp.sin(ang), zeros], axis=-1)
    assert half * 2 == MLA_ROPE

    first = jnp.searchsorted(_t5_bucket(jnp.arange(t)), jnp.arange(REL_BUCKETS), side="left").astype(I32)
    rb = rel_bias.astype(F32)

    def bias_of(dist, valid):
        out = jnp.broadcast_to(rb[0].reshape((NSA_HEADS,) + (1,) * dist.ndim), (NSA_HEADS,) + dist.shape)
        for bkt in range(1, REL_BUCKETS):
            out = jnp.where((dist >= first[bkt])[None], rb[bkt].reshape((NSA_HEADS,) + (1,) * dist.ndim), out)
        return jnp.where(valid[None], out, NEG_INF)

    tb = NSA_TILE
    sat = rb[REL_BUCKETS - 1]
    cvec = jnp.broadcast_to(sat.reshape(NSA_GROUPS, NSA_HPG, 1), (NSA_GROUPS, NSA_HPG, 4 * tb))
    span = WINDOW + tb
    wd = (jnp.arange(WINDOW // tb + 1, dtype=I32) * tb)[:, None, None] + jnp.arange(tb, dtype=I32)[None, :, None] \
        - jnp.arange(span, dtype=I32)[None, None, :]
    wtab = bias_of(wd, (wd >= 0) & (wd < WINDOW)).transpose(1, 0, 2, 3)
    stab = bias_of(wd, wd >= 0).transpose(1, 0, 2, 3)
    nb = t // CMP_STRIDE
    dist_c = jnp.arange(t, dtype=I32)[:, None] - (jnp.arange(nb, dtype=I32) * CMP_STRIDE + CMP_LEN - 1)[None, :]
    bias_c = bias_of(dist_c, dist_c >= 0)
    tq_mla = _pick_tile(t, 512)
    cm = jnp.arange(tq_mla, dtype=I32)
    causal = jnp.where(cm[None, :] <= cm[:, None], 0.0, NEG_INF).astype(F32)

    n_slc = t // SEL_BLOCK
    tok = np.arange(nb)[:, None] * CMP_STRIDE + np.arange(CMP_LEN)[None, :]
    cover = (tok[:, :, None] // SEL_BLOCK == np.arange(LANES)[None, None, :]).sum(1) / CMP_LEN
    cover[nb - 1] = 0.0
    cover[:, n_slc:] = 0.0
    expand_t = (np.arange(t)[:, None] // SEL_BLOCK == np.arange(LANES)[None, :])
    return dict(cos=cos, sin=sin, stab=stab, cvec=cvec, wtab=wtab, bias_c=bias_c, causal=causal,
                cover=jnp.asarray(cover, BF16), expand_t=jnp.asarray(expand_t, BF16))


def _pad_heads(w, heads, width):
    k = w.shape[0]
    w = w.reshape(k, heads, width)
    return jnp.pad(w, ((0, 0), (0, 0), (0, HEAD_PAD - width))).reshape(k, heads * HEAD_PAD)


def _pick_tile(total, want):
    tile = min(total, want)
    while total % tile:
        tile //= 2
    return tile


def _token_mixer(x, xb, tabs, p, *, b, t, d):
    n = b * t
    tm = _pick_tile(t, 512)
    tpb = t // tm
    offs = np.cumsum([0, MLA_Q_LORA, MLA_KV_LORA, MLA_ROPE, NSA_HEADS * NSA_DK,
                      NSA_GROUPS * NSA_DK, NSA_GROUPS * NSA_DV, NSA_GROUPS * NSA_DK, NSA_GROUPS * NSA_DV,
                      NSA_GROUPS * NSA_DK, NSA_GROUPS * NSA_DV, NSA_HEADS * NSA_BRANCHES, 2 * d])
    w_in = p["w_in"]
    seg = [w_in[:, offs[s]:offs[s + 1]] for s in range(12)]
    x_lhs = [(xb, _row_spec(tm, d))]
    mla_scale = (MLA_NOPE + MLA_ROPE) ** -0.5
    nsa_scale = NSA_DK ** -0.5

    def cast_epi(acc, er, orr):
        orr[0][...] = acc.astype(BF16)

    lat_w = MLA_Q_LORA + MLA_KV_LORA
    w_lat = jnp.concatenate([seg[0], seg[1]], axis=1).astype(BF16)
    tn_lat = _pick_tile(lat_w, 1024)
    (lat,) = _matmul("proj_latents", x_lhs, w_lat, [], [(jax.ShapeDtypeStruct((n, lat_w), BF16), _tile_spec(tm, tn_lat))],
                     _cast_prologue, cast_epi, m=n, k=d, n=lat_w, tm=tm, tn=tn_lat)

    w_small = jnp.concatenate([seg[2], jnp.zeros((d, LANES - MLA_ROPE), F32), seg[10],
                               jnp.zeros((d, LANES - NSA_HEADS * NSA_BRANCHES), F32)], axis=1).astype(BF16)

    def small_epi(acc, er, orr):
        orr[0][...] = _rope_slab(acc[:, :LANES], er[0][...], er[1][...]).astype(BF16)
        orr[1][...] = _sigmoid(acc[:, LANES:])

    cs_extras = [(tabs["cos_n"], _row_spec(tm, LANES)), (tabs["sin_n"], _row_spec(tm, LANES))]
    kr, nsa_gates = _matmul(
        "proj_rope_key_gates", x_lhs, w_small, cs_extras,
        [(jax.ShapeDtypeStruct((n, LANES), BF16), _row_spec(tm, LANES)),
         (jax.ShapeDtypeStruct((n, LANES), F32), _row_spec(tm, LANES))],
        _cast_prologue, small_epi, m=n, k=d, n=2 * LANES, tm=tm, tn=2 * LANES)

    hb = 2
    hq = 4
    w_qn = _pad_heads(seg[3], NSA_HEADS, NSA_DK).astype(BF16)
    (q_nsa,) = _matmul("proj_nsa_q", x_lhs, w_qn, [],
                       [(jax.ShapeDtypeStruct((b, NSA_HEADS, t, HEAD_PAD), BF16), _head_spec(tm, hq, HEAD_PAD, tpb))],
                       _cast_prologue, _heads_epilogue(hq, HEAD_PAD, nsa_scale),
                       m=n, k=d, n=NSA_HEADS * HEAD_PAD, tm=tm, tn=hq * HEAD_PAD)

    w_k6 = jnp.concatenate([_pad_heads(seg[s], NSA_GROUPS, NSA_DK) for s in (4, 6, 8)], axis=1).astype(BF16)
    (k6,) = _matmul("proj_nsa_k", x_lhs, w_k6, [],
                    [(jax.ShapeDtypeStruct((b, 3 * NSA_GROUPS, t, HEAD_PAD), BF16), _head_spec(tm, hb, HEAD_PAD, tpb))],
                    _cast_prologue, _heads_epilogue(hb, HEAD_PAD),
                    m=n, k=d, n=3 * NSA_GROUPS * HEAD_PAD, tm=tm, tn=hb * HEAD_PAD)

    (v_c,) = _matmul("proj_nsa_v_cmp", x_lhs, seg[5].astype(BF16), [],
                     [(jax.ShapeDtypeStruct((b, NSA_GROUPS, t, NSA_DV), BF16), _head_spec(tm, hb, NSA_DV, tpb))],
                     _cast_prologue, _heads_epilogue(hb, NSA_DV),
                     m=n, k=d, n=NSA_GROUPS * NSA_DV, tm=tm, tn=hb * NSA_DV)

    def ones_lane(rows):
        return jnp.where(lax.broadcasted_iota(I32, (rows, HEAD_PAD - NSA_DV), 1) == 0, 1.0, 0.0).astype(BF16)

    def v_ones_epi(acc, er, orr):
        for c in range(hb):
            orr[0][0, c, :, :NSA_DV] = acc[:, c * NSA_DV:(c + 1) * NSA_DV].astype(BF16)
            orr[0][0, c, :, NSA_DV:] = ones_lane(acc.shape[0])

    w_v4 = jnp.concatenate([seg[7], seg[9]], axis=1).astype(BF16)
    (v4,) = _matmul("proj_nsa_v", x_lhs, w_v4, [],
                    [(jax.ShapeDtypeStruct((b, 2 * NSA_GROUPS, t, HEAD_PAD), BF16), _head_spec(tm, hb, HEAD_PAD, tpb))],
                    _cast_prologue, v_ones_epi, m=n, k=d, n=2 * NSA_GROUPS * NSA_DV, tm=tm, tn=hb * NSA_DV)

    def sig_epi(acc, er, orr):
        orr[0][...] = _sigmoid(acc).astype(BF16)

    tn_d = _pick_tile(d, 1024)
    (merge,) = _matmul("proj_merge_gates", x_lhs, seg[11].astype(BF16), [],
                       [(jax.ShapeDtypeStruct((n, 2 * d), BF16), _tile_spec(tm, tn_d))],
                       _cast_prologue, sig_epi, m=n, k=d, n=2 * d, tm=tm, tn=tn_d)

    w_uq = p["mla_w_uq"].reshape(MLA_Q_LORA, MLA_HEADS, MLA_NOPE + MLA_ROPE)
    w_uq = jnp.pad(w_uq, ((0, 0), (0, 0), (0, HEAD_PAD - MLA_NOPE - MLA_ROPE))).reshape(MLA_Q_LORA, MLA_HEADS * HEAD_PAD)

    def q_epi(acc, er, orr):
        cos, sin = er[0][...], er[1][...]
        for c in range(hq):
            orr[0][0, c, :, :MLA_NOPE] = (acc[:, c * HEAD_PAD:c * HEAD_PAD + MLA_NOPE] * mla_scale).astype(BF16)
            slab = acc[:, c * HEAD_PAD + MLA_NOPE:(c + 1) * HEAD_PAD] * mla_scale
            orr[0][0, c, :, MLA_NOPE:] = _rope_slab(slab, cos, sin).astype(BF16)

    (q_mla,) = _matmul(
        "mla_q_up", [(lat, _row_spec(tm, MLA_Q_LORA)), (p["mla_q_norm"].reshape(1, -1), pl.BlockSpec((1, MLA_Q_LORA), lambda i, j: (0, 0)))],
        w_uq.astype(BF16), cs_extras,
        [(jax.ShapeDtypeStruct((b, MLA_HEADS, t, HEAD_PAD), BF16), _head_spec(tm, hq, HEAD_PAD, tpb))],
        _rms_prologue, q_epi, m=n, k=MLA_Q_LORA, n=MLA_HEADS * HEAD_PAD, tm=tm, tn=hq * HEAD_PAD)

    kv_w = MLA_NOPE + MLA_V

    def kv_epi(acc, er, orr):
        k_rope = er[0][...]
        for c in range(hq):
            orr[0][0, c, :, :MLA_NOPE] = acc[:, c * kv_w:c * kv_w + MLA_NOPE].astype(BF16)
            orr[0][0, c, :, MLA_NOPE:] = k_rope
            orr[1][0, c, :, :MLA_V] = acc[:, c * kv_w + MLA_NOPE:(c + 1) * kv_w].astype(BF16)
            orr[1][0, c, :, MLA_V:] = ones_lane(acc.shape[0])

    k_mla, v_mla = _matmul(
        "mla_kv_up", [(lat, _row_spec(tm, MLA_KV_LORA, MLA_Q_LORA // MLA_KV_LORA)),
                      (p["mla_kv_norm"].reshape(1, -1), pl.BlockSpec((1, MLA_KV_LORA), lambda i, j: (0, 0)))],
        p["mla_w_ukv"].astype(BF16), [(kr, _row_spec(tm, LANES))],
        [(jax.ShapeDtypeStruct((b, MLA_HEADS, t, HEAD_PAD), BF16), _head_spec(tm, hq, HEAD_PAD, tpb)),
         (jax.ShapeDtypeStruct((b, MLA_HEADS, t, HEAD_PAD), BF16), _head_spec(tm, hq, HEAD_PAD, tpb))],
        _rms_prologue, kv_epi, m=n, k=MLA_KV_LORA, n=MLA_HEADS * kv_w, tm=tm, tn=hq * kv_w)

    attn_a = _mla_attention(q_mla, k_mla, v_mla, tabs["causal"], b=b, t=t, tq=tabs["causal"].shape[0])

    def ya_epi(acc, er, orr):
        orr[0][...] = (er[0][...].astype(F32) * acc).astype(BF16)

    (y_a,) = _matmul("mla_branch_out", [(attn_a.reshape(n, MLA_HEADS * MLA_V), _row_spec(tm, MLA_HEADS * MLA_V))],
                     p["w_branch_a"].astype(BF16), [(merge, _tile_spec(tm, tn_d))],
                     [(jax.ShapeDtypeStruct((n, d), BF16), _tile_spec(tm, tn_d))],
                     _cast_prologue, ya_epi, m=n, k=MLA_HEADS * MLA_V, n=d, tm=tm, tn=tn_d)

    def cmp_weights(pe, w1, w2, width_in, width):
        pe_p = jnp.pad(pe, ((0, 0), (0, width - width_in)))
        pe2 = jnp.zeros((8, CMP_STRIDE * width), F32).at[0].set(pe_p[:CMP_STRIDE].reshape(-1)).at[1].set(pe_p[CMP_STRIDE:].reshape(-1))
        w1p = jnp.pad(w1.reshape(CMP_LEN, width_in, width_in), ((0, 0), (0, width - width_in), (0, width - width_in)))
        w2p = jnp.pad(w2, ((0, width - width_in), (0, width - width_in)))
        return (pe2.astype(BF16), w1p[:CMP_STRIDE].reshape(-1, width).astype(BF16),
                w1p[CMP_STRIDE:].reshape(-1, width).astype(BF16), w2p.astype(BF16))

    kcmp = _compress(k6, *cmp_weights(p["cmp_pe_k"], p["cmp_w1_k"], p["cmp_w2_k"], NSA_DK, HEAD_PAD), b=b, t=t, width=HEAD_PAD)
    vcmp = _compress(v_c, *cmp_weights(p["cmp_pe_v"], p["cmp_w1_v"], p["cmp_w2_v"], NSA_DV, NSA_DV), b=b, t=t, width=NSA_DV)
    gates3 = nsa_gates.reshape(b, t, LANES)
    o_cmp, sel = _nsa_cmp(q_nsa, kcmp, vcmp, tabs["bias_c"], tabs["cover"], gates3, b=b, t=t)
    o_slc = _nsa_sel(q_nsa, k6, v4, tabs["stab"], tabs["cvec"], gates3, sel, tabs["expand_t"], b=b, t=t)
    o_win = _nsa_win(q_nsa, k6, v4, tabs["wtab"], gates3, b=b, t=t)

    def sum3_prologue(a_ref, b_ref, c_ref):
        return (a_ref[...].astype(F32) + b_ref[...].astype(F32) + c_ref[...].astype(F32)).astype(BF16)

    def merge_epi(acc, er, orr):
        orr[0][...] = (er[0][...].astype(F32) + er[1][...].astype(F32) * acc).astype(BF16)

    hv = NSA_HEADS * NSA_DV
    (merged,) = _matmul("nsa_branch_out_merge", [(o.reshape(n, hv), _row_spec(tm, hv)) for o in (o_cmp, o_slc, o_win)],
                        p["w_branch_b"].astype(BF16), [(y_a, _tile_spec(tm, tn_d)), (merge, _tile_spec(tm, tn_d, d // tn_d))],
                        [(jax.ShapeDtypeStruct((n, d), BF16), _tile_spec(tm, tn_d))],
                        sum3_prologue, merge_epi, m=n, k=hv, n=d, tm=tm, tn=tn_d)

    tm_ln = _pick_tile(t, 256)
    alpha = p["alpha"]

    chunks = d // LANES

    def ln_epi(acc, er, orr):
        out = _layer_norm(alpha * er[0][...] + acc, er[1][...], er[2][...])
        orr[0][...] = out
        for c in range(chunks):
            orr[1][pl.ds(c, tm_ln, stride=chunks), :] = out[:, c * LANES:(c + 1) * LANES]

    vec_spec = pl.BlockSpec((1, d), lambda i, j: (0, 0))
    x1, x1_chunks = _matmul(
        "mixer_out_ln", [(merged, _row_spec(tm_ln, d))], p["w_out"].astype(BF16),
        [(x, _row_spec(tm_ln, d)), (p["ln1_g"].reshape(1, d), vec_spec), (p["ln1_b"].reshape(1, d), vec_spec)],
        [(jax.ShapeDtypeStruct((n, d), F32), _row_spec(tm_ln, d)),
         (jax.ShapeDtypeStruct((n * chunks, LANES), F32), pl.BlockSpec((tm_ln * chunks, LANES), lambda i, j: (i, 0)))],
        _cast_prologue, ln_epi, m=n, k=d, n=d, tm=tm_ln, tn=d)
    return x1, x1_chunks


def _moe(x, x_chunks, p, *, n, d):
    _, n_exp, _, f2 = p["exp_w_up"].shape
    f = f2 // 2
    tm_e = p["tm_e"]
    rw = jnp.pad(p["router_w"], ((0, 0), (0, LANES - n_exp)))
    rw_hi = rw.astype(BF16)
    rw_lo = (rw - rw_hi.astype(F32)).astype(BF16)
    rb = jnp.pad(p["router_b"], (0, LANES - n_exp)).reshape(1, LANES)
    top_e, gates = _router(x, rw_hi, rw_lo, rb, n=n, d=d, n_exp=n_exp, tm=_pick_tile(n, 512))

    n_pairs = n * TOP_K
    e_flat = top_e[:, :TOP_K].reshape(-1)
    grp = LANES
    onehot = (e_flat[:, None] == jnp.arange(n_exp, dtype=I32)[None, :]).astype(F32).reshape(n_pairs // grp, grp, n_exp)
    tri = jnp.tril(jnp.ones((grp, grp), F32))
    within = jnp.einsum("ij,gje->gie", tri, onehot)
    totals = within[:, -1, :]
    before = jnp.cumsum(totals, axis=0) - totals
    rank = (jnp.sum((within + before[:, None, :]) * onehot, axis=-1).reshape(-1) - 1.0).astype(I32)
    counts = jnp.sum(totals, axis=0).astype(I32)
    padded = (counts + tm_e - 1) // tm_e * tm_e
    pad_end = jnp.cumsum(padded)
    dest = (pad_end - padded)[e_flat] + rank
    n_rows = n_pairs + n_exp * tm_e
    n_blk = n_rows // tm_e
    row_tok = jnp.zeros((n_rows,), I32).at[dest].set(jnp.arange(n_pairs, dtype=I32) // TOP_K)
    n_valid = (pad_end[-1] // tm_e).astype(I32)
    blk_e = jnp.searchsorted(pad_end, jnp.arange(n_blk, dtype=I32) * tm_e, side="right").astype(I32)
    blk_e = jnp.minimum(blk_e, blk_e[jnp.maximum(n_valid - 1, 0)])
    n_valid = n_valid.reshape(1)

    tf = _pick_tile(f, 512)
    pick_even = jnp.asarray(np.arange(2 * tf)[:, None] == 2 * np.arange(tf)[None, :], BF16)
    a = _moe_up(blk_e, n_valid, x_chunks, row_tok.reshape(n_blk, 1, tm_e),
                p["exp_w_up"], p["exp_b_up"].reshape(n_exp, 1, 2 * f), pick_even,
                layer=p["layer"], n_rows=n_rows, d=d, f=f, tm=tm_e, tf=tf)
    split = max(1, tm_e // 256)
    tm_d = tm_e // split
    blk_d = jnp.repeat(blk_e, split)
    fresh = jnp.concatenate([jnp.ones((1,), I32), (blk_d[1:] != blk_d[:-1]).astype(I32)])
    ys = _moe_down(blk_d, fresh, n_valid * split, a, p["exp_w_down"], p["exp_b_down"].reshape(n_exp, 1, d),
                   layer=p["layer"], n_rows=n_rows, d=d, f=f, tm=tm_d)

    tq = _pick_tile(n, 128)
    pos3 = dest.reshape(n // tq, tq, TOP_K).transpose(0, 2, 1).reshape(n // tq, 1, TOP_K * tq)
    return _moe_combine(pos3, ys, gates, x, p["ln2_g"].reshape(1, d), p["ln2_b"].reshape(1, d),
                        n=n, d=d, tq=tq, alpha=p["alpha"])


def kernel(x, positions, rel_bias, w_in, mla_q_norm, mla_w_uq, mla_kv_norm, mla_w_ukv, cmp_pe_k, cmp_w1_k, cmp_w2_k, cmp_pe_v, cmp_w1_v, cmp_w2_v, w_branch_a, w_branch_b, w_out, ln1_g, ln1_b, router_w, router_b, exp_w_up, exp_b_up, exp_w_down, exp_b_down, ln2_g, ln2_b):
    b, t, d = x.shape
    depth = w_in.shape[0]
    n = b * t
    alpha = (2 * depth) ** 0.25
    tabs = _tables(positions, rel_bias, t)
    tabs["cos_n"] = jnp.tile(tabs["cos"], (b, 1))
    tabs["sin_n"] = jnp.tile(tabs["sin"], (b, 1))
    xf = x.reshape(n, d)
    xb = xf.astype(BF16)
    for l in range(depth):
        p = dict(w_in=w_in[l], mla_q_norm=mla_q_norm[l], mla_w_uq=mla_w_uq[l], mla_kv_norm=mla_kv_norm[l],
                 mla_w_ukv=mla_w_ukv[l], cmp_pe_k=cmp_pe_k[l], cmp_w1_k=cmp_w1_k[l], cmp_w2_k=cmp_w2_k[l],
                 cmp_pe_v=cmp_pe_v[l], cmp_w1_v=cmp_w1_v[l], cmp_w2_v=cmp_w2_v[l], w_branch_a=w_branch_a[l],
                 w_branch_b=w_branch_b[l], w_out=w_out[l], ln1_g=ln1_g[l], ln1_b=ln1_b[l], router_w=router_w[l],
                 router_b=router_b[l], exp_w_up=exp_w_up, exp_b_up=exp_b_up[l], exp_w_down=exp_w_down, layer=l,
                 exp_b_down=exp_b_down[l], ln2_g=ln2_g[l], ln2_b=ln2_b[l], alpha=alpha, tm_e=_pick_tile(n * TOP_K, 512))
        x1, x1_chunks = _token_mixer(xf, xb, tabs, p, b=b, t=t, d=d)
        xf, xb = _moe(x1, x1_chunks, p, n=n, d=d)
    return xf.reshape(b, t, d)
```

```python
import functools
import math

import numpy as np
import jax
import jax.numpy as jnp
from jax import lax
from jax.experimental import pallas as pl
from jax.experimental.pallas import tpu as pltpu

F32, BF16, I32 = jnp.float32, jnp.bfloat16, jnp.int32

MLA_HEADS = 16
MLA_Q_LORA = 1536
MLA_KV_LORA = 512
MLA_NOPE = 128
MLA_ROPE = 64
MLA_V = 128
ROPE_THETA = 10000.0
NSA_HEADS = 16
NSA_GROUPS = 2
NSA_HPG = NSA_HEADS // NSA_GROUPS
NSA_DK = 192
NSA_DV = 128
CMP_LEN = 32
CMP_STRIDE = 16
SEL_BLOCK = 64
SEL_COUNT = 16
WINDOW = 512
NSA_BRANCHES = 3
SEL_FORCE = 1.0e4
REL_BUCKETS = 32
REL_MAX_DIST = 128
TOP_K = 4
SWIGLU_LIMIT = 7.0
SWIGLU_ALPHA = 1.702
NEG_INF = -1.0e30
LN_EPS = 1e-5
RMS_EPS = 1e-6

LANES = 128
HEAD_PAD = 256
NSA_TILE = 128
GATHER_UNROLL = 8
VMEM_LIMIT_BYTES = 56 * 1024 * 1024


def _pcall(body, *, name, grid, in_specs, out_specs, out_shape, scratch=(), prefetch=0):
    gs = pltpu.PrefetchScalarGridSpec(num_scalar_prefetch=prefetch, grid=grid, in_specs=in_specs,
                                      out_specs=out_specs, scratch_shapes=list(scratch))
    return pl.pallas_call(
        body, grid_spec=gs, out_shape=out_shape, name=name,
        compiler_params=pltpu.CompilerParams(dimension_semantics=("arbitrary",) * len(grid),
                                             vmem_limit_bytes=VMEM_LIMIT_BYTES))


def _sigmoid(x):
    return 1.0 / (1.0 + jnp.exp(-x))


def _dot_t(a, b):
    return lax.dot_general(a, b, (((1,), (1,)), ((), ())), preferred_element_type=F32)


def _dot(a, b):
    return jnp.dot(a, b, preferred_element_type=F32)


def _layer_norm(r, g, b):
    mu = jnp.mean(r, axis=-1, keepdims=True)
    c = r - mu
    var = jnp.mean(c * c, axis=-1, keepdims=True)
    return c * lax.rsqrt(var + LN_EPS) * g + b


def _matmul(name, lhs, w, extras, outs, prologue, epilogue, *, m, k, n, tm, tn):
    nl, ne, no = len(lhs), len(extras), len(outs)

    def body(*refs):
        lr = refs[:nl]
        wr = refs[nl]
        er = refs[nl + 1:nl + 1 + ne]
        orr = refs[nl + 1 + ne:nl + 1 + ne + no]
        xs = refs[-1]

        @pl.when(pl.program_id(1) == 0)
        def _():
            xs[...] = prologue(*lr)

        epilogue(_dot(xs[...], wr[...]), er, orr)

    return _pcall(
        body, name=name, grid=(m // tm, n // tn),
        in_specs=[s for _, s in lhs] + [pl.BlockSpec((k, tn), lambda i, j: (0, j))] + [s for _, s in extras],
        out_specs=[s for _, s in outs], out_shape=[o for o, _ in outs],
        scratch=[pltpu.VMEM((tm, k), BF16)],
    )(*[a for a, _ in lhs], w, *[a for a, _ in extras])


def _row_spec(tm, width, col=0):
    return pl.BlockSpec((tm, width), lambda i, j: (i, col))


def _tile_spec(tm, tn, col_off=0):
    return pl.BlockSpec((tm, tn), lambda i, j: (i, j + col_off))


def _head_spec(tm, hb, width, tpb):
    return pl.BlockSpec((1, hb, tm, width), lambda i, j: (i // tpb, j, i % tpb, 0))


def _cast_prologue(x_ref):
    return x_ref[...].astype(BF16)


def _rms_prologue(x_ref, g_ref):
    xf = x_ref[...].astype(F32)
    ms = jnp.mean(xf * xf, axis=-1, keepdims=True)
    return (xf * lax.rsqrt(ms + RMS_EPS) * g_ref[...]).astype(BF16)


def _rope_slab(t, cos, sin):
    lane = lax.broadcasted_iota(I32, t.shape, 1)
    rot = jnp.where(lane < MLA_ROPE // 2, -pltpu.roll(t, LANES - MLA_ROPE // 2, 1), pltpu.roll(t, MLA_ROPE // 2, 1))
    return t * cos + rot * sin


def _heads_epilogue(hb, width, scale=None):
    def epi(acc, er, orr):
        for c in range(hb):
            v = acc[:, c * width:(c + 1) * width]
            if scale is not None:
                v = v * scale
            orr[0][0, c] = v.astype(BF16)
    return epi


def _softmax_init(rows, width):
    return (jnp.full((rows, 1), NEG_INF, F32), jnp.zeros((rows, 1), F32), jnp.zeros((rows, width), F32))


def _softmax_step(s, v, carry):
    m, l, acc = carry
    m_new = jnp.maximum(m, jnp.max(s, axis=-1, keepdims=True))
    a = jnp.exp(m - m_new)
    p = jnp.exp(s - m_new)
    return m_new, a * l + jnp.sum(p, axis=-1, keepdims=True), a * acc + _dot(p.astype(BF16), v)


def _mla_attention(q, k, v_ones, causal, *, b, t, tq):
    nq = t // tq

    def body(q_ref, k_ref, v_ref, c_ref, o_ref):
        i = pl.program_id(2)
        qv = q_ref[0, 0]

        def step(koff, carry, diagonal):
            s = _dot_t(qv, k_ref[0, 0, pl.ds(koff, tq), :])
            if diagonal:
                s = s + c_ref[...]
            return _softmax_step(s, v_ref[0, 0, pl.ds(koff, tq), :], carry)

        def pair(j, c):
            c = step(pl.multiple_of(2 * j * tq, tq), c, False)
            return step(pl.multiple_of((2 * j + 1) * tq, tq), c, False)
        carry = lax.fori_loop(0, i // 2, pair, _softmax_init(tq, MLA_V))
        carry = lax.fori_loop(0, i % 2, lambda j, c: step(pl.multiple_of((i - 1) * tq, tq), c, False), carry)
        _, l, acc = step(pl.multiple_of(i * tq, tq), carry, True)
        o_ref[0] = (acc / l).astype(BF16)

    return _pcall(
        body, name="mla_attention", grid=(b, MLA_HEADS, nq),
        in_specs=[pl.BlockSpec((1, 1, tq, HEAD_PAD), lambda bi, h, i: (bi, h, i, 0)),
                  pl.BlockSpec((1, 1, t, HEAD_PAD), lambda bi, h, i: (bi, h, 0, 0)),
                  pl.BlockSpec((1, 1, t, MLA_V), lambda bi, h, i: (bi, h, 0, 0)),
                  pl.BlockSpec((tq, tq), lambda bi, h, i: (0, 0))],
        out_specs=pl.BlockSpec((1, tq, MLA_V), lambda bi, h, i: (bi, i, h)),
        out_shape=jax.ShapeDtypeStruct((b, t, MLA_HEADS * MLA_V), BF16),
    )(q, k, v_ones, causal)


def _gelu_tanh(x):
    return 0.5 * x * (1.0 + jnp.tanh(math.sqrt(2.0 / math.pi) * (x + 0.044715 * (x * x * x))))


def _compress(kv6, pe2, w1lo, w1hi, w2, *, b, t, width):
    nb = t // CMP_STRIDE
    half = CMP_STRIDE * width
    h = kv6.reshape(b, kv6.shape[1], nb, half)

    def body(h_ref, pe_ref, lo_ref, hi_ref, w2_ref, o_ref):
        hv = h_ref[0, 0]
        lo, hi = lo_ref[...], hi_ref[...]
        a = _dot(hv, lo)
        bb = _dot(hv, hi)
        pe = pe_ref[...]
        c = _dot(pe, lo)[0:1] + _dot(pe, hi)[1:2]
        z = a + pltpu.roll(bb, nb - 1, 0) + c
        o_ref[0, 0] = _dot(_gelu_tanh(z).astype(BF16), w2_ref[...]).astype(BF16)

    return _pcall(
        body, name=f"nsa_compress_{width}", grid=(b, NSA_GROUPS),
        in_specs=[
            pl.BlockSpec((1, 1, nb, half), lambda bi, g: (bi, g, 0, 0)),
            pl.BlockSpec((8, half), lambda bi, g: (0, 0)),
            pl.BlockSpec((half, width), lambda bi, g: (0, 0)),
            pl.BlockSpec((half, width), lambda bi, g: (0, 0)),
            pl.BlockSpec((width, width), lambda bi, g: (0, 0)),
        ],
        out_specs=pl.BlockSpec((1, 1, nb, width), lambda bi, g: (bi, g, 0, 0)),
        out_shape=jax.ShapeDtypeStruct((b, NSA_GROUPS, nb, width), BF16),
    )(h, pe2, w1lo, w1hi, w2)


def _gate_col(gates, col):
    lane = lax.broadcasted_iota(I32, gates.shape, 1)
    return jnp.sum(jnp.where(lane == col, gates, 0.0), axis=-1, keepdims=True)


def _nsa_cmp(q, kcmp, vcmp, bias_c, cover, gates, *, b, t):
    tq = NSA_TILE
    nb = t // CMP_STRIDE
    n_slc = t // SEL_BLOCK
    n_sel = min(SEL_COUNT, n_slc)
    hpg = NSA_HPG

    def body(q_ref, k_ref, v_ref, bias_ref, cov_ref, g_ref, o_ref, sel_ref):
        g = pl.program_id(1)
        i = pl.program_id(2)
        q2 = q_ref[0].reshape(hpg * tq, HEAD_PAD)
        s3 = _dot_t(q2, k_ref[0, 0]).reshape(hpg, tq, nb) + bias_ref[...]
        p = jnp.exp(s3 - jnp.max(s3, axis=-1, keepdims=True))
        has_key = jnp.where(i * tq + lax.broadcasted_iota(I32, (tq, 1), 0) >= CMP_LEN - 1, 1.0, 0.0)
        p = p * (has_key / jnp.maximum(jnp.sum(p, axis=-1, keepdims=True), 1e-30))
        o3 = _dot(p.reshape(hpg * tq, nb).astype(BF16), v_ref[0, 0]).reshape(hpg, tq, NSA_DV)
        _write_gated_heads(o_ref, o3, g_ref[0], g, 0)

        psum = jnp.sum(p, axis=0)
        p_hi = psum.astype(BF16)
        p_lo = (psum - p_hi.astype(F32)).astype(BF16)
        cov = cov_ref[...]
        imp = _dot(p_hi, cov) + _dot(p_lo, cov)

        jl = lax.broadcasted_iota(I32, (tq, LANES), 1)
        cur = (i * tq + lax.broadcasted_iota(I32, (tq, LANES), 0)) // SEL_BLOCK
        forced = (jl == 0) | (jl == cur) | (jl == cur - 1)
        score = jnp.where(forced, SEL_FORCE, jnp.where(jl <= cur, imp, -1.0))
        n_used = -(-n_slc // 8) * 8
        st = jnp.transpose(score)[:n_used]
        jrow = lax.broadcasted_iota(I32, (n_used, tq), 0)
        rank = jnp.zeros((n_used, tq), F32)
        for kk in range(n_slc):
            row = st[kk:kk + 1, :]
            rank = rank + jnp.where(jrow > kk, jnp.where(row >= st, 1.0, 0.0), jnp.where(row > st, 1.0, 0.0))
        chosen_t = jnp.where((rank < n_sel) & (st >= 0.0), 1.0, 0.0)
        if n_used < LANES:
            chosen_t = jnp.concatenate([chosen_t, jnp.zeros((LANES - n_used, tq), F32)], axis=0)
        sel_ref[0, 0] = jnp.transpose(chosen_t).astype(BF16)

    return _pcall(
        body, name="nsa_cmp_select", grid=(b, NSA_GROUPS, t // tq),
        in_specs=[
            pl.BlockSpec((1, hpg, tq, HEAD_PAD), lambda bi, g, i: (bi, g, i, 0)),
            pl.BlockSpec((1, 1, nb, HEAD_PAD), lambda bi, g, i: (bi, g, 0, 0)),
            pl.BlockSpec((1, 1, nb, NSA_DV), lambda bi, g, i: (bi, g, 0, 0)),
            pl.BlockSpec((hpg, tq, nb), lambda bi, g, i: (g, i, 0)),
            pl.BlockSpec((nb, LANES), lambda bi, g, i: (0, 0)),
            pl.BlockSpec((1, tq, LANES), lambda bi, g, i: (bi, i, 0)),
        ],
        out_specs=[
            pl.BlockSpec((1, tq, hpg * NSA_DV), lambda bi, g, i: (bi, i, g)),
            pl.BlockSpec((1, 1, tq, LANES), lambda bi, g, i: (bi, g, i, 0)),
        ],
        out_shape=[jax.ShapeDtypeStruct((b, t, NSA_HEADS * NSA_DV), BF16),
                   jax.ShapeDtypeStruct((b, NSA_GROUPS, t, LANES), BF16)],
    )(q, kcmp, vcmp, bias_c, cover, gates)


def _write_gated_heads(o_ref, o3, gates_v, g, branch):
    for hh in range(NSA_HPG):
        gc = _gate_col(gates_v, (g * NSA_HPG + hh) * NSA_BRANCHES + branch)
        o_ref[0, :, hh * NSA_DV:(hh + 1) * NSA_DV] = (o3[hh] * gc).astype(BF16)


def _nsa_sel(q, k6, v4, stab, cvec, gates, sel, expand_t, *, b, t):
    tb = NSA_TILE
    hpg = NSA_HPG
    wide = 4 * tb
    span = WINDOW + tb
    back = WINDOW // tb
    rows = hpg * tb

    def body(q_ref, k_ref, v_ref, tab_ref, cv_ref, g_ref, sel_ref, et_ref, o_ref):
        g = pl.program_id(1)
        i = pl.program_id(2)
        q2 = q_ref[0].reshape(rows, HEAD_PAD)
        sel_v = sel_ref[0, 0]
        far_end = jnp.maximum(i - back, 0) * tb

        def far(j, carry):
            koff = pl.multiple_of(j * wide, wide)
            s3 = _dot_t(q2, k_ref[0, 0, pl.ds(koff, wide), :]).reshape(hpg, tb, wide) + cv_ref[0][:, None, :]
            key = koff + lax.broadcasted_iota(I32, (tb, wide), 1)
            chosen = jnp.where(key < far_end, _dot_t(sel_v, et_ref[pl.ds(koff, wide), :]), 0.0) > 0.5
            s3 = jnp.where(chosen[None], s3, NEG_INF)
            return _softmax_step(s3.reshape(rows, wide), v_ref[0, 0, pl.ds(koff, wide), :], carry)

        n_far = (far_end + wide - 1) // wide
        carry = lax.fori_loop(0, n_far // 2, lambda j, c: far(2 * j + 1, far(2 * j, c)), _softmax_init(rows, NSA_DV))
        carry = lax.fori_loop(0, n_far % 2, lambda j, c: far(n_far - 1, c), carry)

        koff = pl.multiple_of(far_end, tb)
        s3 = _dot_t(q2, k_ref[0, 0, pl.ds(koff, span), :]).reshape(hpg, tb, span) + tab_ref[0]
        chosen = _dot_t(sel_v, et_ref[pl.ds(koff, span), :]) > 0.5
        s3 = jnp.where(chosen[None], s3, NEG_INF)
        _, l, acc = _softmax_step(s3.reshape(rows, span), v_ref[0, 0, pl.ds(koff, span), :], carry)
        _write_gated_heads(o_ref, (acc / l).reshape(hpg, tb, NSA_DV), g_ref[0], g, 1)

    return _pcall(
        body, name="nsa_sel_attention", grid=(b, NSA_GROUPS, t // tb),
        in_specs=[
            pl.BlockSpec((1, hpg, tb, HEAD_PAD), lambda bi, g, i: (bi, g, i, 0)),
            pl.BlockSpec((1, 1, t, HEAD_PAD), lambda bi, g, i: (bi, NSA_GROUPS + g, 0, 0)),
            pl.BlockSpec((1, 1, t, NSA_DV), lambda bi, g, i: (bi, g, 0, 0)),
            pl.BlockSpec((1, hpg, tb, span), lambda bi, g, i: (jnp.minimum(i, back), g, 0, 0)),
            pl.BlockSpec((1, hpg, wide), lambda bi, g, i: (g, 0, 0)),
            pl.BlockSpec((1, tb, LANES), lambda bi, g, i: (bi, i, 0)),
            pl.BlockSpec((1, 1, tb, LANES), lambda bi, g, i: (bi, g, i, 0)),
            pl.BlockSpec((t, LANES), lambda bi, g, i: (0, 0)),
        ],
        out_specs=pl.BlockSpec((1, tb, hpg * NSA_DV), lambda bi, g, i: (bi, i, g)),
        out_shape=jax.ShapeDtypeStruct((b, t, NSA_HEADS * NSA_DV), BF16),
    )(q, k6, v4, stab, cvec, gates, sel, expand_t)


def _nsa_win(q, k6, v4, wtab, gates, *, b, t):
    tb = NSA_TILE
    hpg = NSA_HPG
    span = WINDOW + tb
    back = WINDOW // tb
    rows = hpg * tb

    def body(q_ref, k_ref, v_ref, w_ref, g_ref, o_ref):
        g = pl.program_id(1)
        i = pl.program_id(2)
        koff = pl.multiple_of(jnp.maximum(i - back, 0) * tb, tb)
        q2 = q_ref[0].reshape(rows, HEAD_PAD)
        s3 = _dot_t(q2, k_ref[0, 0, pl.ds(koff, span), :]).reshape(hpg, tb, span) + w_ref[0]
        s = s3.reshape(rows, span)
        p = jnp.exp(s - jnp.max(s, axis=-1, keepdims=True))
        acc = _dot(p.astype(BF16), v_ref[0, 0, pl.ds(koff, span), :])
        o3 = (acc[:, :NSA_DV] / acc[:, NSA_DV:NSA_DV + 1]).reshape(hpg, tb, NSA_DV)
        _write_gated_heads(o_ref, o3, g_ref[0], g, 2)

    return _pcall(
        body, name="nsa_win_attention", grid=(b, NSA_GROUPS, t // tb),
        in_specs=[
            pl.BlockSpec((1, hpg, tb, HEAD_PAD), lambda bi, g, i: (bi, g, i, 0)),
            pl.BlockSpec((1, 1, t, HEAD_PAD), lambda bi, g, i: (bi, 2 * NSA_GROUPS + g, 0, 0)),
            pl.BlockSpec((1, 1, t, HEAD_PAD), lambda bi, g, i: (bi, NSA_GROUPS + g, 0, 0)),
            pl.BlockSpec((1, hpg, tb, span), lambda bi, g, i: (jnp.minimum(i, back), g, 0, 0)),
            pl.BlockSpec((1, tb, LANES), lambda bi, g, i: (bi, i, 0)),
        ],
        out_specs=pl.BlockSpec((1, tb, hpg * NSA_DV), lambda bi, g, i: (bi, i, g)),
        out_shape=jax.ShapeDtypeStruct((b, t, NSA_HEADS * NSA_DV), BF16),
    )(q, k6, v4, wtab, gates)


def _router(x, w_hi, w_lo, bias, *, n, d, n_exp, tm):
    def body(x_ref, wh_ref, wl_ref, b_ref, e_ref, gate_ref):
        xv = x_ref[...]
        x_hi = xv.astype(BF16)
        x_lo = (xv - x_hi.astype(F32)).astype(BF16)
        logits = _dot(x_hi, wh_ref[...]) + _dot(x_lo, wh_ref[...]) + _dot(x_hi, wl_ref[...]) + b_ref[...]
        lane = lax.broadcasted_iota(I32, (tm, LANES), 1)
        logits = jnp.where(lane < n_exp, logits, NEG_INF)
        e_out = jnp.zeros((tm, LANES), I32)
        v_out = jnp.full((tm, LANES), NEG_INF, F32)
        for kk in range(TOP_K):
            mx = jnp.max(logits, axis=-1, keepdims=True)
            idx = jnp.min(jnp.where(logits == mx, lane, LANES), axis=-1, keepdims=True)
            e_out = jnp.where(lane == kk, idx, e_out)
            v_out = jnp.where(lane == kk, mx, v_out)
            logits = jnp.where(lane == idx, NEG_INF, logits)
        ex = jnp.where(lane < TOP_K, jnp.exp(v_out - jnp.max(v_out, axis=-1, keepdims=True)), 0.0)
        e_ref[...] = e_out
        gate_ref[...] = ex / jnp.sum(ex, axis=-1, keepdims=True)

    return _pcall(
        body, name="moe_router", grid=(n // tm,),
        in_specs=[pl.BlockSpec((tm, d), lambda i: (i, 0)),
                  pl.BlockSpec((d, LANES), lambda i: (0, 0)),
                  pl.BlockSpec((d, LANES), lambda i: (0, 0)),
                  pl.BlockSpec((1, LANES), lambda i: (0, 0))],
        out_specs=[pl.BlockSpec((tm, LANES), lambda i: (i, 0)), pl.BlockSpec((tm, LANES), lambda i: (i, 0))],
        out_shape=[jax.ShapeDtypeStruct((n, LANES), I32), jax.ShapeDtypeStruct((n, LANES), F32)],
    )(x, w_hi, w_lo, bias)


def _row_copy(src_hbm, row, dst, slot, sem, chunks):
    src = src_hbm.at[pl.ds(pl.multiple_of(row * chunks, chunks), chunks)]
    return pltpu.make_async_copy(src, dst.at[pl.ds(pl.multiple_of(slot * chunks, chunks), chunks)], sem)


def _moe_up(blk_e, n_valid, x_chunks, idx3, w_up, b_up, pick_even, *, layer, n_rows, d, f, tm, tf):
    chunks = d // LANES
    n_blk = n_rows // tm

    def body(be_ref, nv_ref, idx_ref, nxt_ref, x_hbm, w_ref, b_ref, pick_ref, o_ref, buf, xs, sem):
        i = pl.program_id(0)
        nv = nv_ref[0]
        slot = i % 2

        def start_rows(src_idx, s):
            def issue(r8, c):
                for u in range(GATHER_UNROLL):
                    r = r8 * GATHER_UNROLL + u
                    _row_copy(x_hbm, src_idx[0, 0, r], buf.at[s], r, sem.at[s], chunks).start(priority=u % 2)
                return c
            lax.fori_loop(0, tm // GATHER_UNROLL, issue, 0)

        @pl.when(i < nv)
        def _():
            @pl.when(i == 0)
            def _():
                start_rows(idx_ref, 0)

            def wait(r8, c):
                for u in range(GATHER_UNROLL):
                    _row_copy(x_hbm, 0, buf.at[slot], r8 * GATHER_UNROLL + u, sem.at[slot], chunks).wait()
                return c
            lax.fori_loop(0, tm // GATHER_UNROLL, wait, 0)

            @pl.when(i + 1 < nv)
            def _():
                start_rows(nxt_ref, 1 - slot)

            for c in range(chunks):
                xs[:, c * LANES:(c + 1) * LANES] = buf[slot, pl.ds(c, tm, stride=chunks), :].astype(BF16)

            for j in range(f // tf):
                h = _dot(xs[...], w_ref[0, 0, :, 2 * j * tf:2 * (j + 1) * tf]) + b_ref[0, :, 2 * j * tf:2 * (j + 1) * tf]
                g = jnp.minimum(h, SWIGLU_LIMIT)
                glu = g * _sigmoid(SWIGLU_ALPHA * g)
                lin = jnp.clip(h, -SWIGLU_LIMIT, SWIGLU_LIMIT) + 1.0
                pairs = [glu[:, s * LANES:(s + 1) * LANES] * pltpu.roll(lin[:, s * LANES:(s + 1) * LANES], LANES - 1, 1)
                         for s in range(2 * tf // LANES)]
                o_ref[:, j * tf:(j + 1) * tf] = _dot(jnp.concatenate(pairs, axis=1).astype(BF16), pick_ref[...]).astype(BF16)

        @pl.when(i >= nv)
        def _():
            o_ref[...] = jnp.zeros(o_ref.shape, BF16)

    idx_spec = pl.BlockSpec((1, 1, tm), lambda i, be, nv: (i, 0, 0), memory_space=pltpu.SMEM)
    nxt_spec = pl.BlockSpec((1, 1, tm), lambda i, be, nv: (jnp.minimum(i + 1, n_blk - 1), 0, 0), memory_space=pltpu.SMEM)
    return _pcall(
        body, name="moe_up", grid=(n_blk,), prefetch=2,
        in_specs=[idx_spec, nxt_spec, pl.BlockSpec(memory_space=pl.ANY),
                  pl.BlockSpec((1, 1, d, 2 * f), lambda i, be, nv: (layer, be[i], 0, 0)),
                  pl.BlockSpec((1, 1, 2 * f), lambda i, be, nv: (be[i], 0, 0)),
                  pl.BlockSpec((2 * tf, tf), lambda i, be, nv: (0, 0))],
        out_specs=pl.BlockSpec((tm, f), lambda i, be, nv: (i, 0)),
        out_shape=jax.ShapeDtypeStruct((n_rows, f), BF16),
        scratch=[pltpu.VMEM((2, tm * chunks, LANES), F32), pltpu.VMEM((tm, d), BF16), pltpu.SemaphoreType.DMA((2,))],
    )(blk_e, n_valid, idx3, idx3, x_chunks, w_up, b_up, pick_even)


def _moe_down(blk_e, fresh, n_valid, a, w_down, b_down, *, layer, n_rows, d, f, tm):
    chunks = d // LANES

    def body(be_ref, fr_ref, nv_ref, a_ref, w_ref, b_ref, o_ref, wb_ref):
        i = pl.program_id(0)

        @pl.when(fr_ref[i] == 1)
        def _():
            wb_ref[...] = w_ref[0, 0].astype(BF16)

        @pl.when(i < nv_ref[0])
        def _():
            res = _dot(a_ref[...], wb_ref[...]) + b_ref[0]
            for c in range(chunks):
                o_ref[pl.ds(c, tm, stride=chunks), :] = res[:, c * LANES:(c + 1) * LANES]

        @pl.when(i >= nv_ref[0])
        def _():
            o_ref[...] = jnp.zeros(o_ref.shape, F32)

    return _pcall(
        body, name="moe_down", grid=(n_rows // tm,), prefetch=3,
        in_specs=[pl.BlockSpec((tm, f), lambda i, be, fr, nv: (i, 0)),
                  pl.BlockSpec((1, 1, f, d), lambda i, be, fr, nv: (layer, be[i], 0, 0)),
                  pl.BlockSpec((1, 1, d), lambda i, be, fr, nv: (be[i], 0, 0))],
        out_specs=pl.BlockSpec((tm * chunks, LANES), lambda i, be, fr, nv: (i, 0)),
        out_shape=jax.ShapeDtypeStruct((n_rows * chunks, LANES), F32),
        scratch=[pltpu.VMEM((f, d), BF16)],
    )(blk_e, fresh, n_valid, a, w_down, b_down)


def _moe_combine(pos3, ys, gates, x, ln_g, ln_b, *, n, d, tq, alpha):
    chunks = d // LANES
    n_tiles = n // tq

    def body(pos_ref, nxt_ref, y_hbm, g_ref, x_ref, lg_ref, lb_ref, of_ref, ob_ref, buf, y_ref, sem):
        i = pl.program_id(0)
        slot = i % 2

        def start_rows(src_pos, s):
            def issue(r, c):
                for kk in range(TOP_K):
                    _row_copy(y_hbm, src_pos[0, 0, kk * tq + r], buf.at[s, kk], r, sem.at[s], chunks).start(priority=kk % 2)
                return c
            lax.fori_loop(0, tq, issue, 0)

        @pl.when(i == 0)
        def _():
            start_rows(pos_ref, 0)

        def wait(r, c):
            for kk in range(TOP_K):
                _row_copy(y_hbm, 0, buf.at[slot, kk], r, sem.at[slot], chunks).wait()
            return c
        lax.fori_loop(0, tq, wait, 0)

        @pl.when(i + 1 < n_tiles)
        def _():
            start_rows(nxt_ref, 1 - slot)

        gv = g_ref[...]
        for c in range(chunks):
            piece = gv[:, 0:1] * buf[slot, 0, pl.ds(c, tq, stride=chunks), :]
            for kk in range(1, TOP_K):
                piece = piece + gv[:, kk:kk + 1] * buf[slot, kk, pl.ds(c, tq, stride=chunks), :]
            y_ref[:, c * LANES:(c + 1) * LANES] = piece
        out = _layer_norm(alpha * x_ref[...] + y_ref[...], lg_ref[...], lb_ref[...])
        of_ref[...] = out
        ob_ref[...] = out.astype(BF16)

    return _pcall(
        body, name="moe_combine_ln", grid=(n_tiles,),
        in_specs=[pl.BlockSpec((1, 1, TOP_K * tq), lambda i: (i, 0, 0), memory_space=pltpu.SMEM),
                  pl.BlockSpec((1, 1, TOP_K * tq), lambda i: (jnp.minimum(i + 1, n_tiles - 1), 0, 0), memory_space=pltpu.SMEM),
                  pl.BlockSpec(memory_space=pl.ANY),
                  pl.BlockSpec((tq, LANES), lambda i: (i, 0)),
                  pl.BlockSpec((tq, d), lambda i: (i, 0)),
                  pl.BlockSpec((1, d), lambda i: (0, 0)),
                  pl.BlockSpec((1, d), lambda i: (0, 0))],
        out_specs=[pl.BlockSpec((tq, d), lambda i: (i, 0)), pl.BlockSpec((tq, d), lambda i: (i, 0))],
        out_shape=[jax.ShapeDtypeStruct((n, d), F32), jax.ShapeDtypeStruct((n, d), BF16)],
        scratch=[pltpu.VMEM((2, TOP_K, tq * chunks, LANES), F32), pltpu.VMEM((tq, d), F32), pltpu.SemaphoreType.DMA((2,))],
    )(pos3, pos3, ys, gates, x, ln_g, ln_b)


def _t5_bucket(dist):
    max_exact = REL_BUCKETS // 2
    dd = jnp.maximum(dist, 1).astype(F32)
    large = max_exact + (jnp.log(dd / max_exact) / math.log(REL_MAX_DIST / max_exact)
                         * (REL_BUCKETS - max_exact)).astype(I32)
    large = jnp.minimum(large, REL_BUCKETS - 1)
    return jnp.where(dist < max_exact, dist, large)


def _tables(positions, rel_bias, t):
    half = MLA_ROPE // 2
    inv = 1.0 / (ROPE_THETA ** (jnp.arange(0, MLA_ROPE, 2, dtype=F32) / MLA_ROPE))
    ang = positions.astype(F32)[:, None] * inv[None, :]
    zeros = jnp.zeros((t, LANES - MLA_ROPE), F32)
    cos = jnp.concatenate([jnp.cos(ang), jnp.cos(ang), zeros], axis=-1)
    sin = jnp.concatenate([jnp.sin(ang), jnp.sin(ang), zeros], axis=-1)
    assert half * 2 == MLA_ROPE

    first = jnp.sum(_t5_bucket(jnp.arange(t))[None, :] < jnp.arange(REL_BUCKETS)[:, None], axis=1).astype(I32)
    rb = rel_bias.astype(F32)

    def bias_of(dist, valid):
        out = jnp.broadcast_to(rb[0].reshape((NSA_HEADS,) + (1,) * dist.ndim), (NSA_HEADS,) + dist.shape)
        for bkt in range(1, REL_BUCKETS):
            out = jnp.where((dist >= first[bkt])[None], rb[bkt].reshape((NSA_HEADS,) + (1,) * dist.ndim), out)
        return jnp.where(valid[None], out, NEG_INF)

    tb = NSA_TILE
    sat = rb[REL_BUCKETS - 1]
    cvec = jnp.broadcast_to(sat.reshape(NSA_GROUPS, NSA_HPG, 1), (NSA_GROUPS, NSA_HPG, 4 * tb))
    span = WINDOW + tb
    wd = (jnp.arange(WINDOW // tb + 1, dtype=I32) * tb)[:, None, None] + jnp.arange(tb, dtype=I32)[None, :, None] \
        - jnp.arange(span, dtype=I32)[None, None, :]
    wtab = bias_of(wd, (wd >= 0) & (wd < WINDOW)).transpose(1, 0, 2, 3)
    stab = bias_of(wd, wd >= 0).transpose(1, 0, 2, 3)
    nb = t // CMP_STRIDE
    dist_c = jnp.arange(t, dtype=I32)[:, None] - (jnp.arange(nb, dtype=I32) * CMP_STRIDE + CMP_LEN - 1)[None, :]
    bias_c = bias_of(dist_c, dist_c >= 0)
    tq_mla = _pick_tile(t, 512)
    cm = jnp.arange(tq_mla, dtype=I32)
    causal = jnp.where(cm[None, :] <= cm[:, None], 0.0, NEG_INF).astype(F32)

    n_slc = t // SEL_BLOCK
    tok = np.arange(nb)[:, None] * CMP_STRIDE + np.arange(CMP_LEN)[None, :]
    cover = (tok[:, :, None] // SEL_BLOCK == np.arange(LANES)[None, None, :]).sum(1) / CMP_LEN
    cover[nb - 1] = 0.0
    cover[:, n_slc:] = 0.0
    expand_t = (np.arange(t)[:, None] // SEL_BLOCK == np.arange(LANES)[None, :])
    return dict(cos=cos, sin=sin, stab=stab, cvec=cvec, wtab=wtab, bias_c=bias_c, causal=causal,
                cover=jnp.asarray(cover, BF16), expand_t=jnp.asarray(expand_t, BF16))


def _pad_heads(w, heads, width):
    k = w.shape[0]
    w = w.reshape(k, heads, width)
    return jnp.pad(w, ((0, 0), (0, 0), (0, HEAD_PAD - width))).reshape(k, heads * HEAD_PAD)


def _pick_tile(total, want):
    tile = min(total, want)
    while total % tile:
        tile //= 2
    return tile


def _token_mixer(x, xb, tabs, p, *, b, t, d):
    n = b * t
    tm = _pick_tile(t, 512)
    tpb = t // tm
    offs = np.cumsum([0, MLA_Q_LORA, MLA_KV_LORA, MLA_ROPE, NSA_HEADS * NSA_DK,
                      NSA_GROUPS * NSA_DK, NSA_GROUPS * NSA_DV, NSA_GROUPS * NSA_DK, NSA_GROUPS * NSA_DV,
                      NSA_GROUPS * NSA_DK, NSA_GROUPS * NSA_DV, NSA_HEADS * NSA_BRANCHES, 2 * d])
    w_in = p["w_in"]
    seg = [w_in[:, offs[s]:offs[s + 1]] for s in range(12)]
    x_lhs = [(xb, _row_spec(tm, d))]
    mla_scale = (MLA_NOPE + MLA_ROPE) ** -0.5
    nsa_scale = NSA_DK ** -0.5

    def cast_epi(acc, er, orr):
        orr[0][...] = acc.astype(BF16)

    lat_w = MLA_Q_LORA + MLA_KV_LORA
    w_lat = jnp.concatenate([seg[0], seg[1]], axis=1).astype(BF16)
    tn_lat = _pick_tile(lat_w, 1024)
    (lat,) = _matmul("proj_latents", x_lhs, w_lat, [], [(jax.ShapeDtypeStruct((n, lat_w), BF16), _tile_spec(tm, tn_lat))],
                     _cast_prologue, cast_epi, m=n, k=d, n=lat_w, tm=tm, tn=tn_lat)

    w_small = jnp.concatenate([seg[2], jnp.zeros((d, LANES - MLA_ROPE), F32), seg[10],
                               jnp.zeros((d, LANES - NSA_HEADS * NSA_BRANCHES), F32)], axis=1).astype(BF16)

    def small_epi(acc, er, orr):
        orr[0][...] = _rope_slab(acc[:, :LANES], er[0][...], er[1][...]).astype(BF16)
        orr[1][...] = _sigmoid(acc[:, LANES:])

    cs_extras = [(tabs["cos_n"], _row_spec(tm, LANES)), (tabs["sin_n"], _row_spec(tm, LANES))]
    kr, nsa_gates = _matmul(
        "proj_rope_key_gates", x_lhs, w_small, cs_extras,
        [(jax.ShapeDtypeStruct((n, LANES), BF16), _row_spec(tm, LANES)),
         (jax.ShapeDtypeStruct((n, LANES), F32), _row_spec(tm, LANES))],
        _cast_prologue, small_epi, m=n, k=d, n=2 * LANES, tm=tm, tn=2 * LANES)

    hb = 2
    hq = 4
    w_qn = _pad_heads(seg[3], NSA_HEADS, NSA_DK).astype(BF16)
    (q_nsa,) = _matmul("proj_nsa_q", x_lhs, w_qn, [],
                       [(jax.ShapeDtypeStruct((b, NSA_HEADS, t, HEAD_PAD), BF16), _head_spec(tm, hq, HEAD_PAD, tpb))],
                       _cast_prologue, _heads_epilogue(hq, HEAD_PAD, nsa_scale),
                       m=n, k=d, n=NSA_HEADS * HEAD_PAD, tm=tm, tn=hq * HEAD_PAD)

    w_k6 = jnp.concatenate([_pad_heads(seg[s], NSA_GROUPS, NSA_DK) for s in (4, 6, 8)], axis=1).astype(BF16)
    (k6,) = _matmul("proj_nsa_k", x_lhs, w_k6, [],
                    [(jax.ShapeDtypeStruct((b, 3 * NSA_GROUPS, t, HEAD_PAD), BF16), _head_spec(tm, hb, HEAD_PAD, tpb))],
                    _cast_prologue, _heads_epilogue(hb, HEAD_PAD),
                    m=n, k=d, n=3 * NSA_GROUPS * HEAD_PAD, tm=tm, tn=hb * HEAD_PAD)

    (v_c,) = _matmul("proj_nsa_v_cmp", x_lhs, seg[5].astype(BF16), [],
                     [(jax.ShapeDtypeStruct((b, NSA_GROUPS, t, NSA_DV), BF16), _head_spec(tm, hb, NSA_DV, tpb))],
                     _cast_prologue, _heads_epilogue(hb, NSA_DV),
                     m=n, k=d, n=NSA_GROUPS * NSA_DV, tm=tm, tn=hb * NSA_DV)

    def ones_lane(rows):
        return jnp.where(lax.broadcasted_iota(I32, (rows, HEAD_PAD - NSA_DV), 1) == 0, 1.0, 0.0).astype(BF16)

    def v_ones_epi(acc, er, orr):
        for c in range(hb):
            orr[0][0, c, :, :NSA_DV] = acc[:, c * NSA_DV:(c + 1) * NSA_DV].astype(BF16)
            orr[0][0, c, :, NSA_DV:] = ones_lane(acc.shape[0])

    w_v4 = jnp.concatenate([seg[7], seg[9]], axis=1).astype(BF16)
    (v4,) = _matmul("proj_nsa_v", x_lhs, w_v4, [],
                    [(jax.ShapeDtypeStruct((b, 2 * NSA_GROUPS, t, HEAD_PAD), BF16), _head_spec(tm, hb, HEAD_PAD, tpb))],
                    _cast_prologue, v_ones_epi, m=n, k=d, n=2 * NSA_GROUPS * NSA_DV, tm=tm, tn=hb * NSA_DV)

    def sig_epi(acc, er, orr):
        orr[0][...] = _sigmoid(acc).astype(BF16)

    tn_d = _pick_tile(d, 1024)
    (merge,) = _matmul("proj_merge_gates", x_lhs, seg[11].astype(BF16), [],
                       [(jax.ShapeDtypeStruct((n, 2 * d), BF16), _tile_spec(tm, tn_d))],
                       _cast_prologue, sig_epi, m=n, k=d, n=2 * d, tm=tm, tn=tn_d)

    w_uq = p["mla_w_uq"].reshape(MLA_Q_LORA, MLA_HEADS, MLA_NOPE + MLA_ROPE)
    w_uq = jnp.pad(w_uq, ((0, 0), (0, 0), (0, HEAD_PAD - MLA_NOPE - MLA_ROPE))).reshape(MLA_Q_LORA, MLA_HEADS * HEAD_PAD)

    def q_epi(acc, er, orr):
        cos, sin = er[0][...], er[1][...]
        for c in range(hq):
            orr[0][0, c, :, :MLA_NOPE] = (acc[:, c * HEAD_PAD:c * HEAD_PAD + MLA_NOPE] * mla_scale).astype(BF16)
            slab = acc[:, c * HEAD_PAD + MLA_NOPE:(c + 1) * HEAD_PAD] * mla_scale
            orr[0][0, c, :, MLA_NOPE:] = _rope_slab(slab, cos, sin).astype(BF16)

    (q_mla,) = _matmul(
        "mla_q_up", [(lat, _row_spec(tm, MLA_Q_LORA)), (p["mla_q_norm"].reshape(1, -1), pl.BlockSpec((1, MLA_Q_LORA), lambda i, j: (0, 0)))],
        w_uq.astype(BF16), cs_extras,
        [(jax.ShapeDtypeStruct((b, MLA_HEADS, t, HEAD_PAD), BF16), _head_spec(tm, hq, HEAD_PAD, tpb))],
        _rms_prologue, q_epi, m=n, k=MLA_Q_LORA, n=MLA_HEADS * HEAD_PAD, tm=tm, tn=hq * HEAD_PAD)

    kv_w = MLA_NOPE + MLA_V

    def kv_epi(acc, er, orr):
        k_rope = er[0][...]
        for c in range(hq):
            orr[0][0, c, :, :MLA_NOPE] = acc[:, c * kv_w:c * kv_w + MLA_NOPE].astype(BF16)
            orr[0][0, c, :, MLA_NOPE:] = k_rope
            orr[1][0, c, :, :MLA_V] = acc[:, c * kv_w + MLA_NOPE:(c + 1) * kv_w].astype(BF16)
            orr[1][0, c, :, MLA_V:] = ones_lane(acc.shape[0])

    k_mla, v_mla = _matmul(
        "mla_kv_up", [(lat, _row_spec(tm, MLA_KV_LORA, MLA_Q_LORA // MLA_KV_LORA)),
                      (p["mla_kv_norm"].reshape(1, -1), pl.BlockSpec((1, MLA_KV_LORA), lambda i, j: (0, 0)))],
        p["mla_w_ukv"].astype(BF16), [(kr, _row_spec(tm, LANES))],
        [(jax.ShapeDtypeStruct((b, MLA_HEADS, t, HEAD_PAD), BF16), _head_spec(tm, hq, HEAD_PAD, tpb)),
         (jax.ShapeDtypeStruct((b, MLA_HEADS, t, HEAD_PAD), BF16), _head_spec(tm, hq, HEAD_PAD, tpb))],
        _rms_prologue, kv_epi, m=n, k=MLA_KV_LORA, n=MLA_HEADS * kv_w, tm=tm, tn=hq * kv_w)

    attn_a = _mla_attention(q_mla, k_mla, v_mla, tabs["causal"], b=b, t=t, tq=tabs["causal"].shape[0])

    def ya_epi(acc, er, orr):
        orr[0][...] = (er[0][...].astype(F32) * acc).astype(BF16)

    (y_a,) = _matmul("mla_branch_out", [(attn_a.reshape(n, MLA_HEADS * MLA_V), _row_spec(tm, MLA_HEADS * MLA_V))],
                     p["w_branch_a"].astype(BF16), [(merge, _tile_spec(tm, tn_d))],
                     [(jax.ShapeDtypeStruct((n, d), BF16), _tile_spec(tm, tn_d))],
                     _cast_prologue, ya_epi, m=n, k=MLA_HEADS * MLA_V, n=d, tm=tm, tn=tn_d)

    def cmp_weights(pe, w1, w2, width_in, width):
        pe_p = jnp.pad(pe, ((0, 0), (0, width - width_in)))
        pe2 = jnp.zeros((8, CMP_STRIDE * width), F32).at[0].set(pe_p[:CMP_STRIDE].reshape(-1)).at[1].set(pe_p[CMP_STRIDE:].reshape(-1))
        w1p = jnp.pad(w1.reshape(CMP_LEN, width_in, width_in), ((0, 0), (0, width - width_in), (0, width - width_in)))
        w2p = jnp.pad(w2, ((0, width - width_in), (0, width - width_in)))
        return (pe2.astype(BF16), w1p[:CMP_STRIDE].reshape(-1, width).astype(BF16),
                w1p[CMP_STRIDE:].reshape(-1, width).astype(BF16), w2p.astype(BF16))

    kcmp = _compress(k6, *cmp_weights(p["cmp_pe_k"], p["cmp_w1_k"], p["cmp_w2_k"], NSA_DK, HEAD_PAD), b=b, t=t, width=HEAD_PAD)
    vcmp = _compress(v_c, *cmp_weights(p["cmp_pe_v"], p["cmp_w1_v"], p["cmp_w2_v"], NSA_DV, NSA_DV), b=b, t=t, width=NSA_DV)
    gates3 = nsa_gates.reshape(b, t, LANES)
    o_cmp, sel = _nsa_cmp(q_nsa, kcmp, vcmp, tabs["bias_c"], tabs["cover"], gates3, b=b, t=t)
    o_slc = _nsa_sel(q_nsa, k6, v4, tabs["stab"], tabs["cvec"], gates3, sel, tabs["expand_t"], b=b, t=t)
    o_win = _nsa_win(q_nsa, k6, v4, tabs["wtab"], gates3, b=b, t=t)

    def sum3_prologue(a_ref, b_ref, c_ref):
        return (a_ref[...].astype(F32) + b_ref[...].astype(F32) + c_ref[...].astype(F32)).astype(BF16)

    def merge_epi(acc, er, orr):
        orr[0][...] = (er[0][...].astype(F32) + er[1][...].astype(F32) * acc).astype(BF16)

    hv = NSA_HEADS * NSA_DV
    (merged,) = _matmul("nsa_branch_out_merge", [(o.reshape(n, hv), _row_spec(tm, hv)) for o in (o_cmp, o_slc, o_win)],
                        p["w_branch_b"].astype(BF16), [(y_a, _tile_spec(tm, tn_d)), (merge, _tile_spec(tm, tn_d, d // tn_d))],
                        [(jax.ShapeDtypeStruct((n, d), BF16), _tile_spec(tm, tn_d))],
                        sum3_prologue, merge_epi, m=n, k=hv, n=d, tm=tm, tn=tn_d)

    tm_ln = _pick_tile(t, 256)
    alpha = p["alpha"]

    chunks = d // LANES

    def ln_epi(acc, er, orr):
        out = _layer_norm(alpha * er[0][...] + acc, er[1][...], er[2][...])
        orr[0][...] = out
        for c in range(chunks):
            orr[1][pl.ds(c, tm_ln, stride=chunks), :] = out[:, c * LANES:(c + 1) * LANES]

    vec_spec = pl.BlockSpec((1, d), lambda i, j: (0, 0))
    x1, x1_chunks = _matmul(
        "mixer_out_ln", [(merged, _row_spec(tm_ln, d))], p["w_out"].astype(BF16),
        [(x, _row_spec(tm_ln, d)), (p["ln1_g"].reshape(1, d), vec_spec), (p["ln1_b"].reshape(1, d), vec_spec)],
        [(jax.ShapeDtypeStruct((n, d), F32), _row_spec(tm_ln, d)),
         (jax.ShapeDtypeStruct((n * chunks, LANES), F32), pl.BlockSpec((tm_ln * chunks, LANES), lambda i, j: (i, 0)))],
        _cast_prologue, ln_epi, m=n, k=d, n=d, tm=tm_ln, tn=d)
    return x1, x1_chunks


def _moe(x, x_chunks, p, *, n, d):
    _, n_exp, _, f2 = p["exp_w_up"].shape
    f = f2 // 2
    tm_e = p["tm_e"]
    rw = jnp.pad(p["router_w"], ((0, 0), (0, LANES - n_exp)))
    rw_hi = rw.astype(BF16)
    rw_lo = (rw - rw_hi.astype(F32)).astype(BF16)
    rb = jnp.pad(p["router_b"], (0, LANES - n_exp)).reshape(1, LANES)
    top_e, gates = _router(x, rw_hi, rw_lo, rb, n=n, d=d, n_exp=n_exp, tm=_pick_tile(n, 512))

    n_pairs = n * TOP_K
    e_flat = top_e[:, :TOP_K].reshape(-1)
    grp = LANES
    onehot = (e_flat[:, None] == jnp.arange(n_exp, dtype=I32)[None, :]).astype(F32).reshape(n_pairs // grp, grp, n_exp)
    tri = jnp.tril(jnp.ones((grp, grp), F32))
    within = jnp.einsum("ij,gje->gie", tri, onehot)
    totals = within[:, -1, :]
    before = jnp.cumsum(totals, axis=0) - totals
    rank = (jnp.sum((within + before[:, None, :]) * onehot, axis=-1).reshape(-1) - 1.0).astype(I32)
    counts = jnp.sum(totals, axis=0).astype(I32)
    padded = (counts + tm_e - 1) // tm_e * tm_e
    pad_end = jnp.cumsum(padded)
    dest = (pad_end - padded)[e_flat] + rank
    n_rows = n_pairs + n_exp * tm_e
    n_blk = n_rows // tm_e
    row_tok = jnp.zeros((n_rows,), I32).at[dest].set(jnp.arange(n_pairs, dtype=I32) // TOP_K)
    n_valid = (pad_end[-1] // tm_e).astype(I32)
    blk_e = jnp.sum(pad_end[None, :] <= (jnp.arange(n_blk, dtype=I32) * tm_e)[:, None], axis=1).astype(I32)
    blk_e = jnp.minimum(blk_e, blk_e[jnp.maximum(n_valid - 1, 0)])
    n_valid = n_valid.reshape(1)

    tf = _pick_tile(f, 512)
    pick_even = jnp.asarray(np.arange(2 * tf)[:, None] == 2 * np.arange(tf)[None, :], BF16)
    a = _moe_up(blk_e, n_valid, x_chunks, row_tok.reshape(n_blk, 1, tm_e),
                p["exp_w_up_bf16"], p["exp_b_up"].reshape(n_exp, 1, 2 * f), pick_even,
                layer=p["layer"], n_rows=n_rows, d=d, f=f, tm=tm_e, tf=tf)
    split = max(1, tm_e // 256)
    tm_d = tm_e // split
    blk_d = jnp.repeat(blk_e, split)
    fresh = jnp.concatenate([jnp.ones((1,), I32), (blk_d[1:] != blk_d[:-1]).astype(I32)])
    ys = _moe_down(blk_d, fresh, n_valid * split, a, p["exp_w_down"], p["exp_b_down"].reshape(n_exp, 1, d),
                   layer=p["layer"], n_rows=n_rows, d=d, f=f, tm=tm_d)

    tq = _pick_tile(n, 128)
    pos3 = dest.reshape(n // tq, tq, TOP_K).transpose(0, 2, 1).reshape(n // tq, 1, TOP_K * tq)
    return _moe_combine(pos3, ys, gates, x, p["ln2_g"].reshape(1, d), p["ln2_b"].reshape(1, d),
                        n=n, d=d, tq=tq, alpha=p["alpha"])


def kernel(x, positions, rel_bias, w_in, mla_q_norm, mla_w_uq, mla_kv_norm, mla_w_ukv, cmp_pe_k, cmp_w1_k, cmp_w2_k, cmp_pe_v, cmp_w1_v, cmp_w2_v, w_branch_a, w_branch_b, w_out, ln1_g, ln1_b, router_w, router_b, exp_w_up, exp_b_up, exp_w_down, exp_b_down, ln2_g, ln2_b):
    b, t, d = x.shape
    depth = w_in.shape[0]
    n = b * t
    alpha = (2 * depth) ** 0.25
    tabs = _tables(positions, rel_bias, t)
    tabs["cos_n"] = jnp.tile(tabs["cos"], (b, 1))
    tabs["sin_n"] = jnp.tile(tabs["sin"], (b, 1))
    xf = x.reshape(n, d)
    xb = xf.astype(BF16)
    exp_w_up_bf16 = exp_w_up.astype(BF16)
    for l in range(depth):
        p = dict(w_in=w_in[l], mla_q_norm=mla_q_norm[l], mla_w_uq=mla_w_uq[l], mla_kv_norm=mla_kv_norm[l],
                 mla_w_ukv=mla_w_ukv[l], cmp_pe_k=cmp_pe_k[l], cmp_w1_k=cmp_w1_k[l], cmp_w2_k=cmp_w2_k[l],
                 cmp_pe_v=cmp_pe_v[l], cmp_w1_v=cmp_w1_v[l], cmp_w2_v=cmp_w2_v[l], w_branch_a=w_branch_a[l],
                 w_branch_b=w_branch_b[l], w_out=w_out[l], ln1_g=ln1_g[l], ln1_b=ln1_b[l], router_w=router_w[l],
                 router_b=router_b[l], exp_w_up=exp_w_up, exp_w_up_bf16=exp_w_up_bf16, exp_b_up=exp_b_up[l], exp_w_down=exp_w_down, layer=l,
                 exp_b_down=exp_b_down[l], ln2_g=ln2_g[l], ln2_b=ln2_b[l], alpha=alpha, tm_e=_pick_tile(n * TOP_K, 512))
        x1, x1_chunks = _token_mixer(xf, xb, tabs, p, b=b, t=t, d=d)
        xf, xb = _moe(x1, x1_chunks, p, n=n, d=d)
    return xf.reshape(b, t, d)
```

```python
import functools
import math

import numpy as np
import jax
import jax.numpy as jnp
from jax import lax
from jax.experimental import pallas as pl
from jax.experimental.pallas import tpu as pltpu

F32, BF16, I32 = jnp.float32, jnp.bfloat16, jnp.int32

MLA_HEADS = 16
MLA_Q_LORA = 1536
MLA_KV_LORA = 512
MLA_NOPE = 128
MLA_ROPE = 64
MLA_V = 128
ROPE_THETA = 10000.0
NSA_HEADS = 16
NSA_GROUPS = 2
NSA_HPG = NSA_HEADS // NSA_GROUPS
NSA_DK = 192
NSA_DV = 128
CMP_LEN = 32
CMP_STRIDE = 16
SEL_BLOCK = 64
SEL_COUNT = 16
WINDOW = 512
NSA_BRANCHES = 3
SEL_FORCE = 1.0e4
REL_BUCKETS = 32
REL_MAX_DIST = 128
TOP_K = 4
SWIGLU_LIMIT = 7.0
SWIGLU_ALPHA = 1.702
NEG_INF = -1.0e30
LN_EPS = 1e-5
RMS_EPS = 1e-6

LANES = 128
HEAD_PAD = 256
NSA_TILE = 128
GATHER_UNROLL = 8
MLA_TRIP_WIDTHS = (4, 2, 1)
VMEM_LIMIT_BYTES = 56 * 1024 * 1024


def _pcall(body, *, name, grid, in_specs, out_specs, out_shape, scratch=(), prefetch=0):
    gs = pltpu.PrefetchScalarGridSpec(num_scalar_prefetch=prefetch, grid=grid, in_specs=in_specs,
                                      out_specs=out_specs, scratch_shapes=list(scratch))
    return pl.pallas_call(
        body, grid_spec=gs, out_shape=out_shape, name=name,
        compiler_params=pltpu.CompilerParams(dimension_semantics=("arbitrary",) * len(grid),
                                             vmem_limit_bytes=VMEM_LIMIT_BYTES))


def _sigmoid(x):
    return 1.0 / (1.0 + jnp.exp(-x))


def _dot_t(a, b):
    return lax.dot_general(a, b, (((1,), (1,)), ((), ())), preferred_element_type=F32)


def _dot(a, b):
    return jnp.dot(a, b, preferred_element_type=F32)


def _layer_norm(r, g, b):
    mu = jnp.mean(r, axis=-1, keepdims=True)
    c = r - mu
    var = jnp.mean(c * c, axis=-1, keepdims=True)
    return c * lax.rsqrt(var + LN_EPS) * g + b


def _matmul(name, lhs, w, extras, outs, prologue, epilogue, *, m, k, n, tm, tn):
    nl, ne, no = len(lhs), len(extras), len(outs)

    def body(*refs):
        lr = refs[:nl]
        wr = refs[nl]
        er = refs[nl + 1:nl + 1 + ne]
        orr = refs[nl + 1 + ne:nl + 1 + ne + no]
        xs = refs[-1]

        @pl.when(pl.program_id(1) == 0)
        def _():
            xs[...] = prologue(*lr)

        epilogue(_dot(xs[...], wr[...]), er, orr)

    return _pcall(
        body, name=name, grid=(m // tm, n // tn),
        in_specs=[s for _, s in lhs] + [pl.BlockSpec((k, tn), lambda i, j: (0, j))] + [s for _, s in extras],
        out_specs=[s for _, s in outs], out_shape=[o for o, _ in outs],
        scratch=[pltpu.VMEM((tm, k), BF16)],
    )(*[a for a, _ in lhs], w, *[a for a, _ in extras])


def _row_spec(tm, width, col=0):
    return pl.BlockSpec((tm, width), lambda i, j: (i, col))


def _tile_spec(tm, tn, col_off=0):
    return pl.BlockSpec((tm, tn), lambda i, j: (i, j + col_off))


def _head_spec(tm, hb, width, tpb):
    return pl.BlockSpec((1, hb, tm, width), lambda i, j: (i // tpb, j, i % tpb, 0))


def _cast_prologue(x_ref):
    return x_ref[...].astype(BF16)


def _rms_prologue(x_ref, g_ref):
    xf = x_ref[...].astype(F32)
    ms = jnp.mean(xf * xf, axis=-1, keepdims=True)
    return (xf * lax.rsqrt(ms + RMS_EPS) * g_ref[...]).astype(BF16)


def _rope_slab(t, cos, sin):
    lane = lax.broadcasted_iota(I32, t.shape, 1)
    rot = jnp.where(lane < MLA_ROPE // 2, -pltpu.roll(t, LANES - MLA_ROPE // 2, 1), pltpu.roll(t, MLA_ROPE // 2, 1))
    return t * cos + rot * sin


def _heads_epilogue(hb, width, scale=None):
    def epi(acc, er, orr):
        for c in range(hb):
            v = acc[:, c * width:(c + 1) * width]
            if scale is not None:
                v = v * scale
            orr[0][0, c] = v.astype(BF16)
    return epi


def _softmax_init(rows, width):
    return (jnp.full((rows, 1), NEG_INF, F32), jnp.zeros((rows, 1), F32), jnp.zeros((rows, width), F32))


def _softmax_step(s, v, carry):
    m, l, acc = carry
    m_new = jnp.maximum(m, jnp.max(s, axis=-1, keepdims=True))
    a = jnp.exp(m - m_new)
    p = jnp.exp(s - m_new)
    return m_new, a * l + jnp.sum(p, axis=-1, keepdims=True), a * acc + _dot(p.astype(BF16), v)


def _mla_attention(q, k, v_ones, causal, *, b, t, tq):
    nq = t // tq

    def body(q_ref, k_ref, v_ref, c_ref, o_ref):
        i = pl.program_id(2)
        qv = q_ref[0, 0]

        def step(koff, carry, diagonal):
            s = _dot_t(qv, k_ref[0, 0, pl.ds(koff, tq), :])
            if diagonal:
                s = s + c_ref[...]
            return _softmax_step(s, v_ref[0, 0, pl.ds(koff, tq), :], carry)

        carry = _softmax_init(tq, MLA_V)
        done = 0
        for width in MLA_TRIP_WIDTHS:
            def group(j, c, width=width, done=done):
                for u in range(width):
                    c = step(pl.multiple_of((done + width * j + u) * tq, tq), c, False)
                return c
            n_grp = (i - done) // width
            carry = lax.fori_loop(0, n_grp, group, carry)
            done = done + n_grp * width
        _, l, acc = step(pl.multiple_of(i * tq, tq), carry, True)
        o_ref[0] = (acc / l).astype(BF16)

    return _pcall(
        body, name="mla_attention", grid=(b, MLA_HEADS, nq),
        in_specs=[pl.BlockSpec((1, 1, tq, HEAD_PAD), lambda bi, h, i: (bi, h, i, 0)),
                  pl.BlockSpec((1, 1, t, HEAD_PAD), lambda bi, h, i: (bi, h, 0, 0)),
                  pl.BlockSpec((1, 1, t, MLA_V), lambda bi, h, i: (bi, h, 0, 0)),
                  pl.BlockSpec((tq, tq), lambda bi, h, i: (0, 0))],
        out_specs=pl.BlockSpec((1, tq, MLA_V), lambda bi, h, i: (bi, i, h)),
        out_shape=jax.ShapeDtypeStruct((b, t, MLA_HEADS * MLA_V), BF16),
    )(q, k, v_ones, causal)


def _gelu_tanh(x):
    return 0.5 * x * (1.0 + jnp.tanh(math.sqrt(2.0 / math.pi) * (x + 0.044715 * (x * x * x))))


def _compress(kv6, pe2, w1lo, w1hi, w2, *, b, t, width):
    nb = t // CMP_STRIDE
    half = CMP_STRIDE * width
    h = kv6.reshape(b, kv6.shape[1], nb, half)

    def body(h_ref, pe_ref, lo_ref, hi_ref, w2_ref, o_ref):
        hv = h_ref[0, 0]
        lo, hi = lo_ref[...], hi_ref[...]
        a = _dot(hv, lo)
        bb = _dot(hv, hi)
        pe = pe_ref[...]
        c = _dot(pe, lo)[0:1] + _dot(pe, hi)[1:2]
        z = a + pltpu.roll(bb, nb - 1, 0) + c
        o_ref[0, 0] = _dot(_gelu_tanh(z).astype(BF16), w2_ref[...]).astype(BF16)

    return _pcall(
        body, name=f"nsa_compress_{width}", grid=(b, NSA_GROUPS),
        in_specs=[
            pl.BlockSpec((1, 1, nb, half), lambda bi, g: (bi, g, 0, 0)),
            pl.BlockSpec((8, half), lambda bi, g: (0, 0)),
            pl.BlockSpec((half, width), lambda bi, g: (0, 0)),
            pl.BlockSpec((half, width), lambda bi, g: (0, 0)),
            pl.BlockSpec((width, width), lambda bi, g: (0, 0)),
        ],
        out_specs=pl.BlockSpec((1, 1, nb, width), lambda bi, g: (bi, g, 0, 0)),
        out_shape=jax.ShapeDtypeStruct((b, NSA_GROUPS, nb, width), BF16),
    )(h, pe2, w1lo, w1hi, w2)


def _gate_col(gates, col):
    lane = lax.broadcasted_iota(I32, gates.shape, 1)
    return jnp.sum(jnp.where(lane == col, gates, 0.0), axis=-1, keepdims=True)


def _nsa_cmp(q, kcmp, vcmp, bias_c, cover, gates, *, b, t):
    tq = NSA_TILE
    nb = t // CMP_STRIDE
    n_slc = t // SEL_BLOCK
    n_sel = min(SEL_COUNT, n_slc)
    hpg = NSA_HPG

    def body(q_ref, k_ref, v_ref, bias_ref, cov_ref, g_ref, o_ref, sel_ref):
        g = pl.program_id(1)
        i = pl.program_id(2)
        q2 = q_ref[0].reshape(hpg * tq, HEAD_PAD)
        s3 = _dot_t(q2, k_ref[0, 0]).reshape(hpg, tq, nb) + bias_ref[...]
        p = jnp.exp(s3 - jnp.max(s3, axis=-1, keepdims=True))
        has_key = jnp.where(i * tq + lax.broadcasted_iota(I32, (tq, 1), 0) >= CMP_LEN - 1, 1.0, 0.0)
        p = p * (has_key / jnp.maximum(jnp.sum(p, axis=-1, keepdims=True), 1e-30))
        o3 = _dot(p.reshape(hpg * tq, nb).astype(BF16), v_ref[0, 0]).reshape(hpg, tq, NSA_DV)
        _write_gated_heads(o_ref, o3, g_ref[0], g, 0)

        psum = jnp.sum(p, axis=0)
        p_hi = psum.astype(BF16)
        p_lo = (psum - p_hi.astype(F32)).astype(BF16)
        cov = cov_ref[...]
        imp = _dot(p_hi, cov) + _dot(p_lo, cov)

        jl = lax.broadcasted_iota(I32, (tq, LANES), 1)
        cur = (i * tq + lax.broadcasted_iota(I32, (tq, LANES), 0)) // SEL_BLOCK
        forced = (jl == 0) | (jl == cur) | (jl == cur - 1)
        score = jnp.where(forced, SEL_FORCE, jnp.where(jl <= cur, imp, -1.0))
        n_used = -(-n_slc // 8) * 8
        st = jnp.transpose(score)[:n_used]
        jrow = lax.broadcasted_iota(I32, (n_used, tq), 0)
        rank = jnp.zeros((n_used, tq), F32)
        for kk in range(n_slc):
            row = st[kk:kk + 1, :]
            rank = rank + jnp.where(jrow > kk, jnp.where(row >= st, 1.0, 0.0), jnp.where(row > st, 1.0, 0.0))
        chosen_t = jnp.where((rank < n_sel) & (st >= 0.0), 1.0, 0.0)
        if n_used < LANES:
            chosen_t = jnp.concatenate([chosen_t, jnp.zeros((LANES - n_used, tq), F32)], axis=0)
        sel_ref[0, 0] = jnp.transpose(chosen_t).astype(BF16)

    return _pcall(
        body, name="nsa_cmp_select", grid=(b, NSA_GROUPS, t // tq),
        in_specs=[
            pl.BlockSpec((1, hpg, tq, HEAD_PAD), lambda bi, g, i: (bi, g, i, 0)),
            pl.BlockSpec((1, 1, nb, HEAD_PAD), lambda bi, g, i: (bi, g, 0, 0)),
            pl.BlockSpec((1, 1, nb, NSA_DV), lambda bi, g, i: (bi, g, 0, 0)),
            pl.BlockSpec((hpg, tq, nb), lambda bi, g, i: (g, i, 0)),
            pl.BlockSpec((nb, LANES), lambda bi, g, i: (0, 0)),
            pl.BlockSpec((1, tq, LANES), lambda bi, g, i: (bi, i, 0)),
        ],
        out_specs=[
            pl.BlockSpec((1, tq, hpg * NSA_DV), lambda bi, g, i: (bi, i, g)),
            pl.BlockSpec((1, 1, tq, LANES), lambda bi, g, i: (bi, g, i, 0)),
        ],
        out_shape=[jax.ShapeDtypeStruct((b, t, NSA_HEADS * NSA_DV), BF16),
                   jax.ShapeDtypeStruct((b, NSA_GROUPS, t, LANES), BF16)],
    )(q, kcmp, vcmp, bias_c, cover, gates)


def _write_gated_heads(o_ref, o3, gates_v, g, branch):
    for hh in range(NSA_HPG):
        gc = _gate_col(gates_v, (g * NSA_HPG + hh) * NSA_BRANCHES + branch)
        o_ref[0, :, hh * NSA_DV:(hh + 1) * NSA_DV] = (o3[hh] * gc).astype(BF16)


def _nsa_sel(q, k6, v4, stab, cvec, gates, sel, expand_t, *, b, t):
    tb = NSA_TILE
    hpg = NSA_HPG
    wide = 4 * tb
    span = WINDOW + tb
    back = WINDOW // tb
    parts = 2
    hpp = hpg // parts

    def body(q_ref, k_ref, v_ref, tab_ref, cv_ref, g_ref, sel_ref, et_ref, o_ref):
        g = pl.program_id(1)
        i = pl.program_id(2)
        qs = [q_ref[0, pt * hpp:(pt + 1) * hpp].reshape(hpp * tb, HEAD_PAD) for pt in range(parts)]
        sel_v = sel_ref[0, 0]
        far_end = jnp.maximum(i - back, 0) * tb

        def tile(koff, tk, bias_of_part, in_range, carries):
            k = k_ref[0, 0, pl.ds(koff, tk), :]
            v = v_ref[0, 0, pl.ds(koff, tk), :]
            chosen = _dot_t(sel_v, et_ref[pl.ds(koff, tk), :])
            if in_range is not None:
                chosen = jnp.where(in_range, chosen, 0.0)
            chosen = (chosen > 0.5)[None]
            out = []
            for pt in range(parts):
                s3 = _dot_t(qs[pt], k).reshape(hpp, tb, tk) + bias_of_part(pt)
                s3 = jnp.where(chosen, s3, NEG_INF)
                out.append(_softmax_step(s3.reshape(hpp * tb, tk), v, carries[pt]))
            return tuple(out)

        def far(j, carries):
            koff = pl.multiple_of(j * wide, wide)
            key = koff + lax.broadcasted_iota(I32, (tb, wide), 1)
            return tile(koff, wide, lambda pt: cv_ref[0, pt * hpp:(pt + 1) * hpp][:, None, :], key < far_end, carries)

        n_far = (far_end + wide - 1) // wide
        carries = tuple(_softmax_init(hpp * tb, NSA_DV) for _ in range(parts))
        carries = lax.fori_loop(0, n_far // 2, lambda j, c: far(2 * j + 1, far(2 * j, c)), carries)
        carries = lax.fori_loop(2 * (n_far // 2), n_far, far, carries)
        carries = tile(pl.multiple_of(far_end, tb), span, lambda pt: tab_ref[0, pt * hpp:(pt + 1) * hpp], None, carries)
        o3 = jnp.concatenate([(acc / l).reshape(hpp, tb, NSA_DV) for _, l, acc in carries], axis=0)
        _write_gated_heads(o_ref, o3, g_ref[0], g, 1)

    return _pcall(
        body, name="nsa_sel_attention", grid=(b, NSA_GROUPS, t // tb),
        in_specs=[
            pl.BlockSpec((1, hpg, tb, HEAD_PAD), lambda bi, g, i: (bi, g, i, 0)),
            pl.BlockSpec((1, 1, t, HEAD_PAD), lambda bi, g, i: (bi, NSA_GROUPS + g, 0, 0)),
            pl.BlockSpec((1, 1, t, NSA_DV), lambda bi, g, i: (bi, g, 0, 0)),
            pl.BlockSpec((1, hpg, tb, span), lambda bi, g, i: (jnp.minimum(i, back), g, 0, 0)),
            pl.BlockSpec((1, hpg, wide), lambda bi, g, i: (g, 0, 0)),
            pl.BlockSpec((1, tb, LANES), lambda bi, g, i: (bi, i, 0)),
            pl.BlockSpec((1, 1, tb, LANES), lambda bi, g, i: (bi, g, i, 0)),
            pl.BlockSpec((t, LANES), lambda bi, g, i: (0, 0)),
        ],
        out_specs=pl.BlockSpec((1, tb, hpg * NSA_DV), lambda bi, g, i: (bi, i, g)),
        out_shape=jax.ShapeDtypeStruct((b, t, NSA_HEADS * NSA_DV), BF16),
    )(q, k6, v4, stab, cvec, gates, sel, expand_t)


def _nsa_win(q, k6, v4, wtab, gates, *, b, t):
    tb = NSA_TILE
    hpg = NSA_HPG
    span = WINDOW + tb
    back = WINDOW // tb
    parts = 2
    hpp = hpg // parts

    def body(q_ref, k_ref, v_ref, w_ref, g_ref, o_ref):
        g = pl.program_id(1)
        i = pl.program_id(2)
        koff = pl.multiple_of(jnp.maximum(i - back, 0) * tb, tb)
        k = k_ref[0, 0, pl.ds(koff, span), :]
        v = v_ref[0, 0, pl.ds(koff, span), :]
        o3 = []
        for pt in range(parts):
            q2 = q_ref[0, pt * hpp:(pt + 1) * hpp].reshape(hpp * tb, HEAD_PAD)
            s3 = _dot_t(q2, k).reshape(hpp, tb, span) + w_ref[0, pt * hpp:(pt + 1) * hpp]
            s = s3.reshape(hpp * tb, span)
            p = jnp.exp(s - jnp.max(s, axis=-1, keepdims=True))
            acc = _dot(p.astype(BF16), v)
            o3.append((acc[:, :NSA_DV] / acc[:, NSA_DV:NSA_DV + 1]).reshape(hpp, tb, NSA_DV))
        _write_gated_heads(o_ref, jnp.concatenate(o3, axis=0), g_ref[0], g, 2)

    return _pcall(
        body, name="nsa_win_attention", grid=(b, NSA_GROUPS, t // tb),
        in_specs=[
            pl.BlockSpec((1, hpg, tb, HEAD_PAD), lambda bi, g, i: (bi, g, i, 0)),
            pl.BlockSpec((1, 1, t, HEAD_PAD), lambda bi, g, i: (bi, 2 * NSA_GROUPS + g, 0, 0)),
            pl.BlockSpec((1, 1, t, HEAD_PAD), lambda bi, g, i: (bi, NSA_GROUPS + g, 0, 0)),
            pl.BlockSpec((1, hpg, tb, span), lambda bi, g, i: (jnp.minimum(i, back), g, 0, 0)),
            pl.BlockSpec((1, tb, LANES), lambda bi, g, i: (bi, i, 0)),
        ],
        out_specs=pl.BlockSpec((1, tb, hpg * NSA_DV), lambda bi, g, i: (bi, i, g)),
        out_shape=jax.ShapeDtypeStruct((b, t, NSA_HEADS * NSA_DV), BF16),
    )(q, k6, v4, wtab, gates)


def _router(x, w_hi, w_lo, bias, *, n, d, n_exp, tm):
    def body(x_ref, wh_ref, wl_ref, b_ref, e_ref, gate_ref):
        xv = x_ref[...]
        x_hi = xv.astype(BF16)
        x_lo = (xv - x_hi.astype(F32)).astype(BF16)
        logits = _dot(x_hi, wh_ref[...]) + _dot(x_lo, wh_ref[...]) + _dot(x_hi, wl_ref[...]) + b_ref[...]
        lane = lax.broadcasted_iota(I32, (tm, LANES), 1)
        logits = jnp.where(lane < n_exp, logits, NEG_INF)
        e_out = jnp.zeros((tm, LANES), I32)
        v_out = jnp.full((tm, LANES), NEG_INF, F32)
        for kk in range(TOP_K):
            mx = jnp.max(logits, axis=-1, keepdims=True)
            idx = jnp.min(jnp.where(logits == mx, lane, LANES), axis=-1, keepdims=True)
            e_out = jnp.where(lane == kk, idx, e_out)
            v_out = jnp.where(lane == kk, mx, v_out)
            logits = jnp.where(lane == idx, NEG_INF, logits)
        ex = jnp.where(lane < TOP_K, jnp.exp(v_out - jnp.max(v_out, axis=-1, keepdims=True)), 0.0)
        e_ref[...] = e_out
        gate_ref[...] = ex / jnp.sum(ex, axis=-1, keepdims=True)

    return _pcall(
        body, name="moe_router", grid=(n // tm,),
        in_specs=[pl.BlockSpec((tm, d), lambda i: (i, 0)),
                  pl.BlockSpec((d, LANES), lambda i: (0, 0)),
                  pl.BlockSpec((d, LANES), lambda i: (0, 0)),
                  pl.BlockSpec((1, LANES), lambda i: (0, 0))],
        out_specs=[pl.BlockSpec((tm, LANES), lambda i: (i, 0)), pl.BlockSpec((tm, LANES), lambda i: (i, 0))],
        out_shape=[jax.ShapeDtypeStruct((n, LANES), I32), jax.ShapeDtypeStruct((n, LANES), F32)],
    )(x, w_hi, w_lo, bias)


def _row_copy(src_hbm, row, dst, slot, sem, chunks):
    src = src_hbm.at[pl.ds(pl.multiple_of(row * chunks, chunks), chunks)]
    return pltpu.make_async_copy(src, dst.at[pl.ds(pl.multiple_of(slot * chunks, chunks), chunks)], sem)


def _moe_up(blk_e, n_valid, x_chunks, idx3, w_up, b_up, pick_even, *, layer, n_rows, d, f, tm, tf):
    chunks = d // LANES
    n_blk = n_rows // tm

    def body(be_ref, nv_ref, idx_ref, nxt_ref, x_hbm, w_ref, b_ref, pick_ref, o_ref, buf, xs, sem):
        i = pl.program_id(0)
        nv = nv_ref[0]
        slot = i % 2

        def start_rows(src_idx, s):
            def issue(r8, c):
                for u in range(GATHER_UNROLL):
                    r = r8 * GATHER_UNROLL + u
                    _row_copy(x_hbm, src_idx[0, 0, r], buf.at[s], r, sem.at[s], chunks).start(priority=u % 2)
                return c
            lax.fori_loop(0, tm // GATHER_UNROLL, issue, 0)

        @pl.when(i < nv)
        def _():
            @pl.when(i == 0)
            def _():
                start_rows(idx_ref, 0)

            def wait(r8, c):
                for u in range(GATHER_UNROLL):
                    _row_copy(x_hbm, 0, buf.at[slot], r8 * GATHER_UNROLL + u, sem.at[slot], chunks).wait()
                return c
            lax.fori_loop(0, tm // GATHER_UNROLL, wait, 0)

            @pl.when(i + 1 < nv)
            def _():
                start_rows(nxt_ref, 1 - slot)

            for c in range(chunks):
                xs[:, c * LANES:(c + 1) * LANES] = buf[slot, pl.ds(c, tm, stride=chunks), :].astype(BF16)

            for j in range(f // tf):
                h = _dot(xs[...], w_ref[0, 0, :, 2 * j * tf:2 * (j + 1) * tf]) + b_ref[0, :, 2 * j * tf:2 * (j + 1) * tf]
                g = jnp.minimum(h, SWIGLU_LIMIT)
                glu = g * _sigmoid(SWIGLU_ALPHA * g)
                lin = jnp.clip(h, -SWIGLU_LIMIT, SWIGLU_LIMIT) + 1.0
                pairs = [glu[:, s * LANES:(s + 1) * LANES] * pltpu.roll(lin[:, s * LANES:(s + 1) * LANES], LANES - 1, 1)
                         for s in range(2 * tf // LANES)]
                o_ref[:, j * tf:(j + 1) * tf] = _dot(jnp.concatenate(pairs, axis=1).astype(BF16), pick_ref[...]).astype(BF16)

        @pl.when(i >= nv)
        def _():
            o_ref[...] = jnp.zeros(o_ref.shape, BF16)

    idx_spec = pl.BlockSpec((1, 1, tm), lambda i, be, nv: (i, 0, 0), memory_space=pltpu.SMEM)
    nxt_spec = pl.BlockSpec((1, 1, tm), lambda i, be, nv: (jnp.minimum(i + 1, n_blk - 1), 0, 0), memory_space=pltpu.SMEM)
    return _pcall(
        body, name="moe_up", grid=(n_blk,), prefetch=2,
        in_specs=[idx_spec, nxt_spec, pl.BlockSpec(memory_space=pl.ANY),
                  pl.BlockSpec((1, 1, d, 2 * f), lambda i, be, nv: (layer, be[i], 0, 0)),
                  pl.BlockSpec((1, 1, 2 * f), lambda i, be, nv: (be[i], 0, 0)),
                  pl.BlockSpec((2 * tf, tf), lambda i, be, nv: (0, 0))],
        out_specs=pl.BlockSpec((tm, f), lambda i, be, nv: (i, 0)),
        out_shape=jax.ShapeDtypeStruct((n_rows, f), BF16),
        scratch=[pltpu.VMEM((2, tm * chunks, LANES), F32), pltpu.VMEM((tm, d), BF16), pltpu.SemaphoreType.DMA((2,))],
    )(blk_e, n_valid, idx3, idx3, x_chunks, w_up, b_up, pick_even)


def _moe_down(blk_e, fresh, n_valid, a, w_down, b_down, *, layer, n_rows, d, f, tm):
    chunks = d // LANES

    def body(be_ref, fr_ref, nv_ref, a_ref, w_ref, b_ref, o_ref, wb_ref):
        i = pl.program_id(0)

        @pl.when(fr_ref[i] == 1)
        def _():
            wb_ref[...] = w_ref[0, 0].astype(BF16)

        @pl.when(i < nv_ref[0])
        def _():
            res = _dot(a_ref[...], wb_ref[...]) + b_ref[0]
            for c in range(chunks):
                o_ref[pl.ds(c, tm, stride=chunks), :] = res[:, c * LANES:(c + 1) * LANES]

        @pl.when(i >= nv_ref[0])
        def _():
            o_ref[...] = jnp.zeros(o_ref.shape, F32)

    return _pcall(
        body, name="moe_down", grid=(n_rows // tm,), prefetch=3,
        in_specs=[pl.BlockSpec((tm, f), lambda i, be, fr, nv: (i, 0)),
                  pl.BlockSpec((1, 1, f, d), lambda i, be, fr, nv: (layer, be[i], 0, 0)),
                  pl.BlockSpec((1, 1, d), lambda i, be, fr, nv: (be[i], 0, 0))],
        out_specs=pl.BlockSpec((tm * chunks, LANES), lambda i, be, fr, nv: (i, 0)),
        out_shape=jax.ShapeDtypeStruct((n_rows * chunks, LANES), F32),
        scratch=[pltpu.VMEM((f, d), BF16)],
    )(blk_e, fresh, n_valid, a, w_down, b_down)


def _moe_combine(pos3, ys, gates, x, ln_g, ln_b, *, n, d, tq, alpha):
    chunks = d // LANES
    n_tiles = n // tq

    def body(pos_ref, nxt_ref, y_hbm, g_ref, x_ref, lg_ref, lb_ref, of_ref, ob_ref, buf, y_ref, sem):
        i = pl.program_id(0)
        slot = i % 2

        def start_rows(src_pos, s):
            def issue(r, c):
                for kk in range(TOP_K):
                    _row_copy(y_hbm, src_pos[0, 0, kk * tq + r], buf.at[s, kk], r, sem.at[s], chunks).start(priority=kk % 2)
                return c
            lax.fori_loop(0, tq, issue, 0)

        @pl.when(i == 0)
        def _():
            start_rows(pos_ref, 0)

        def wait(r, c):
            for kk in range(TOP_K):
                _row_copy(y_hbm, 0, buf.at[slot, kk], r, sem.at[slot], chunks).wait()
            return c
        lax.fori_loop(0, tq, wait, 0)

        @pl.when(i + 1 < n_tiles)
        def _():
            start_rows(nxt_ref, 1 - slot)

        gv = g_ref[...]
        for c in range(chunks):
            piece = gv[:, 0:1] * buf[slot, 0, pl.ds(c, tq, stride=chunks), :]
            for kk in range(1, TOP_K):
                piece = piece + gv[:, kk:kk + 1] * buf[slot, kk, pl.ds(c, tq, stride=chunks), :]
            y_ref[:, c * LANES:(c + 1) * LANES] = piece
        out = _layer_norm(alpha * x_ref[...] + y_ref[...], lg_ref[...], lb_ref[...])
        of_ref[...] = out
        ob_ref[...] = out.astype(BF16)

    return _pcall(
        body, name="moe_combine_ln", grid=(n_tiles,),
        in_specs=[pl.BlockSpec((1, 1, TOP_K * tq), lambda i: (i, 0, 0), memory_space=pltpu.SMEM),
                  pl.BlockSpec((1, 1, TOP_K * tq), lambda i: (jnp.minimum(i + 1, n_tiles - 1), 0, 0), memory_space=pltpu.SMEM),
                  pl.BlockSpec(memory_space=pl.ANY),
                  pl.BlockSpec((tq, LANES), lambda i: (i, 0)),
                  pl.BlockSpec((tq, d), lambda i: (i, 0)),
                  pl.BlockSpec((1, d), lambda i: (0, 0)),
                  pl.BlockSpec((1, d), lambda i: (0, 0))],
        out_specs=[pl.BlockSpec((tq, d), lambda i: (i, 0)), pl.BlockSpec((tq, d), lambda i: (i, 0))],
        out_shape=[jax.ShapeDtypeStruct((n, d), F32), jax.ShapeDtypeStruct((n, d), BF16)],
        scratch=[pltpu.VMEM((2, TOP_K, tq * chunks, LANES), F32), pltpu.VMEM((tq, d), F32), pltpu.SemaphoreType.DMA((2,))],
    )(pos3, pos3, ys, gates, x, ln_g, ln_b)


def _t5_bucket(dist):
    max_exact = REL_BUCKETS // 2
    dd = jnp.maximum(dist, 1).astype(F32)
    large = max_exact + (jnp.log(dd / max_exact) / math.log(REL_MAX_DIST / max_exact)
                         * (REL_BUCKETS - max_exact)).astype(I32)
    large = jnp.minimum(large, REL_BUCKETS - 1)
    return jnp.where(dist < max_exact, dist, large)


def _tables(positions, rel_bias, t):
    half = MLA_ROPE // 2
    inv = 1.0 / (ROPE_THETA ** (jnp.arange(0, MLA_ROPE, 2, dtype=F32) / MLA_ROPE))
    ang = positions.astype(F32)[:, None] * inv[None, :]
    zeros = jnp.zeros((t, LANES - MLA_ROPE), F32)
    cos = jnp.concatenate([jnp.cos(ang), jnp.cos(ang), zeros], axis=-1)
    sin = jnp.concatenate([jnp.sin(ang), jnp.sin(ang), zeros], axis=-1)
    assert half * 2 == MLA_ROPE

    first = jnp.sum(_t5_bucket(jnp.arange(t))[None, :] < jnp.arange(REL_BUCKETS)[:, None], axis=1).astype(I32)
    rb = rel_bias.astype(F32)

    def bias_of(dist, valid):
        out = jnp.broadcast_to(rb[0].reshape((NSA_HEADS,) + (1,) * dist.ndim), (NSA_HEADS,) + dist.shape)
        for bkt in range(1, REL_BUCKETS):
            out = jnp.where((dist >= first[bkt])[None], rb[bkt].reshape((NSA_HEADS,) + (1,) * dist.ndim), out)
        return jnp.where(valid[None], out, NEG_INF)

    tb = NSA_TILE
    sat = rb[REL_BUCKETS - 1]
    cvec = jnp.broadcast_to(sat.reshape(NSA_GROUPS, NSA_HPG, 1), (NSA_GROUPS, NSA_HPG, 4 * tb))
    span = WINDOW + tb
    wd = (jnp.arange(WINDOW // tb + 1, dtype=I32) * tb)[:, None, None] + jnp.arange(tb, dtype=I32)[None, :, None] \
        - jnp.arange(span, dtype=I32)[None, None, :]
    wtab = bias_of(wd, (wd >= 0) & (wd < WINDOW)).transpose(1, 0, 2, 3)
    stab = bias_of(wd, wd >= 0).transpose(1, 0, 2, 3)
    nb = t // CMP_STRIDE
    dist_c = jnp.arange(t, dtype=I32)[:, None] - (jnp.arange(nb, dtype=I32) * CMP_STRIDE + CMP_LEN - 1)[None, :]
    bias_c = bias_of(dist_c, dist_c >= 0)
    tq_mla = _pick_tile(t, 512)
    cm = jnp.arange(tq_mla, dtype=I32)
    causal = jnp.where(cm[None, :] <= cm[:, None], 0.0, NEG_INF).astype(F32)

    n_slc = t // SEL_BLOCK
    tok = np.arange(nb)[:, None] * CMP_STRIDE + np.arange(CMP_LEN)[None, :]
    cover = (tok[:, :, None] // SEL_BLOCK == np.arange(LANES)[None, None, :]).sum(1) / CMP_LEN
    cover[nb - 1] = 0.0
    cover[:, n_slc:] = 0.0
    expand_t = (np.arange(t)[:, None] // SEL_BLOCK == np.arange(LANES)[None, :])
    return dict(cos=cos, sin=sin, stab=stab, cvec=cvec, wtab=wtab, bias_c=bias_c, causal=causal,
                cover=jnp.asarray(cover, BF16), expand_t=jnp.asarray(expand_t, BF16))


def _pad_heads(w, heads, width):
    k = w.shape[0]
    w = w.reshape(k, heads, width)
    return jnp.pad(w, ((0, 0), (0, 0), (0, HEAD_PAD - width))).reshape(k, heads * HEAD_PAD)


def _pick_tile(total, want):
    tile = min(total, want)
    while total % tile:
        tile //= 2
    return tile


def _token_mixer(x, xb, tabs, p, *, b, t, d):
    n = b * t
    tm = _pick_tile(t, 512)
    tpb = t // tm
    offs = np.cumsum([0, MLA_Q_LORA, MLA_KV_LORA, MLA_ROPE, NSA_HEADS * NSA_DK,
                      NSA_GROUPS * NSA_DK, NSA_GROUPS * NSA_DV, NSA_GROUPS * NSA_DK, NSA_GROUPS * NSA_DV,
                      NSA_GROUPS * NSA_DK, NSA_GROUPS * NSA_DV, NSA_HEADS * NSA_BRANCHES, 2 * d])
    w_in = p["w_in"]
    seg = [w_in[:, offs[s]:offs[s + 1]] for s in range(12)]
    x_lhs = [(xb, _row_spec(tm, d))]
    mla_scale = (MLA_NOPE + MLA_ROPE) ** -0.5
    nsa_scale = NSA_DK ** -0.5

    def cast_epi(acc, er, orr):
        orr[0][...] = acc.astype(BF16)

    lat_w = MLA_Q_LORA + MLA_KV_LORA
    w_lat = jnp.concatenate([seg[0], seg[1]], axis=1).astype(BF16)
    tn_lat = _pick_tile(lat_w, 1024)
    (lat,) = _matmul("proj_latents", x_lhs, w_lat, [], [(jax.ShapeDtypeStruct((n, lat_w), BF16), _tile_spec(tm, tn_lat))],
                     _cast_prologue, cast_epi, m=n, k=d, n=lat_w, tm=tm, tn=tn_lat)

    w_small = jnp.concatenate([seg[2], jnp.zeros((d, LANES - MLA_ROPE), F32), seg[10],
                               jnp.zeros((d, LANES - NSA_HEADS * NSA_BRANCHES), F32)], axis=1).astype(BF16)

    def small_epi(acc, er, orr):
        orr[0][...] = _rope_slab(acc[:, :LANES], er[0][...], er[1][...]).astype(BF16)
        orr[1][...] = _sigmoid(acc[:, LANES:])

    cs_extras = [(tabs["cos_n"], _row_spec(tm, LANES)), (tabs["sin_n"], _row_spec(tm, LANES))]
    kr, nsa_gates = _matmul(
        "proj_rope_key_gates", x_lhs, w_small, cs_extras,
        [(jax.ShapeDtypeStruct((n, LANES), BF16), _row_spec(tm, LANES)),
         (jax.ShapeDtypeStruct((n, LANES), F32), _row_spec(tm, LANES))],
        _cast_prologue, small_epi, m=n, k=d, n=2 * LANES, tm=tm, tn=2 * LANES)

    hb = 2
    hq = 4
    w_qn = _pad_heads(seg[3], NSA_HEADS, NSA_DK).astype(BF16)
    (q_nsa,) = _matmul("proj_nsa_q", x_lhs, w_qn, [],
                       [(jax.ShapeDtypeStruct((b, NSA_HEADS, t, HEAD_PAD), BF16), _head_spec(tm, hq, HEAD_PAD, tpb))],
                       _cast_prologue, _heads_epilogue(hq, HEAD_PAD, nsa_scale),
                       m=n, k=d, n=NSA_HEADS * HEAD_PAD, tm=tm, tn=hq * HEAD_PAD)

    w_k6 = jnp.concatenate([_pad_heads(seg[s], NSA_GROUPS, NSA_DK) for s in (4, 6, 8)], axis=1).astype(BF16)
    (k6,) = _matmul("proj_nsa_k", x_lhs, w_k6, [],
                    [(jax.ShapeDtypeStruct((b, 3 * NSA_GROUPS, t, HEAD_PAD), BF16), _head_spec(tm, hb, HEAD_PAD, tpb))],
                    _cast_prologue, _heads_epilogue(hb, HEAD_PAD),
                    m=n, k=d, n=3 * NSA_GROUPS * HEAD_PAD, tm=tm, tn=hb * HEAD_PAD)

    (v_c,) = _matmul("proj_nsa_v_cmp", x_lhs, seg[5].astype(BF16), [],
                     [(jax.ShapeDtypeStruct((b, NSA_GROUPS, t, NSA_DV), BF16), _head_spec(tm, hb, NSA_DV, tpb))],
                     _cast_prologue, _heads_epilogue(hb, NSA_DV),
                     m=n, k=d, n=NSA_GROUPS * NSA_DV, tm=tm, tn=hb * NSA_DV)

    def ones_lane(rows):
        return jnp.where(lax.broadcasted_iota(I32, (rows, HEAD_PAD - NSA_DV), 1) == 0, 1.0, 0.0).astype(BF16)

    def v_ones_epi(acc, er, orr):
        for c in range(hb):
            orr[0][0, c, :, :NSA_DV] = acc[:, c * NSA_DV:(c + 1) * NSA_DV].astype(BF16)
            orr[0][0, c, :, NSA_DV:] = ones_lane(acc.shape[0])

    w_v4 = jnp.concatenate([seg[7], seg[9]], axis=1).astype(BF16)
    (v4,) = _matmul("proj_nsa_v", x_lhs, w_v4, [],
                    [(jax.ShapeDtypeStruct((b, 2 * NSA_GROUPS, t, HEAD_PAD), BF16), _head_spec(tm, hb, HEAD_PAD, tpb))],
                    _cast_prologue, v_ones_epi, m=n, k=d, n=2 * NSA_GROUPS * NSA_DV, tm=tm, tn=hb * NSA_DV)

    def sig_epi(acc, er, orr):
        orr[0][...] = _sigmoid(acc).astype(BF16)

    tn_d = _pick_tile(d, 1024)
    (merge,) = _matmul("proj_merge_gates", x_lhs, seg[11].astype(BF16), [],
                       [(jax.ShapeDtypeStruct((n, 2 * d), BF16), _tile_spec(tm, tn_d))],
                       _cast_prologue, sig_epi, m=n, k=d, n=2 * d, tm=tm, tn=tn_d)

    w_uq = p["mla_w_uq"].reshape(MLA_Q_LORA, MLA_HEADS, MLA_NOPE + MLA_ROPE)
    w_uq = jnp.pad(w_uq, ((0, 0), (0, 0), (0, HEAD_PAD - MLA_NOPE - MLA_ROPE))).reshape(MLA_Q_LORA, MLA_HEADS * HEAD_PAD)

    def q_epi(acc, er, orr):
        cos, sin = er[0][...], er[1][...]
        for c in range(hq):
            orr[0][0, c, :, :MLA_NOPE] = (acc[:, c * HEAD_PAD:c * HEAD_PAD + MLA_NOPE] * mla_scale).astype(BF16)
            slab = acc[:, c * HEAD_PAD + MLA_NOPE:(c + 1) * HEAD_PAD] * mla_scale
            orr[0][0, c, :, MLA_NOPE:] = _rope_slab(slab, cos, sin).astype(BF16)

    (q_mla,) = _matmul(
        "mla_q_up", [(lat, _row_spec(tm, MLA_Q_LORA)), (p["mla_q_norm"].reshape(1, -1), pl.BlockSpec((1, MLA_Q_LORA), lambda i, j: (0, 0)))],
        w_uq.astype(BF16), cs_extras,
        [(jax.ShapeDtypeStruct((b, MLA_HEADS, t, HEAD_PAD), BF16), _head_spec(tm, hq, HEAD_PAD, tpb))],
        _rms_prologue, q_epi, m=n, k=MLA_Q_LORA, n=MLA_HEADS * HEAD_PAD, tm=tm, tn=hq * HEAD_PAD)

    kv_w = MLA_NOPE + MLA_V

    def kv_epi(acc, er, orr):
        k_rope = er[0][...]
        for c in range(hq):
            orr[0][0, c, :, :MLA_NOPE] = acc[:, c * kv_w:c * kv_w + MLA_NOPE].astype(BF16)
            orr[0][0, c, :, MLA_NOPE:] = k_rope
            orr[1][0, c, :, :MLA_V] = acc[:, c * kv_w + MLA_NOPE:(c + 1) * kv_w].astype(BF16)
            orr[1][0, c, :, MLA_V:] = ones_lane(acc.shape[0])

    k_mla, v_mla = _matmul(
        "mla_kv_up", [(lat, _row_spec(tm, MLA_KV_LORA, MLA_Q_LORA // MLA_KV_LORA)),
                      (p["mla_kv_norm"].reshape(1, -1), pl.BlockSpec((1, MLA_KV_LORA), lambda i, j: (0, 0)))],
        p["mla_w_ukv"].astype(BF16), [(kr, _row_spec(tm, LANES))],
        [(jax.ShapeDtypeStruct((b, MLA_HEADS, t, HEAD_PAD), BF16), _head_spec(tm, hq, HEAD_PAD, tpb)),
         (jax.ShapeDtypeStruct((b, MLA_HEADS, t, HEAD_PAD), BF16), _head_spec(tm, hq, HEAD_PAD, tpb))],
        _rms_prologue, kv_epi, m=n, k=MLA_KV_LORA, n=MLA_HEADS * kv_w, tm=tm, tn=hq * kv_w)

    attn_a = _mla_attention(q_mla, k_mla, v_mla, tabs["causal"], b=b, t=t, tq=tabs["causal"].shape[0])

    def ya_epi(acc, er, orr):
        orr[0][...] = (er[0][...].astype(F32) * acc).astype(BF16)

    (y_a,) = _matmul("mla_branch_out", [(attn_a.reshape(n, MLA_HEADS * MLA_V), _row_spec(tm, MLA_HEADS * MLA_V))],
                     p["w_branch_a"].astype(BF16), [(merge, _tile_spec(tm, tn_d))],
                     [(jax.ShapeDtypeStruct((n, d), BF16), _tile_spec(tm, tn_d))],
                     _cast_prologue, ya_epi, m=n, k=MLA_HEADS * MLA_V, n=d, tm=tm, tn=tn_d)

    def cmp_weights(pe, w1, w2, width_in, width):
        pe_p = jnp.pad(pe, ((0, 0), (0, width - width_in)))
        pe2 = jnp.zeros((8, CMP_STRIDE * width), F32).at[0].set(pe_p[:CMP_STRIDE].reshape(-1)).at[1].set(pe_p[CMP_STRIDE:].reshape(-1))
        w1p = jnp.pad(w1.reshape(CMP_LEN, width_in, width_in), ((0, 0), (0, width - width_in), (0, width - width_in)))
        w2p = jnp.pad(w2, ((0, width - width_in), (0, width - width_in)))
        return (pe2.astype(BF16), w1p[:CMP_STRIDE].reshape(-1, width).astype(BF16),
                w1p[CMP_STRIDE:].reshape(-1, width).astype(BF16), w2p.astype(BF16))

    kcmp = _compress(k6, *cmp_weights(p["cmp_pe_k"], p["cmp_w1_k"], p["cmp_w2_k"], NSA_DK, HEAD_PAD), b=b, t=t, width=HEAD_PAD)
    vcmp = _compress(v_c, *cmp_weights(p["cmp_pe_v"], p["cmp_w1_v"], p["cmp_w2_v"], NSA_DV, NSA_DV), b=b, t=t, width=NSA_DV)
    gates3 = nsa_gates.reshape(b, t, LANES)
    o_cmp, sel = _nsa_cmp(q_nsa, kcmp, vcmp, tabs["bias_c"], tabs["cover"], gates3, b=b, t=t)
    o_slc = _nsa_sel(q_nsa, k6, v4, tabs["stab"], tabs["cvec"], gates3, sel, tabs["expand_t"], b=b, t=t)
    o_win = _nsa_win(q_nsa, k6, v4, tabs["wtab"], gates3, b=b, t=t)

    def sum3_prologue(a_ref, b_ref, c_ref):
        return (a_ref[...].astype(F32) + b_ref[...].astype(F32) + c_ref[...].astype(F32)).astype(BF16)

    def merge_epi(acc, er, orr):
        orr[0][...] = (er[0][...].astype(F32) + er[1][...].astype(F32) * acc).astype(BF16)

    hv = NSA_HEADS * NSA_DV
    (merged,) = _matmul("nsa_branch_out_merge", [(o.reshape(n, hv), _row_spec(tm, hv)) for o in (o_cmp, o_slc, o_win)],
                        p["w_branch_b"].astype(BF16), [(y_a, _tile_spec(tm, tn_d)), (merge, _tile_spec(tm, tn_d, d // tn_d))],
                        [(jax.ShapeDtypeStruct((n, d), BF16), _tile_spec(tm, tn_d))],
                        sum3_prologue, merge_epi, m=n, k=hv, n=d, tm=tm, tn=tn_d)

    tm_ln = _pick_tile(t, 256)
    alpha = p["alpha"]

    chunks = d // LANES

    def ln_epi(acc, er, orr):
        out = _layer_norm(alpha * er[0][...] + acc, er[1][...], er[2][...])
        orr[0][...] = out
        for c in range(chunks):
            orr[1][pl.ds(c, tm_ln, stride=chunks), :] = out[:, c * LANES:(c + 1) * LANES]

    vec_spec = pl.BlockSpec((1, d), lambda i, j: (0, 0))
    x1, x1_chunks = _matmul(
        "mixer_out_ln", [(merged, _row_spec(tm_ln, d))], p["w_out"].astype(BF16),
        [(x, _row_spec(tm_ln, d)), (p["ln1_g"].reshape(1, d), vec_spec), (p["ln1_b"].reshape(1, d), vec_spec)],
        [(jax.ShapeDtypeStruct((n, d), F32), _row_spec(tm_ln, d)),
         (jax.ShapeDtypeStruct((n * chunks, LANES), F32), pl.BlockSpec((tm_ln * chunks, LANES), lambda i, j: (i, 0)))],
        _cast_prologue, ln_epi, m=n, k=d, n=d, tm=tm_ln, tn=d)
    return x1, x1_chunks


def _moe(x, x_chunks, p, *, n, d):
    _, n_exp, _, f2 = p["exp_w_up"].shape
    f = f2 // 2
    tm_e = p["tm_e"]
    rw = jnp.pad(p["router_w"], ((0, 0), (0, LANES - n_exp)))
    rw_hi = rw.astype(BF16)
    rw_lo = (rw - rw_hi.astype(F32)).astype(BF16)
    rb = jnp.pad(p["router_b"], (0, LANES - n_exp)).reshape(1, LANES)
    top_e, gates = _router(x, rw_hi, rw_lo, rb, n=n, d=d, n_exp=n_exp, tm=_pick_tile(n, 512))

    n_pairs = n * TOP_K
    e_flat = top_e[:, :TOP_K].reshape(-1)
    grp = LANES
    onehot = (e_flat[:, None] == jnp.arange(n_exp, dtype=I32)[None, :]).astype(F32).reshape(n_pairs // grp, grp, n_exp)
    tri = jnp.tril(jnp.ones((grp, grp), F32))
    within = jnp.einsum("ij,gje->gie", tri, onehot)
    totals = within[:, -1, :]
    before = jnp.cumsum(totals, axis=0) - totals
    rank = (jnp.sum((within + before[:, None, :]) * onehot, axis=-1).reshape(-1) - 1.0).astype(I32)
    counts = jnp.sum(totals, axis=0).astype(I32)
    padded = (counts + tm_e - 1) // tm_e * tm_e
    pad_end = jnp.cumsum(padded)
    dest = (pad_end - padded)[e_flat] + rank
    n_rows = n_pairs + n_exp * tm_e
    n_blk = n_rows // tm_e
    row_tok = jnp.zeros((n_rows,), I32).at[dest].set(jnp.arange(n_pairs, dtype=I32) // TOP_K)
    n_valid = (pad_end[-1] // tm_e).astype(I32)
    blk_e = jnp.sum(pad_end[None, :] <= (jnp.arange(n_blk, dtype=I32) * tm_e)[:, None], axis=1).astype(I32)
    blk_e = jnp.minimum(blk_e, blk_e[jnp.maximum(n_valid - 1, 0)])
    n_valid = n_valid.reshape(1)

    tf = _pick_tile(f, 512)
    pick_even = jnp.asarray(np.arange(2 * tf)[:, None] == 2 * np.arange(tf)[None, :], BF16)
    a = _moe_up(blk_e, n_valid, x_chunks, row_tok.reshape(n_blk, 1, tm_e),
                p["exp_w_up_bf16"], p["exp_b_up"].reshape(n_exp, 1, 2 * f), pick_even,
                layer=p["layer"], n_rows=n_rows, d=d, f=f, tm=tm_e, tf=tf)
    split = max(1, tm_e // 256)
    tm_d = tm_e // split
    blk_d = jnp.repeat(blk_e, split)
    fresh = jnp.concatenate([jnp.ones((1,), I32), (blk_d[1:] != blk_d[:-1]).astype(I32)])
    ys = _moe_down(blk_d, fresh, n_valid * split, a, p["exp_w_down"], p["exp_b_down"].reshape(n_exp, 1, d),
                   layer=p["layer"], n_rows=n_rows, d=d, f=f, tm=tm_d)

    tq = _pick_tile(n, 128)
    pos3 = dest.reshape(n // tq, tq, TOP_K).transpose(0, 2, 1).reshape(n // tq, 1, TOP_K * tq)
    return _moe_combine(pos3, ys, gates, x, p["ln2_g"].reshape(1, d), p["ln2_b"].reshape(1, d),
                        n=n, d=d, tq=tq, alpha=p["alpha"])


def kernel(x, positions, rel_bias, w_in, mla_q_norm, mla_w_uq, mla_kv_norm, mla_w_ukv, cmp_pe_k, cmp_w1_k, cmp_w2_k, cmp_pe_v, cmp_w1_v, cmp_w2_v, w_branch_a, w_branch_b, w_out, ln1_g, ln1_b, router_w, router_b, exp_w_up, exp_b_up, exp_w_down, exp_b_down, ln2_g, ln2_b):
    b, t, d = x.shape
    depth = w_in.shape[0]
    n = b * t
    alpha = (2 * depth) ** 0.25
    tabs = _tables(positions, rel_bias, t)
    tabs["cos_n"] = jnp.tile(tabs["cos"], (b, 1))
    tabs["sin_n"] = jnp.tile(tabs["sin"], (b, 1))
    xf = x.reshape(n, d)
    xb = xf.astype(BF16)
    exp_w_up_bf16 = exp_w_up.astype(BF16)
    for l in range(depth):
        p = dict(w_in=w_in[l], mla_q_norm=mla_q_norm[l], mla_w_uq=mla_w_uq[l], mla_kv_norm=mla_kv_norm[l],
                 mla_w_ukv=mla_w_ukv[l], cmp_pe_k=cmp_pe_k[l], cmp_w1_k=cmp_w1_k[l], cmp_w2_k=cmp_w2_k[l],
                 cmp_pe_v=cmp_pe_v[l], cmp_w1_v=cmp_w1_v[l], cmp_w2_v=cmp_w2_v[l], w_branch_a=w_branch_a[l],
                 w_branch_b=w_branch_b[l], w_out=w_out[l], ln1_g=ln1_g[l], ln1_b=ln1_b[l], router_w=router_w[l],
                 router_b=router_b[l], exp_w_up=exp_w_up, exp_w_up_bf16=exp_w_up_bf16, exp_b_up=exp_b_up[l], exp_w_down=exp_w_down, layer=l,
                 exp_b_down=exp_b_down[l], ln2_g=ln2_g[l], ln2_b=ln2_b[l], alpha=alpha, tm_e=_pick_tile(n * TOP_K, 512))
        x1, x1_chunks = _token_mixer(xf, xb, tabs, p, b=b, t=t, d=d)
        xf, xb = _moe(x1, x1_chunks, p, n=n, d=d)
    return xf.reshape(b, t, d)
```

```python
import functools
import math

import numpy as np
import jax
import jax.numpy as jnp
from jax import lax
from jax.experimental import pallas as pl
from jax.experimental.pallas import tpu as pltpu

F32, BF16, I32 = jnp.float32, jnp.bfloat16, jnp.int32

MLA_HEADS = 16
MLA_Q_LORA = 1536
MLA_KV_LORA = 512
MLA_NOPE = 128
MLA_ROPE = 64
MLA_V = 128
ROPE_THETA = 10000.0
NSA_HEADS = 16
NSA_GROUPS = 2
NSA_HPG = NSA_HEADS // NSA_GROUPS
NSA_DK = 192
NSA_DV = 128
CMP_LEN = 32
CMP_STRIDE = 16
SEL_BLOCK = 64
SEL_COUNT = 16
WINDOW = 512
NSA_BRANCHES = 3
SEL_FORCE = 1.0e4
REL_BUCKETS = 32
REL_MAX_DIST = 128
TOP_K = 4
SWIGLU_LIMIT = 7.0
SWIGLU_ALPHA = 1.702
NEG_INF = -1.0e30
LN_EPS = 1e-5
RMS_EPS = 1e-6

LANES = 128
HEAD_PAD = 256
NSA_TILE = 128
GATHER_UNROLL = 8
MLA_TRIP_WIDTHS = (2, 1)
MLA_HEADS_PER_STEP = 2
VMEM_LIMIT_BYTES = 56 * 1024 * 1024


def _pcall(body, *, name, grid, in_specs, out_specs, out_shape, scratch=(), prefetch=0):
    gs = pltpu.PrefetchScalarGridSpec(num_scalar_prefetch=prefetch, grid=grid, in_specs=in_specs,
                                      out_specs=out_specs, scratch_shapes=list(scratch))
    return pl.pallas_call(
        body, grid_spec=gs, out_shape=out_shape, name=name,
        compiler_params=pltpu.CompilerParams(dimension_semantics=("arbitrary",) * len(grid),
                                             vmem_limit_bytes=VMEM_LIMIT_BYTES))


def _sigmoid(x):
    return 1.0 / (1.0 + jnp.exp(-x))


def _dot_t(a, b):
    return lax.dot_general(a, b, (((1,), (1,)), ((), ())), preferred_element_type=F32)


def _dot(a, b):
    return jnp.dot(a, b, preferred_element_type=F32)


def _layer_norm(r, g, b):
    mu = jnp.mean(r, axis=-1, keepdims=True)
    c = r - mu
    var = jnp.mean(c * c, axis=-1, keepdims=True)
    return c * lax.rsqrt(var + LN_EPS) * g + b


def _matmul(name, lhs, w, extras, outs, prologue, epilogue, *, m, k, n, tm, tn):
    nl, ne, no = len(lhs), len(extras), len(outs)

    def body(*refs):
        lr = refs[:nl]
        wr = refs[nl]
        er = refs[nl + 1:nl + 1 + ne]
        orr = refs[nl + 1 + ne:nl + 1 + ne + no]
        xs = refs[-1]

        @pl.when(pl.program_id(1) == 0)
        def _():
            xs[...] = prologue(*lr)

        epilogue(_dot(xs[...], wr[...]), er, orr)

    return _pcall(
        body, name=name, grid=(m // tm, n // tn),
        in_specs=[s for _, s in lhs] + [pl.BlockSpec((k, tn), lambda i, j: (0, j))] + [s for _, s in extras],
        out_specs=[s for _, s in outs], out_shape=[o for o, _ in outs],
        scratch=[pltpu.VMEM((tm, k), BF16)],
    )(*[a for a, _ in lhs], w, *[a for a, _ in extras])


def _row_spec(tm, width, col=0):
    return pl.BlockSpec((tm, width), lambda i, j: (i, col))


def _tile_spec(tm, tn, col_off=0):
    return pl.BlockSpec((tm, tn), lambda i, j: (i, j + col_off))


def _head_spec(tm, hb, width, tpb):
    return pl.BlockSpec((1, hb, tm, width), lambda i, j: (i // tpb, j, i % tpb, 0))


def _cast_prologue(x_ref):
    return x_ref[...].astype(BF16)


def _rms_prologue(x_ref, g_ref):
    xf = x_ref[...].astype(F32)
    ms = jnp.mean(xf * xf, axis=-1, keepdims=True)
    return (xf * lax.rsqrt(ms + RMS_EPS) * g_ref[...]).astype(BF16)


def _rope_slab(t, cos, sin):
    lane = lax.broadcasted_iota(I32, t.shape, 1)
    rot = jnp.where(lane < MLA_ROPE // 2, -pltpu.roll(t, LANES - MLA_ROPE // 2, 1), pltpu.roll(t, MLA_ROPE // 2, 1))
    return t * cos + rot * sin


def _heads_epilogue(hb, width, scale=None):
    def epi(acc, er, orr):
        for c in range(hb):
            v = acc[:, c * width:(c + 1) * width]
            if scale is not None:
                v = v * scale
            orr[0][0, c] = v.astype(BF16)
    return epi


def _softmax_init(rows, width):
    return (jnp.full((rows, 1), NEG_INF, F32), jnp.zeros((rows, 1), F32), jnp.zeros((rows, width), F32))


def _softmax_step(s, v, carry):
    m, l, acc = carry
    m_new = jnp.maximum(m, jnp.max(s, axis=-1, keepdims=True))
    a = jnp.exp(m - m_new)
    p = jnp.exp(s - m_new)
    return m_new, a * l + jnp.sum(p, axis=-1, keepdims=True), a * acc + _dot(p.astype(BF16), v)


def _mla_attention(q, k, v_ones, causal, *, b, t, tq):
    nq = t // tq

    hp = MLA_HEADS_PER_STEP

    def body(q_ref, k_ref, v_ref, c_ref, o_ref):
        i = pl.program_id(2)

        def step(koff, carries, diagonal):
            out = []
            for hh in range(hp):
                s = _dot_t(q_ref[0, hh], k_ref[0, hh, pl.ds(koff, tq), :])
                if diagonal:
                    s = s + c_ref[...]
                out.append(_softmax_step(s, v_ref[0, hh, pl.ds(koff, tq), :], carries[hh]))
            return tuple(out)

        carries = tuple(_softmax_init(tq, MLA_V) for _ in range(hp))
        done = 0
        for width in MLA_TRIP_WIDTHS:
            def group(j, c, width=width, done=done):
                for u in range(width):
                    c = step(pl.multiple_of((done + width * j + u) * tq, tq), c, False)
                return c
            n_grp = (i - done) // width
            carries = lax.fori_loop(0, n_grp, group, carries)
            done = done + n_grp * width
        carries = step(pl.multiple_of(i * tq, tq), carries, True)
        for hh, (_, l, acc) in enumerate(carries):
            o_ref[0, :, hh * MLA_V:(hh + 1) * MLA_V] = (acc / l).astype(BF16)

    return _pcall(
        body, name="mla_attention", grid=(b, MLA_HEADS // hp, nq),
        in_specs=[pl.BlockSpec((1, hp, tq, HEAD_PAD), lambda bi, h, i: (bi, h, i, 0)),
                  pl.BlockSpec((1, hp, t, HEAD_PAD), lambda bi, h, i: (bi, h, 0, 0)),
                  pl.BlockSpec((1, hp, t, MLA_V), lambda bi, h, i: (bi, h, 0, 0)),
                  pl.BlockSpec((tq, tq), lambda bi, h, i: (0, 0))],
        out_specs=pl.BlockSpec((1, tq, hp * MLA_V), lambda bi, h, i: (bi, i, h)),
        out_shape=jax.ShapeDtypeStruct((b, t, MLA_HEADS * MLA_V), BF16),
    )(q, k, v_ones, causal)


def _gelu_tanh(x):
    return 0.5 * x * (1.0 + jnp.tanh(math.sqrt(2.0 / math.pi) * (x + 0.044715 * (x * x * x))))


def _compress(kv6, pe2, w1lo, w1hi, w2, *, b, t, width):
    nb = t // CMP_STRIDE
    half = CMP_STRIDE * width
    h = kv6.reshape(b, kv6.shape[1], nb, half)

    def body(h_ref, pe_ref, lo_ref, hi_ref, w2_ref, o_ref):
        hv = h_ref[0, 0]
        lo, hi = lo_ref[...], hi_ref[...]
        a = _dot(hv, lo)
        bb = _dot(hv, hi)
        pe = pe_ref[...]
        c = _dot(pe, lo)[0:1] + _dot(pe, hi)[1:2]
        z = a + pltpu.roll(bb, nb - 1, 0) + c
        o_ref[0, 0] = _dot(_gelu_tanh(z).astype(BF16), w2_ref[...]).astype(BF16)

    return _pcall(
        body, name=f"nsa_compress_{width}", grid=(b, NSA_GROUPS),
        in_specs=[
            pl.BlockSpec((1, 1, nb, half), lambda bi, g: (bi, g, 0, 0)),
            pl.BlockSpec((8, half), lambda bi, g: (0, 0)),
            pl.BlockSpec((half, width), lambda bi, g: (0, 0)),
            pl.BlockSpec((half, width), lambda bi, g: (0, 0)),
            pl.BlockSpec((width, width), lambda bi, g: (0, 0)),
        ],
        out_specs=pl.BlockSpec((1, 1, nb, width), lambda bi, g: (bi, g, 0, 0)),
        out_shape=jax.ShapeDtypeStruct((b, NSA_GROUPS, nb, width), BF16),
    )(h, pe2, w1lo, w1hi, w2)


def _gate_col(gates, col):
    lane = lax.broadcasted_iota(I32, gates.shape, 1)
    return jnp.sum(jnp.where(lane == col, gates, 0.0), axis=-1, keepdims=True)


def _nsa_cmp(q, kcmp, vcmp, bias_c, cover, gates, *, b, t):
    tq = NSA_TILE
    nb = t // CMP_STRIDE
    n_slc = t // SEL_BLOCK
    n_sel = min(SEL_COUNT, n_slc)
    hpg = NSA_HPG

    def body(q_ref, k_ref, v_ref, bias_ref, cov_ref, g_ref, o_ref, sel_ref):
        g = pl.program_id(1)
        i = pl.program_id(2)
        q2 = q_ref[0].reshape(hpg * tq, HEAD_PAD)
        s3 = _dot_t(q2, k_ref[0, 0]).reshape(hpg, tq, nb) + bias_ref[...]
        p = jnp.exp(s3 - jnp.max(s3, axis=-1, keepdims=True))
        has_key = jnp.where(i * tq + lax.broadcasted_iota(I32, (tq, 1), 0) >= CMP_LEN - 1, 1.0, 0.0)
        p = p * (has_key / jnp.maximum(jnp.sum(p, axis=-1, keepdims=True), 1e-30))
        o3 = _dot(p.reshape(hpg * tq, nb).astype(BF16), v_ref[0, 0]).reshape(hpg, tq, NSA_DV)
        _write_gated_heads(o_ref, o3, g_ref[0], g, 0)

        psum = jnp.sum(p, axis=0)
        p_hi = psum.astype(BF16)
        p_lo = (psum - p_hi.astype(F32)).astype(BF16)
        cov = cov_ref[...]
        imp = _dot(p_hi, cov) + _dot(p_lo, cov)

        jl = lax.broadcasted_iota(I32, (tq, LANES), 1)
        cur = (i * tq + lax.broadcasted_iota(I32, (tq, LANES), 0)) // SEL_BLOCK
        forced = (jl == 0) | (jl == cur) | (jl == cur - 1)
        score = jnp.where(forced, SEL_FORCE, jnp.where(jl <= cur, imp, -1.0))
        n_used = -(-n_slc // 8) * 8
        st = jnp.transpose(score)[:n_used]
        jrow = lax.broadcasted_iota(I32, (n_used, tq), 0)
        rank = jnp.zeros((n_used, tq), F32)
        for kk in range(n_slc):
            row = st[kk:kk + 1, :]
            rank = rank + jnp.where(jrow > kk, jnp.where(row >= st, 1.0, 0.0), jnp.where(row > st, 1.0, 0.0))
        chosen_t = jnp.where((rank < n_sel) & (st >= 0.0), 1.0, 0.0)
        if n_used < LANES:
            chosen_t = jnp.concatenate([chosen_t, jnp.zeros((LANES - n_used, tq), F32)], axis=0)
        sel_ref[0, 0] = jnp.transpose(chosen_t).astype(BF16)

    return _pcall(
        body, name="nsa_cmp_select", grid=(b, NSA_GROUPS, t // tq),
        in_specs=[
            pl.BlockSpec((1, hpg, tq, HEAD_PAD), lambda bi, g, i: (bi, g, i, 0)),
            pl.BlockSpec((1, 1, nb, HEAD_PAD), lambda bi, g, i: (bi, g, 0, 0)),
            pl.BlockSpec((1, 1, nb, NSA_DV), lambda bi, g, i: (bi, g, 0, 0)),
            pl.BlockSpec((hpg, tq, nb), lambda bi, g, i: (g, i, 0)),
            pl.BlockSpec((nb, LANES), lambda bi, g, i: (0, 0)),
            pl.BlockSpec((1, tq, LANES), lambda bi, g, i: (bi, i, 0)),
        ],
        out_specs=[
            pl.BlockSpec((1, tq, hpg * NSA_DV), lambda bi, g, i: (bi, i, g)),
            pl.BlockSpec((1, 1, tq, LANES), lambda bi, g, i: (bi, g, i, 0)),
        ],
        out_shape=[jax.ShapeDtypeStruct((b, t, NSA_HEADS * NSA_DV), BF16),
                   jax.ShapeDtypeStruct((b, NSA_GROUPS, t, LANES), BF16)],
    )(q, kcmp, vcmp, bias_c, cover, gates)


def _write_gated_heads(o_ref, o3, gates_v, g, branch):
    for hh in range(NSA_HPG):
        gc = _gate_col(gates_v, (g * NSA_HPG + hh) * NSA_BRANCHES + branch)
        o_ref[0, :, hh * NSA_DV:(hh + 1) * NSA_DV] = (o3[hh] * gc).astype(BF16)


def _nsa_sel(q, k6, v4, stab, cvec, gates, sel, expand_t, *, b, t):
    tb = NSA_TILE
    hpg = NSA_HPG
    wide = 4 * tb
    span = WINDOW + tb
    back = WINDOW // tb
    parts = 2
    hpp = hpg // parts

    def body(q_ref, k_ref, v_ref, tab_ref, cv_ref, g_ref, sel_ref, et_ref, o_ref):
        g = pl.program_id(1)
        i = pl.program_id(2)
        qs = [q_ref[0, pt * hpp:(pt + 1) * hpp].reshape(hpp * tb, HEAD_PAD) for pt in range(parts)]
        sel_v = sel_ref[0, 0]
        far_end = jnp.maximum(i - back, 0) * tb

        def tile(koff, tk, bias_of_part, in_range, carries):
            k = k_ref[0, 0, pl.ds(koff, tk), :]
            v = v_ref[0, 0, pl.ds(koff, tk), :]
            chosen = _dot_t(sel_v, et_ref[pl.ds(koff, tk), :])
            if in_range is not None:
                chosen = jnp.where(in_range, chosen, 0.0)
            chosen = (chosen > 0.5)[None]
            out = []
            for pt in range(parts):
                s3 = _dot_t(qs[pt], k).reshape(hpp, tb, tk) + bias_of_part(pt)
                s3 = jnp.where(chosen, s3, NEG_INF)
                out.append(_softmax_step(s3.reshape(hpp * tb, tk), v, carries[pt]))
            return tuple(out)

        def far(j, carries):
            koff = pl.multiple_of(j * wide, wide)
            key = koff + lax.broadcasted_iota(I32, (tb, wide), 1)
            return tile(koff, wide, lambda pt: cv_ref[0, pt * hpp:(pt + 1) * hpp][:, None, :], key < far_end, carries)

        n_far = (far_end + wide - 1) // wide
        carries = tuple(_softmax_init(hpp * tb, NSA_DV) for _ in range(parts))
        carries = lax.fori_loop(0, n_far // 2, lambda j, c: far(2 * j + 1, far(2 * j, c)), carries)
        carries = lax.fori_loop(2 * (n_far // 2), n_far, far, carries)
        carries = tile(pl.multiple_of(far_end, tb), span, lambda pt: tab_ref[0, pt * hpp:(pt + 1) * hpp], None, carries)
        o3 = jnp.concatenate([(acc / l).reshape(hpp, tb, NSA_DV) for _, l, acc in carries], axis=0)
        _write_gated_heads(o_ref, o3, g_ref[0], g, 1)

    return _pcall(
        body, name="nsa_sel_attention", grid=(b, NSA_GROUPS, t // tb),
        in_specs=[
            pl.BlockSpec((1, hpg, tb, HEAD_PAD), lambda bi, g, i: (bi, g, i, 0)),
            pl.BlockSpec((1, 1, t, HEAD_PAD), lambda bi, g, i: (bi, NSA_GROUPS + g, 0, 0)),
            pl.BlockSpec((1, 1, t, NSA_DV), lambda bi, g, i: (bi, g, 0, 0)),
            pl.BlockSpec((1, hpg, tb, span), lambda bi, g, i: (jnp.minimum(i, back), g, 0, 0)),
            pl.BlockSpec((1, hpg, wide), lambda bi, g, i: (g, 0, 0)),
            pl.BlockSpec((1, tb, LANES), lambda bi, g, i: (bi, i, 0)),
            pl.BlockSpec((1, 1, tb, LANES), lambda bi, g, i: (bi, g, i, 0)),
            pl.BlockSpec((t, LANES), lambda bi, g, i: (0, 0)),
        ],
        out_specs=pl.BlockSpec((1, tb, hpg * NSA_DV), lambda bi, g, i: (bi, i, g)),
        out_shape=jax.ShapeDtypeStruct((b, t, NSA_HEADS * NSA_DV), BF16),
    )(q, k6, v4, stab, cvec, gates, sel, expand_t)


def _nsa_win(q, k6, v4, wtab, gates, *, b, t):
    tb = NSA_TILE
    hpg = NSA_HPG
    span = WINDOW + tb
    back = WINDOW // tb
    parts = 2
    hpp = hpg // parts

    def body(q_ref, k_ref, v_ref, w_ref, g_ref, o_ref):
        g = pl.program_id(1)
        i = pl.program_id(2)
        koff = pl.multiple_of(jnp.maximum(i - back, 0) * tb, tb)
        k = k_ref[0, 0, pl.ds(koff, span), :]
        v = v_ref[0, 0, pl.ds(koff, span), :]
        o3 = []
        for pt in range(parts):
            q2 = q_ref[0, pt * hpp:(pt + 1) * hpp].reshape(hpp * tb, HEAD_PAD)
            s3 = _dot_t(q2, k).reshape(hpp, tb, span) + w_ref[0, pt * hpp:(pt + 1) * hpp]
            s = s3.reshape(hpp * tb, span)
            p = jnp.exp(s - jnp.max(s, axis=-1, keepdims=True))
            acc = _dot(p.astype(BF16), v)
            o3.append((acc[:, :NSA_DV] / acc[:, NSA_DV:NSA_DV + 1]).reshape(hpp, tb, NSA_DV))
        _write_gated_heads(o_ref, jnp.concatenate(o3, axis=0), g_ref[0], g, 2)

    return _pcall(
        body, name="nsa_win_attention", grid=(b, NSA_GROUPS, t // tb),
        in_specs=[
            pl.BlockSpec((1, hpg, tb, HEAD_PAD), lambda bi, g, i: (bi, g, i, 0)),
            pl.BlockSpec((1, 1, t, HEAD_PAD), lambda bi, g, i: (bi, 2 * NSA_GROUPS + g, 0, 0)),
            pl.BlockSpec((1, 1, t, HEAD_PAD), lambda bi, g, i: (bi, NSA_GROUPS + g, 0, 0)),
            pl.BlockSpec((1, hpg, tb, span), lambda bi, g, i: (jnp.minimum(i, back), g, 0, 0)),
            pl.BlockSpec((1, tb, LANES), lambda bi, g, i: (bi, i, 0)),
        ],
        out_specs=pl.BlockSpec((1, tb, hpg * NSA_DV), lambda bi, g, i: (bi, i, g)),
        out_shape=jax.ShapeDtypeStruct((b, t, NSA_HEADS * NSA_DV), BF16),
    )(q, k6, v4, wtab, gates)


def _router(x, w_hi, w_lo, bias, *, n, d, n_exp, tm):
    def body(x_ref, wh_ref, wl_ref, b_ref, e_ref, gate_ref):
        xv = x_ref[...]
        x_hi = xv.astype(BF16)
        x_lo = (xv - x_hi.astype(F32)).astype(BF16)
        logits = _dot(x_hi, wh_ref[...]) + _dot(x_lo, wh_ref[...]) + _dot(x_hi, wl_ref[...]) + b_ref[...]
        lane = lax.broadcasted_iota(I32, (tm, LANES), 1)
        logits = jnp.where(lane < n_exp, logits, NEG_INF)
        e_out = jnp.zeros((tm, LANES), I32)
        v_out = jnp.full((tm, LANES), NEG_INF, F32)
        for kk in range(TOP_K):
            mx = jnp.max(logits, axis=-1, keepdims=True)
            idx = jnp.min(jnp.where(logits == mx, lane, LANES), axis=-1, keepdims=True)
            e_out = jnp.where(lane == kk, idx, e_out)
            v_out = jnp.where(lane == kk, mx, v_out)
            logits = jnp.where(lane == idx, NEG_INF, logits)
        ex = jnp.where(lane < TOP_K, jnp.exp(v_out - jnp.max(v_out, axis=-1, keepdims=True)), 0.0)
        e_ref[...] = e_out
        gate_ref[...] = ex / jnp.sum(ex, axis=-1, keepdims=True)

    return _pcall(
        body, name="moe_router", grid=(n // tm,),
        in_specs=[pl.BlockSpec((tm, d), lambda i: (i, 0)),
                  pl.BlockSpec((d, LANES), lambda i: (0, 0)),
                  pl.BlockSpec((d, LANES), lambda i: (0, 0)),
                  pl.BlockSpec((1, LANES), lambda i: (0, 0))],
        out_specs=[pl.BlockSpec((tm, LANES), lambda i: (i, 0)), pl.BlockSpec((tm, LANES), lambda i: (i, 0))],
        out_shape=[jax.ShapeDtypeStruct((n, LANES), I32), jax.ShapeDtypeStruct((n, LANES), F32)],
    )(x, w_hi, w_lo, bias)


def _row_copy(src_hbm, row, dst, slot, sem, chunks):
    src = src_hbm.at[pl.ds(pl.multiple_of(row * chunks, chunks), chunks)]
    return pltpu.make_async_copy(src, dst.at[pl.ds(pl.multiple_of(slot * chunks, chunks), chunks)], sem)


def _moe_up(blk_e, n_valid, x_chunks, idx3, w_up, b_up, pick_even, *, layer, n_rows, d, f, tm, tf):
    chunks = d // LANES
    n_blk = n_rows // tm

    def body(be_ref, nv_ref, idx_ref, nxt_ref, x_hbm, w_ref, b_ref, pick_ref, o_ref, buf, xs, sem):
        i = pl.program_id(0)
        nv = nv_ref[0]
        slot = i % 2

        def start_rows(src_idx, s):
            def issue(r8, c):
                for u in range(GATHER_UNROLL):
                    r = r8 * GATHER_UNROLL + u
                    _row_copy(x_hbm, src_idx[0, 0, r], buf.at[s], r, sem.at[s], chunks).start(priority=u % 2)
                return c
            lax.fori_loop(0, tm // GATHER_UNROLL, issue, 0)

        @pl.when(i < nv)
        def _():
            @pl.when(i == 0)
            def _():
                start_rows(idx_ref, 0)

            def wait(r8, c):
                for u in range(GATHER_UNROLL):
                    _row_copy(x_hbm, 0, buf.at[slot], r8 * GATHER_UNROLL + u, sem.at[slot], chunks).wait()
                return c
            lax.fori_loop(0, tm // GATHER_UNROLL, wait, 0)

            @pl.when(i + 1 < nv)
            def _():
                start_rows(nxt_ref, 1 - slot)

            for c in range(chunks):
                xs[:, c * LANES:(c + 1) * LANES] = buf[slot, pl.ds(c, tm, stride=chunks), :].astype(BF16)

            for j in range(f // tf):
                h = _dot(xs[...], w_ref[0, 0, :, 2 * j * tf:2 * (j + 1) * tf]) + b_ref[0, :, 2 * j * tf:2 * (j + 1) * tf]
                g = jnp.minimum(h, SWIGLU_LIMIT)
                glu = g * _sigmoid(SWIGLU_ALPHA * g)
                lin = jnp.clip(h, -SWIGLU_LIMIT, SWIGLU_LIMIT) + 1.0
                pairs = [glu[:, s * LANES:(s + 1) * LANES] * pltpu.roll(lin[:, s * LANES:(s + 1) * LANES], LANES - 1, 1)
                         for s in range(2 * tf // LANES)]
                o_ref[:, j * tf:(j + 1) * tf] = _dot(jnp.concatenate(pairs, axis=1).astype(BF16), pick_ref[...]).astype(BF16)

        @pl.when(i >= nv)
        def _():
            o_ref[...] = jnp.zeros(o_ref.shape, BF16)

    idx_spec = pl.BlockSpec((1, 1, tm), lambda i, be, nv: (i, 0, 0), memory_space=pltpu.SMEM)
    nxt_spec = pl.BlockSpec((1, 1, tm), lambda i, be, nv: (jnp.minimum(i + 1, n_blk - 1), 0, 0), memory_space=pltpu.SMEM)
    return _pcall(
        body, name="moe_up", grid=(n_blk,), prefetch=2,
        in_specs=[idx_spec, nxt_spec, pl.BlockSpec(memory_space=pl.ANY),
                  pl.BlockSpec((1, 1, d, 2 * f), lambda i, be, nv: (layer, be[i], 0, 0)),
                  pl.BlockSpec((1, 1, 2 * f), lambda i, be, nv: (be[i], 0, 0)),
                  pl.BlockSpec((2 * tf, tf), lambda i, be, nv: (0, 0))],
        out_specs=pl.BlockSpec((tm, f), lambda i, be, nv: (i, 0)),
        out_shape=jax.ShapeDtypeStruct((n_rows, f), BF16),
        scratch=[pltpu.VMEM((2, tm * chunks, LANES), F32), pltpu.VMEM((tm, d), BF16), pltpu.SemaphoreType.DMA((2,))],
    )(blk_e, n_valid, idx3, idx3, x_chunks, w_up, b_up, pick_even)


def _moe_down(blk_e, fresh, n_valid, a, w_down, b_down, *, layer, n_rows, d, f, tm):
    chunks = d // LANES

    def body(be_ref, fr_ref, nv_ref, a_ref, w_ref, b_ref, o_ref, wb_ref):
        i = pl.program_id(0)

        @pl.when(fr_ref[i] == 1)
        def _():
            wb_ref[...] = w_ref[0, 0].astype(BF16)

        @pl.when(i < nv_ref[0])
        def _():
            res = _dot(a_ref[...], wb_ref[...]) + b_ref[0]
            for c in range(chunks):
                o_ref[pl.ds(c, tm, stride=chunks), :] = res[:, c * LANES:(c + 1) * LANES]

        @pl.when(i >= nv_ref[0])
        def _():
            o_ref[...] = jnp.zeros(o_ref.shape, F32)

    return _pcall(
        body, name="moe_down", grid=(n_rows // tm,), prefetch=3,
        in_specs=[pl.BlockSpec((tm, f), lambda i, be, fr, nv: (i, 0)),
                  pl.BlockSpec((1, 1, f, d), lambda i, be, fr, nv: (layer, be[i], 0, 0)),
                  pl.BlockSpec((1, 1, d), lambda i, be, fr, nv: (be[i], 0, 0))],
        out_specs=pl.BlockSpec((tm * chunks, LANES), lambda i, be, fr, nv: (i, 0)),
        out_shape=jax.ShapeDtypeStruct((n_rows * chunks, LANES), F32),
        scratch=[pltpu.VMEM((f, d), BF16)],
    )(blk_e, fresh, n_valid, a, w_down, b_down)


def _moe_combine(pos3, ys, gates, x, ln_g, ln_b, *, n, d, tq, alpha):
    chunks = d // LANES
    n_tiles = n // tq

    def body(pos_ref, nxt_ref, y_hbm, g_ref, x_ref, lg_ref, lb_ref, of_ref, ob_ref, buf, y_ref, sem):
        i = pl.program_id(0)
        slot = i % 2

        def start_rows(src_pos, s):
            def issue(r, c):
                for kk in range(TOP_K):
                    _row_copy(y_hbm, src_pos[0, 0, kk * tq + r], buf.at[s, kk], r, sem.at[s], chunks).start(priority=kk % 2)
                return c
            lax.fori_loop(0, tq, issue, 0)

        @pl.when(i == 0)
        def _():
            start_rows(pos_ref, 0)

        def wait(r, c):
            for kk in range(TOP_K):
                _row_copy(y_hbm, 0, buf.at[slot, kk], r, sem.at[slot], chunks).wait()
            return c
        lax.fori_loop(0, tq, wait, 0)

        @pl.when(i + 1 < n_tiles)
        def _():
            start_rows(nxt_ref, 1 - slot)

        gv = g_ref[...]
        for c in range(chunks):
            piece = gv[:, 0:1] * buf[slot, 0, pl.ds(c, tq, stride=chunks), :]
            for kk in range(1, TOP_K):
                piece = piece + gv[:, kk:kk + 1] * buf[slot, kk, pl.ds(c, tq, stride=chunks), :]
            y_ref[:, c * LANES:(c + 1) * LANES] = piece
        out = _layer_norm(alpha * x_ref[...] + y_ref[...], lg_ref[...], lb_ref[...])
        of_ref[...] = out
        ob_ref[...] = out.astype(BF16)

    return _pcall(
        body, name="moe_combine_ln", grid=(n_tiles,),
        in_specs=[pl.BlockSpec((1, 1, TOP_K * tq), lambda i: (i, 0, 0), memory_space=pltpu.SMEM),
                  pl.BlockSpec((1, 1, TOP_K * tq), lambda i: (jnp.minimum(i + 1, n_tiles - 1), 0, 0), memory_space=pltpu.SMEM),
                  pl.BlockSpec(memory_space=pl.ANY),
                  pl.BlockSpec((tq, LANES), lambda i: (i, 0)),
                  pl.BlockSpec((tq, d), lambda i: (i, 0)),
                  pl.BlockSpec((1, d), lambda i: (0, 0)),
                  pl.BlockSpec((1, d), lambda i: (0, 0))],
        out_specs=[pl.BlockSpec((tq, d), lambda i: (i, 0)), pl.BlockSpec((tq, d), lambda i: (i, 0))],
        out_shape=[jax.ShapeDtypeStruct((n, d), F32), jax.ShapeDtypeStruct((n, d), BF16)],
        scratch=[pltpu.VMEM((2, TOP_K, tq * chunks, LANES), F32), pltpu.VMEM((tq, d), F32), pltpu.SemaphoreType.DMA((2,))],
    )(pos3, pos3, ys, gates, x, ln_g, ln_b)


def _t5_bucket(dist):
    max_exact = REL_BUCKETS // 2
    dd = jnp.maximum(dist, 1).astype(F32)
    large = max_exact + (jnp.log(dd / max_exact) / math.log(REL_MAX_DIST / max_exact)
                         * (REL_BUCKETS - max_exact)).astype(I32)
    large = jnp.minimum(large, REL_BUCKETS - 1)
    return jnp.where(dist < max_exact, dist, large)


def _tables(positions, rel_bias, t):
    half = MLA_ROPE // 2
    inv = 1.0 / (ROPE_THETA ** (jnp.arange(0, MLA_ROPE, 2, dtype=F32) / MLA_ROPE))
    ang = positions.astype(F32)[:, None] * inv[None, :]
    zeros = jnp.zeros((t, LANES - MLA_ROPE), F32)
    cos = jnp.concatenate([jnp.cos(ang), jnp.cos(ang), zeros], axis=-1)
    sin = jnp.concatenate([jnp.sin(ang), jnp.sin(ang), zeros], axis=-1)
    assert half * 2 == MLA_ROPE

    first = jnp.sum(_t5_bucket(jnp.arange(t))[None, :] < jnp.arange(REL_BUCKETS)[:, None], axis=1).astype(I32)
    rb = rel_bias.astype(F32)

    def bias_of(dist, valid):
        out = jnp.broadcast_to(rb[0].reshape((NSA_HEADS,) + (1,) * dist.ndim), (NSA_HEADS,) + dist.shape)
        for bkt in range(1, REL_BUCKETS):
            out = jnp.where((dist >= first[bkt])[None], rb[bkt].reshape((NSA_HEADS,) + (1,) * dist.ndim), out)
        return jnp.where(valid[None], out, NEG_INF)

    tb = NSA_TILE
    sat = rb[REL_BUCKETS - 1]
    cvec = jnp.broadcast_to(sat.reshape(NSA_GROUPS, NSA_HPG, 1), (NSA_GROUPS, NSA_HPG, 4 * tb))
    span = WINDOW + tb
    wd = (jnp.arange(WINDOW // tb + 1, dtype=I32) * tb)[:, None, None] + jnp.arange(tb, dtype=I32)[None, :, None] \
        - jnp.arange(span, dtype=I32)[None, None, :]
    wtab = bias_of(wd, (wd >= 0) & (wd < WINDOW)).transpose(1, 0, 2, 3)
    stab = bias_of(wd, wd >= 0).transpose(1, 0, 2, 3)
    nb = t // CMP_STRIDE
    dist_c = jnp.arange(t, dtype=I32)[:, None] - (jnp.arange(nb, dtype=I32) * CMP_STRIDE + CMP_LEN - 1)[None, :]
    bias_c = bias_of(dist_c, dist_c >= 0)
    tq_mla = _pick_tile(t, 512)
    cm = jnp.arange(tq_mla, dtype=I32)
    causal = jnp.where(cm[None, :] <= cm[:, None], 0.0, NEG_INF).astype(F32)

    n_slc = t // SEL_BLOCK
    tok = np.arange(nb)[:, None] * CMP_STRIDE + np.arange(CMP_LEN)[None, :]
    cover = (tok[:, :, None] // SEL_BLOCK == np.arange(LANES)[None, None, :]).sum(1) / CMP_LEN
    cover[nb - 1] = 0.0
    cover[:, n_slc:] = 0.0
    expand_t = (np.arange(t)[:, None] // SEL_BLOCK == np.arange(LANES)[None, :])
    return dict(cos=cos, sin=sin, stab=stab, cvec=cvec, wtab=wtab, bias_c=bias_c, causal=causal,
                cover=jnp.asarray(cover, BF16), expand_t=jnp.asarray(expand_t, BF16))


def _pad_heads(w, heads, width):
    k = w.shape[0]
    w = w.reshape(k, heads, width)
    return jnp.pad(w, ((0, 0), (0, 0), (0, HEAD_PAD - width))).reshape(k, heads * HEAD_PAD)


def _pick_tile(total, want):
    tile = min(total, want)
    while total % tile:
        tile //= 2
    return tile


def _token_mixer(x, xb, tabs, p, *, b, t, d):
    n = b * t
    tm = _pick_tile(t, 512)
    tpb = t // tm
    offs = np.cumsum([0, MLA_Q_LORA, MLA_KV_LORA, MLA_ROPE, NSA_HEADS * NSA_DK,
                      NSA_GROUPS * NSA_DK, NSA_GROUPS * NSA_DV, NSA_GROUPS * NSA_DK, NSA_GROUPS * NSA_DV,
                      NSA_GROUPS * NSA_DK, NSA_GROUPS * NSA_DV, NSA_HEADS * NSA_BRANCHES, 2 * d])
    w_in = p["w_in"]
    seg = [w_in[:, offs[s]:offs[s + 1]] for s in range(12)]
    x_lhs = [(xb, _row_spec(tm, d))]
    mla_scale = (MLA_NOPE + MLA_ROPE) ** -0.5
    nsa_scale = NSA_DK ** -0.5

    def cast_epi(acc, er, orr):
        orr[0][...] = acc.astype(BF16)

    lat_w = MLA_Q_LORA + MLA_KV_LORA
    w_lat = jnp.concatenate([seg[0], seg[1]], axis=1).astype(BF16)
    tn_lat = _pick_tile(lat_w, 1024)
    (lat,) = _matmul("proj_latents", x_lhs, w_lat, [], [(jax.ShapeDtypeStruct((n, lat_w), BF16), _tile_spec(tm, tn_lat))],
                     _cast_prologue, cast_epi, m=n, k=d, n=lat_w, tm=tm, tn=tn_lat)

    w_small = jnp.concatenate([seg[2], jnp.zeros((d, LANES - MLA_ROPE), F32), seg[10],
                               jnp.zeros((d, LANES - NSA_HEADS * NSA_BRANCHES), F32)], axis=1).astype(BF16)

    def small_epi(acc, er, orr):
        orr[0][...] = _rope_slab(acc[:, :LANES], er[0][...], er[1][...]).astype(BF16)
        orr[1][...] = _sigmoid(acc[:, LANES:])

    cs_extras = [(tabs["cos_n"], _row_spec(tm, LANES)), (tabs["sin_n"], _row_spec(tm, LANES))]
    kr, nsa_gates = _matmul(
        "proj_rope_key_gates", x_lhs, w_small, cs_extras,
        [(jax.ShapeDtypeStruct((n, LANES), BF16), _row_spec(tm, LANES)),
         (jax.ShapeDtypeStruct((n, LANES), F32), _row_spec(tm, LANES))],
        _cast_prologue, small_epi, m=n, k=d, n=2 * LANES, tm=tm, tn=2 * LANES)

    hb = 2
    hq = 4
    w_qn = _pad_heads(seg[3], NSA_HEADS, NSA_DK).astype(BF16)
    (q_nsa,) = _matmul("proj_nsa_q", x_lhs, w_qn, [],
                       [(jax.ShapeDtypeStruct((b, NSA_HEADS, t, HEAD_PAD), BF16), _head_spec(tm, hq, HEAD_PAD, tpb))],
                       _cast_prologue, _heads_epilogue(hq, HEAD_PAD, nsa_scale),
                       m=n, k=d, n=NSA_HEADS * HEAD_PAD, tm=tm, tn=hq * HEAD_PAD)

    w_k6 = jnp.concatenate([_pad_heads(seg[s], NSA_GROUPS, NSA_DK) for s in (4, 6, 8)], axis=1).astype(BF16)
    (k6,) = _matmul("proj_nsa_k", x_lhs, w_k6, [],
                    [(jax.ShapeDtypeStruct((b, 3 * NSA_GROUPS, t, HEAD_PAD), BF16), _head_spec(tm, hb, HEAD_PAD, tpb))],
                    _cast_prologue, _heads_epilogue(hb, HEAD_PAD),
                    m=n, k=d, n=3 * NSA_GROUPS * HEAD_PAD, tm=tm, tn=hb * HEAD_PAD)

    (v_c,) = _matmul("proj_nsa_v_cmp", x_lhs, seg[5].astype(BF16), [],
                     [(jax.ShapeDtypeStruct((b, NSA_GROUPS, t, NSA_DV), BF16), _head_spec(tm, hb, NSA_DV, tpb))],
                     _cast_prologue, _heads_epilogue(hb, NSA_DV),
                     m=n, k=d, n=NSA_GROUPS * NSA_DV, tm=tm, tn=hb * NSA_DV)

    def ones_lane(rows):
        return jnp.where(lax.broadcasted_iota(I32, (rows, HEAD_PAD - NSA_DV), 1) == 0, 1.0, 0.0).astype(BF16)

    def v_ones_epi(acc, er, orr):
        for c in range(hb):
            orr[0][0, c, :, :NSA_DV] = acc[:, c * NSA_DV:(c + 1) * NSA_DV].astype(BF16)
            orr[0][0, c, :, NSA_DV:] = ones_lane(acc.shape[0])

    w_v4 = jnp.concatenate([seg[7], seg[9]], axis=1).astype(BF16)
    (v4,) = _matmul("proj_nsa_v", x_lhs, w_v4, [],
                    [(jax.ShapeDtypeStruct((b, 2 * NSA_GROUPS, t, HEAD_PAD), BF16), _head_spec(tm, hb, HEAD_PAD, tpb))],
                    _cast_prologue, v_ones_epi, m=n, k=d, n=2 * NSA_GROUPS * NSA_DV, tm=tm, tn=hb * NSA_DV)

    def sig_epi(acc, er, orr):
        orr[0][...] = _sigmoid(acc).astype(BF16)

    tn_d = _pick_tile(d, 1024)
    (merge,) = _matmul("proj_merge_gates", x_lhs, seg[11].astype(BF16), [],
                       [(jax.ShapeDtypeStruct((n, 2 * d), BF16), _tile_spec(tm, tn_d))],
                       _cast_prologue, sig_epi, m=n, k=d, n=2 * d, tm=tm, tn=tn_d)

    w_uq = p["mla_w_uq"].reshape(MLA_Q_LORA, MLA_HEADS, MLA_NOPE + MLA_ROPE)
    w_uq = jnp.pad(w_uq, ((0, 0), (0, 0), (0, HEAD_PAD - MLA_NOPE - MLA_ROPE))).reshape(MLA_Q_LORA, MLA_HEADS * HEAD_PAD)

    def q_epi(acc, er, orr):
        cos, sin = er[0][...], er[1][...]
        for c in range(hq):
            orr[0][0, c, :, :MLA_NOPE] = (acc[:, c * HEAD_PAD:c * HEAD_PAD + MLA_NOPE] * mla_scale).astype(BF16)
            slab = acc[:, c * HEAD_PAD + MLA_NOPE:(c + 1) * HEAD_PAD] * mla_scale
            orr[0][0, c, :, MLA_NOPE:] = _rope_slab(slab, cos, sin).astype(BF16)

    (q_mla,) = _matmul(
        "mla_q_up", [(lat, _row_spec(tm, MLA_Q_LORA)), (p["mla_q_norm"].reshape(1, -1), pl.BlockSpec((1, MLA_Q_LORA), lambda i, j: (0, 0)))],
        w_uq.astype(BF16), cs_extras,
        [(jax.ShapeDtypeStruct((b, MLA_HEADS, t, HEAD_PAD), BF16), _head_spec(tm, hq, HEAD_PAD, tpb))],
        _rms_prologue, q_epi, m=n, k=MLA_Q_LORA, n=MLA_HEADS * HEAD_PAD, tm=tm, tn=hq * HEAD_PAD)

    kv_w = MLA_NOPE + MLA_V

    def kv_epi(acc, er, orr):
        k_rope = er[0][...]
        for c in range(hq):
            orr[0][0, c, :, :MLA_NOPE] = acc[:, c * kv_w:c * kv_w + MLA_NOPE].astype(BF16)
            orr[0][0, c, :, MLA_NOPE:] = k_rope
            orr[1][0, c, :, :MLA_V] = acc[:, c * kv_w + MLA_NOPE:(c + 1) * kv_w].astype(BF16)
            orr[1][0, c, :, MLA_V:] = ones_lane(acc.shape[0])

    k_mla, v_mla = _matmul(
        "mla_kv_up", [(lat, _row_spec(tm, MLA_KV_LORA, MLA_Q_LORA // MLA_KV_LORA)),
                      (p["mla_kv_norm"].reshape(1, -1), pl.BlockSpec((1, MLA_KV_LORA), lambda i, j: (0, 0)))],
        p["mla_w_ukv"].astype(BF16), [(kr, _row_spec(tm, LANES))],
        [(jax.ShapeDtypeStruct((b, MLA_HEADS, t, HEAD_PAD), BF16), _head_spec(tm, hq, HEAD_PAD, tpb)),
         (jax.ShapeDtypeStruct((b, MLA_HEADS, t, HEAD_PAD), BF16), _head_spec(tm, hq, HEAD_PAD, tpb))],
        _rms_prologue, kv_epi, m=n, k=MLA_KV_LORA, n=MLA_HEADS * kv_w, tm=tm, tn=hq * kv_w)

    attn_a = _mla_attention(q_mla, k_mla, v_mla, tabs["causal"], b=b, t=t, tq=tabs["causal"].shape[0])

    def ya_epi(acc, er, orr):
        orr[0][...] = (er[0][...].astype(F32) * acc).astype(BF16)

    (y_a,) = _matmul("mla_branch_out", [(attn_a.reshape(n, MLA_HEADS * MLA_V), _row_spec(tm, MLA_HEADS * MLA_V))],
                     p["w_branch_a"].astype(BF16), [(merge, _tile_spec(tm, tn_d))],
                     [(jax.ShapeDtypeStruct((n, d), BF16), _tile_spec(tm, tn_d))],
                     _cast_prologue, ya_epi, m=n, k=MLA_HEADS * MLA_V, n=d, tm=tm, tn=tn_d)

    def cmp_weights(pe, w1, w2, width_in, width):
        pe_p = jnp.pad(pe, ((0, 0), (0, width - width_in)))
        pe2 = jnp.zeros((8, CMP_STRIDE * width), F32).at[0].set(pe_p[:CMP_STRIDE].reshape(-1)).at[1].set(pe_p[CMP_STRIDE:].reshape(-1))
        w1p = jnp.pad(w1.reshape(CMP_LEN, width_in, width_in), ((0, 0), (0, width - width_in), (0, width - width_in)))
        w2p = jnp.pad(w2, ((0, width - width_in), (0, width - width_in)))
        return (pe2.astype(BF16), w1p[:CMP_STRIDE].reshape(-1, width).astype(BF16),
                w1p[CMP_STRIDE:].reshape(-1, width).astype(BF16), w2p.astype(BF16))

    kcmp = _compress(k6, *cmp_weights(p["cmp_pe_k"], p["cmp_w1_k"], p["cmp_w2_k"], NSA_DK, HEAD_PAD), b=b, t=t, width=HEAD_PAD)
    vcmp = _compress(v_c, *cmp_weights(p["cmp_pe_v"], p["cmp_w1_v"], p["cmp_w2_v"], NSA_DV, NSA_DV), b=b, t=t, width=NSA_DV)
    gates3 = nsa_gates.reshape(b, t, LANES)
    o_cmp, sel = _nsa_cmp(q_nsa, kcmp, vcmp, tabs["bias_c"], tabs["cover"], gates3, b=b, t=t)
    o_slc = _nsa_sel(q_nsa, k6, v4, tabs["stab"], tabs["cvec"], gates3, sel, tabs["expand_t"], b=b, t=t)
    o_win = _nsa_win(q_nsa, k6, v4, tabs["wtab"], gates3, b=b, t=t)

    def sum3_prologue(a_ref, b_ref, c_ref):
        return (a_ref[...].astype(F32) + b_ref[...].astype(F32) + c_ref[...].astype(F32)).astype(BF16)

    def merge_epi(acc, er, orr):
        orr[0][...] = (er[0][...].astype(F32) + er[1][...].astype(F32) * acc).astype(BF16)

    hv = NSA_HEADS * NSA_DV
    (merged,) = _matmul("nsa_branch_out_merge", [(o.reshape(n, hv), _row_spec(tm, hv)) for o in (o_cmp, o_slc, o_win)],
                        p["w_branch_b"].astype(BF16), [(y_a, _tile_spec(tm, tn_d)), (merge, _tile_spec(tm, tn_d, d // tn_d))],
                        [(jax.ShapeDtypeStruct((n, d), BF16), _tile_spec(tm, tn_d))],
                        sum3_prologue, merge_epi, m=n, k=hv, n=d, tm=tm, tn=tn_d)

    tm_ln = _pick_tile(t, 256)
    alpha = p["alpha"]

    chunks = d // LANES

    def ln_epi(acc, er, orr):
        out = _layer_norm(alpha * er[0][...] + acc, er[1][...], er[2][...])
        orr[0][...] = out
        for c in range(chunks):
            orr[1][pl.ds(c, tm_ln, stride=chunks), :] = out[:, c * LANES:(c + 1) * LANES]

    vec_spec = pl.BlockSpec((1, d), lambda i, j: (0, 0))
    x1, x1_chunks = _matmul(
        "mixer_out_ln", [(merged, _row_spec(tm_ln, d))], p["w_out"].astype(BF16),
        [(x, _row_spec(tm_ln, d)), (p["ln1_g"].reshape(1, d), vec_spec), (p["ln1_b"].reshape(1, d), vec_spec)],
        [(jax.ShapeDtypeStruct((n, d), F32), _row_spec(tm_ln, d)),
         (jax.ShapeDtypeStruct((n * chunks, LANES), F32), pl.BlockSpec((tm_ln * chunks, LANES), lambda i, j: (i, 0)))],
        _cast_prologue, ln_epi, m=n, k=d, n=d, tm=tm_ln, tn=d)
    return x1, x1_chunks


def _moe(x, x_chunks, p, *, n, d):
    _, n_exp, _, f2 = p["exp_w_up"].shape
    f = f2 // 2
    tm_e = p["tm_e"]
    rw = jnp.pad(p["router_w"], ((0, 0), (0, LANES - n_exp)))
    rw_hi = rw.astype(BF16)
    rw_lo = (rw - rw_hi.astype(F32)).astype(BF16)
    rb = jnp.pad(p["router_b"], (0, LANES - n_exp)).reshape(1, LANES)
    top_e, gates = _router(x, rw_hi, rw_lo, rb, n=n, d=d, n_exp=n_exp, tm=_pick_tile(n, 512))

    n_pairs = n * TOP_K
    e_flat = top_e[:, :TOP_K].reshape(-1)
    grp = LANES
    onehot = (e_flat[:, None] == jnp.arange(n_exp, dtype=I32)[None, :]).astype(F32).reshape(n_pairs // grp, grp, n_exp)
    tri = jnp.tril(jnp.ones((grp, grp), F32))
    within = jnp.einsum("ij,gje->gie", tri, onehot)
    totals = within[:, -1, :]
    before = jnp.cumsum(totals, axis=0) - totals
    rank = (jnp.sum((within + before[:, None, :]) * onehot, axis=-1).reshape(-1) - 1.0).astype(I32)
    counts = jnp.sum(totals, axis=0).astype(I32)
    padded = (counts + tm_e - 1) // tm_e * tm_e
    pad_end = jnp.cumsum(padded)
    dest = (pad_end - padded)[e_flat] + rank
    n_rows = n_pairs + n_exp * tm_e
    n_blk = n_rows // tm_e
    row_tok = jnp.zeros((n_rows,), I32).at[dest].set(jnp.arange(n_pairs, dtype=I32) // TOP_K)
    n_valid = (pad_end[-1] // tm_e).astype(I32)
    blk_e = jnp.sum(pad_end[None, :] <= (jnp.arange(n_blk, dtype=I32) * tm_e)[:, None], axis=1).astype(I32)
    blk_e = jnp.minimum(blk_e, blk_e[jnp.maximum(n_valid - 1, 0)])
    n_valid = n_valid.reshape(1)

    tf = _pick_tile(f, 512)
    pick_even = jnp.asarray(np.arange(2 * tf)[:, None] == 2 * np.arange(tf)[None, :], BF16)
    a = _moe_up(blk_e, n_valid, x_chunks, row_tok.reshape(n_blk, 1, tm_e),
                p["exp_w_up_bf16"], p["exp_b_up"].reshape(n_exp, 1, 2 * f), pick_even,
                layer=p["layer"], n_rows=n_rows, d=d, f=f, tm=tm_e, tf=tf)
    split = max(1, tm_e // 256)
    tm_d = tm_e // split
    blk_d = jnp.repeat(blk_e, split)
    fresh = jnp.concatenate([jnp.ones((1,), I32), (blk_d[1:] != blk_d[:-1]).astype(I32)])
    ys = _moe_down(blk_d, fresh, n_valid * split, a, p["exp_w_down"], p["exp_b_down"].reshape(n_exp, 1, d),
                   layer=p["layer"], n_rows=n_rows, d=d, f=f, tm=tm_d)

    tq = _pick_tile(n, 128)
    pos3 = dest.reshape(n // tq, tq, TOP_K).transpose(0, 2, 1).reshape(n // tq, 1, TOP_K * tq)
    return _moe_combine(pos3, ys, gates, x, p["ln2_g"].reshape(1, d), p["ln2_b"].reshape(1, d),
                        n=n, d=d, tq=tq, alpha=p["alpha"])


def kernel(x, positions, rel_bias, w_in, mla_q_norm, mla_w_uq, mla_kv_norm, mla_w_ukv, cmp_pe_k, cmp_w1_k, cmp_w2_k, cmp_pe_v, cmp_w1_v, cmp_w2_v, w_branch_a, w_branch_b, w_out, ln1_g, ln1_b, router_w, router_b, exp_w_up, exp_b_up, exp_w_down, exp_b_down, ln2_g, ln2_b):
    b, t, d = x.shape
    depth = w_in.shape[0]
    n = b * t
    alpha = (2 * depth) ** 0.25
    tabs = _tables(positions, rel_bias, t)
    tabs["cos_n"] = jnp.tile(tabs["cos"], (b, 1))
    tabs["sin_n"] = jnp.tile(tabs["sin"], (b, 1))
    xf = x.reshape(n, d)
    xb = xf.astype(BF16)
    exp_w_up_bf16 = exp_w_up.astype(BF16)
    for l in range(depth):
        p = dict(w_in=w_in[l], mla_q_norm=mla_q_norm[l], mla_w_uq=mla_w_uq[l], mla_kv_norm=mla_kv_norm[l],
                 mla_w_ukv=mla_w_ukv[l], cmp_pe_k=cmp_pe_k[l], cmp_w1_k=cmp_w1_k[l], cmp_w2_k=cmp_w2_k[l],
                 cmp_pe_v=cmp_pe_v[l], cmp_w1_v=cmp_w1_v[l], cmp_w2_v=cmp_w2_v[l], w_branch_a=w_branch_a[l],
                 w_branch_b=w_branch_b[l], w_out=w_out[l], ln1_g=ln1_g[l], ln1_b=ln1_b[l], router_w=router_w[l],
                 router_b=router_b[l], exp_w_up=exp_w_up, exp_w_up_bf16=exp_w_up_bf16, exp_b_up=exp_b_up[l], exp_w_down=exp_w_down, layer=l,
                 exp_b_down=exp_b_down[l], ln2_g=ln2_g[l], ln2_b=ln2_b[l], alpha=alpha, tm_e=_pick_tile(n * TOP_K, 512))
        x1, x1_chunks = _token_mixer(xf, xb, tabs, p, b=b, t=t, d=d)
        xf, xb = _moe(x1, x1_chunks, p, n=n, d=d)
    return xf.reshape(b, t, d)
```
